```python
import jax, jax.numpy as jnp
from jax import lax
import numpy as np

D_MODEL = 1024
BATCH = 4
SEQ = 8192
DEPTH = 2

N_HEADS_A = 8
HEAD_DIM_A = 64
KV_LATENT = 256
IDX_HEADS = 8
IDX_DIM = 64
TOPK_MAX = 256
Q_BLOCK = 128
SSM_D_INNER = 1024
SSM_HEADS = 16
SSM_HEAD_DIM = SSM_D_INNER // SSM_HEADS
SSM_GROUPS = 2
SSM_STATE = 128
SSM_CONV = 4
SSM_CHUNK = 128
SC_WIDTH = D_MODEL
SC_CONV = 3
D_FF = -(-8 * D_MODEL // (3 * 256)) * 256
EPS = 1e-6

A_SIZES = (N_HEADS_A * HEAD_DIM_A, KV_LATENT, IDX_HEADS * IDX_DIM, IDX_DIM, IDX_HEADS)
B_SIZES = (SSM_D_INNER, SSM_D_INNER + 2 * SSM_GROUPS * SSM_STATE, SSM_HEADS)
D_IN_EVEN = sum(A_SIZES) + sum(B_SIZES)
D_OUT_EVEN = N_HEADS_A * HEAD_DIM_A + SSM_D_INNER

kernel_name = "hybrid_dsa_ssd_shortconv_trunk"


def _rmsnorm(x, w):
    xf = x.astype(jnp.float32)
    y = xf * lax.rsqrt(jnp.mean(xf * xf, axis=-1, keepdims=True) + EPS)
    return y.astype(x.dtype) * w


def _split(h, sizes):
    cuts = [int(c) for c in np.cumsum(sizes)[:-1]]
    return jnp.split(h, cuts, axis=-1)


def _causal_dwconv(x, w):
    K = w.shape[0]
    return lax.conv_general_dilated(
        x, w[:, None, :], window_strides=(1,), padding=[(K - 1, 0)],
        dimension_numbers=("NWC", "WIO", "NWC"), feature_group_count=x.shape[-1])


def _dsa_attention(q, ckv, iq, ik, iw, w_uk, w_uv):
    b, S, _ = q.shape
    topk = min(TOPK_MAX, S // 4)
    nb = S // Q_BLOCK
    qb = jnp.moveaxis(q.reshape(b, nb, Q_BLOCK, N_HEADS_A, HEAD_DIM_A), 1, 0)
    iqb = jnp.moveaxis(iq.reshape(b, nb, Q_BLOCK, IDX_HEADS, IDX_DIM), 1, 0)
    iwb = jnp.moveaxis(iw.reshape(b, nb, Q_BLOCK, IDX_HEADS), 1, 0) * (IDX_HEADS ** -0.5)
    t0s = jnp.arange(nb, dtype=jnp.int32) * Q_BLOCK
    s_pos = jnp.arange(S, dtype=jnp.int32)

    def block(args):
        q_blk, iq_blk, iw_blk, t0 = args
        t_pos = t0 + jnp.arange(Q_BLOCK, dtype=jnp.int32)
        causal = s_pos[None, :] <= t_pos[:, None]
        il = jnp.einsum('bthd,bsd->bths', iq_blk, ik) * (IDX_DIM ** -0.5)
        score = jnp.einsum('bths,bth->bts', jax.nn.relu(il), iw_blk)
        score = jnp.where(causal[None], score, -jnp.inf)
        _, sel = lax.top_k(score, topk)
        c_sel = jax.vmap(lambda c, i: c[i])(ckv, sel)
        q_lat = jnp.einsum('bthd,hdc->bthc', q_blk, w_uk)
        logits = jnp.einsum('bthc,btkc->bthk', q_lat, c_sel).astype(jnp.float32) * (HEAD_DIM_A ** -0.5)
        valid = sel <= t_pos[None, :, None]
        logits = jnp.where(valid[:, :, None, :], logits, -jnp.inf)
        p = jax.nn.softmax(logits, axis=-1).astype(c_sel.dtype)
        o_lat = jnp.einsum('bthk,btkc->bthc', p, c_sel)
        o = jnp.einsum('bthc,hcd->bthd', o_lat, w_uv)
        return o.reshape(b, Q_BLOCK, N_HEADS_A * HEAD_DIM_A)

    out = lax.map(block, (qb, iqb, iwb, t0s))
    return jnp.moveaxis(out, 0, 1).reshape(b, S, N_HEADS_A * HEAD_DIM_A)


def _ssd_chunked(xs, dt, A, Bm, Cm):
    b, S, h, p = xs.shape
    g, n = Bm.shape[2], Bm.shape[3]
    r = h // g
    L = SSM_CHUNK
    c = S // L
    a = (dt * A).reshape(b, c, L, g, r)
    xdt = (xs * dt[..., None]).reshape(b, c, L, g, r, p)
    Bc = Bm.reshape(b, c, L, g, n)
    Cc = Cm.reshape(b, c, L, g, n)
    a_cs = jnp.cumsum(a, axis=2)
    tri = jnp.tril(jnp.ones((L, L), dtype=bool))
    seg = a_cs[:, :, :, None] - a_cs[:, :, None, :]
    decay = jnp.exp(jnp.where(tri[:, :, None, None], seg, -jnp.inf))
    cb = jnp.einsum('bclgn,bcsgn->bclsg', Cc, Bc)
    y_diag = jnp.einsum('bclsgr,bcsgrp->bclgrp', cb[..., None] * decay, xdt)
    decay_end = jnp.exp(a_cs[:, :, -1:] - a_cs)
    states = jnp.einsum('bclgn,bclgr,bclgrp->bcgrpn', Bc, decay_end, xdt)
    chunk_decay = jnp.exp(a_cs[:, :, -1])

    def step(hs, inp):
        dec, st = inp
        return hs * dec[..., None, None] + st, hs

    _, prev = lax.scan(step, jnp.zeros_like(states[:, 0]),
                       (jnp.moveaxis(chunk_decay, 1, 0), jnp.moveaxis(states, 1, 0)))
    prev = jnp.moveaxis(prev, 0, 1)
    y_off = jnp.einsum('bclgn,bcgrpn,bclgr->bclgrp', Cc, prev, jnp.exp(a_cs))
    return (y_diag + y_off).reshape(b, S, h, p)


def _mamba2(z, xbc, dt, conv_w, conv_b, dt_bias, A_log, D, norm_w):
    b, S, _ = z.shape
    xbc = jax.nn.silu(_causal_dwconv(xbc, conv_w) + conv_b)
    xs, Bm, Cm = _split(xbc, (SSM_D_INNER, SSM_GROUPS * SSM_STATE, SSM_GROUPS * SSM_STATE))
    xs = xs.reshape(b, S, SSM_HEADS, SSM_HEAD_DIM)
    Bm = Bm.reshape(b, S, SSM_GROUPS, SSM_STATE)
    Cm = Cm.reshape(b, S, SSM_GROUPS, SSM_STATE)
    dt = jax.nn.softplus(dt + dt_bias)
    A = -jnp.exp(A_log)
    y = _ssd_chunked(xs, dt, A, Bm, Cm) + xs * D[:, None]
    y = y.reshape(b, S, SSM_D_INNER) * jax.nn.silu(z)
    y = _rmsnorm(y.reshape(b, S, SSM_GROUPS, SSM_D_INNER // SSM_GROUPS),
                 norm_w.reshape(SSM_GROUPS, SSM_D_INNER // SSM_GROUPS))
    return y.reshape(b, S, SSM_D_INNER)


def _parallel_mixer_layer(x, attn_norm, in_w, kv_norm, w_uk, w_uv, conv_w, conv_b,
                          dt_bias, A_log, D, ssm_norm, out_w):
    h = _rmsnorm(x, attn_norm) @ in_w
    q, ckv, iq, ik, iw, z, xbc, dt = _split(h, A_SIZES + B_SIZES)
    y_a = _dsa_attention(q, _rmsnorm(ckv, kv_norm), iq, ik, iw, w_uk, w_uv)
    y_b = _mamba2(z, xbc, dt, conv_w, conv_b, dt_bias, A_log, D, ssm_norm)
    return x + jnp.concatenate([y_a, y_b], axis=-1) @ out_w


def _short_conv_layer(x, norm_w, in_w, conv_w, out_w):
    h = _rmsnorm(x, norm_w) @ in_w
    gate_b, gate_c, v = _split(h, (SC_WIDTH, SC_WIDTH, SC_WIDTH))
    y = gate_b * _causal_dwconv(gate_c * v, conv_w)
    return x + y @ out_w


def _swiglu_ffn(x, norm_w, w_gate, w_up, w_down):
    h = _rmsnorm(x, norm_w)
    return x + (jax.nn.silu(h @ w_gate) * (h @ w_up)) @ w_down


def setup_inputs(seed: int = 0) -> dict:
    key = jax.random.key(seed)
    ks = jax.random.split(key, 32)

    def nrm(k, shape, fan_in):
        return jax.random.normal(k, shape, jnp.float32) * (fan_in ** -0.5)

    def gain(k, n):
        return 1.0 + 0.02 * jax.random.normal(k, (n,), jnp.float32)

    dt0 = jnp.exp(jax.random.uniform(ks[10], (SSM_HEADS,), jnp.float32, np.log(1e-3), np.log(1e-1)))
    return {
        "x": jax.random.normal(ks[0], (BATCH, SEQ, D_MODEL), jnp.float32),
        "l0_attn_norm": gain(ks[1], D_MODEL),
        "l0_in_w": nrm(ks[2], (D_MODEL, D_IN_EVEN), D_MODEL),
        "l0_kv_norm": gain(ks[3], KV_LATENT),
        "l0_w_uk": nrm(ks[4], (N_HEADS_A, HEAD_DIM_A, KV_LATENT), KV_LATENT),
        "l0_w_uv": nrm(ks[5], (N_HEADS_A, KV_LATENT, HEAD_DIM_A), KV_LATENT),
        "l0_conv_w": nrm(ks[6], (SSM_CONV, SSM_D_INNER + 2 * SSM_GROUPS * SSM_STATE), SSM_CONV),
        "l0_conv_b": 0.02 * jax.random.normal(ks[7], (SSM_D_INNER + 2 * SSM_GROUPS * SSM_STATE,), jnp.float32),
        "l0_dt_bias": dt0 + jnp.log(-jnp.expm1(-dt0)),
        "l0_A_log": jnp.log(jax.random.uniform(ks[8], (SSM_HEADS,), jnp.float32, 1.0, 16.0)),
        "l0_D": 1.0 + 0.1 * jax.random.normal(ks[9], (SSM_HEADS,), jnp.float32),
        "l0_ssm_norm": gain(ks[11], SSM_D_INNER),
        "l0_out_w": nrm(ks[12], (D_OUT_EVEN, D_MODEL), D_OUT_EVEN),
        "l0_ffn_norm": gain(ks[13], D_MODEL),
        "l0_w_gate": nrm(ks[14], (D_MODEL, D_FF), D_MODEL),
        "l0_w_up": nrm(ks[15], (D_MODEL, D_FF), D_MODEL),
        "l0_w_down": nrm(ks[16], (D_FF, D_MODEL), D_FF),
        "l1_conv_norm": gain(ks[17], D_MODEL),
        "l1_in_w": nrm(ks[18], (D_MODEL, 3 * SC_WIDTH), D_MODEL),
        "l1_conv_w": nrm(ks[19], (SC_CONV, SC_WIDTH), SC_CONV),
        "l1_out_w": nrm(ks[20], (SC_WIDTH, D_MODEL), SC_WIDTH),
        "l1_ffn_norm": gain(ks[21], D_MODEL),
        "l1_w_gate": nrm(ks[22], (D_MODEL, D_FF), D_MODEL),
        "l1_w_up": nrm(ks[23], (D_MODEL, D_FF), D_MODEL),
        "l1_w_down": nrm(ks[24], (D_FF, D_MODEL), D_FF),
        "final_norm": gain(ks[25], D_MODEL),
    }


def reference(x, l0_attn_norm, l0_in_w, l0_kv_norm, l0_w_uk, l0_w_uv, l0_conv_w, l0_conv_b,
              l0_dt_bias, l0_A_log, l0_D, l0_ssm_norm, l0_out_w, l0_ffn_norm, l0_w_gate,
              l0_w_up, l0_w_down, l1_conv_norm, l1_in_w, l1_conv_w, l1_out_w, l1_ffn_norm,
              l1_w_gate, l1_w_up, l1_w_down, final_norm):
    layers = (
        ((l0_attn_norm, l0_in_w, l0_kv_norm, l0_w_uk, l0_w_uv, l0_conv_w, l0_conv_b,
          l0_dt_bias, l0_A_log, l0_D, l0_ssm_norm, l0_out_w),
         (l0_ffn_norm, l0_w_gate, l0_w_up, l0_w_down)),
        ((l1_conv_norm, l1_in_w, l1_conv_w, l1_out_w),
         (l1_ffn_norm, l1_w_gate, l1_w_up, l1_w_down)),
    )
    for i in range(DEPTH):
        mix_p, ffn_p = layers[i]
        if i % 2 == 0:
            x = _parallel_mixer_layer(x, *mix_p)
        else:
            x = _short_conv_layer(x, *mix_p)
        x = _swiglu_ffn(x, *ffn_p)
    return _rmsnorm(x, final_norm)
```

```python
import functools

import jax
import jax.numpy as jnp
import numpy as np
from jax import lax
from jax.experimental import pallas as pl
from jax.experimental.pallas import tpu as pltpu

D_MODEL = 1024
N_HEADS_A = 8
HEAD_DIM_A = 64
KV_LATENT = 256
IDX_HEADS = 8
IDX_DIM = 64
TOPK_MAX = 256
Q_BLOCK = 128
SSM_D_INNER = 1024
SSM_HEADS = 16
SSM_HEAD_DIM = SSM_D_INNER // SSM_HEADS
SSM_GROUPS = 2
SSM_STATE = 128
SSM_CONV = 4
SSM_CHUNK = 128
SC_WIDTH = D_MODEL
SC_CONV = 3
D_FF = -(-8 * D_MODEL // (3 * 256)) * 256
EPS = 1e-6

LANES = 128
SUBLANES = 8
TOKEN_TILE = 512
KEY_CHUNK = 512
VMEM_LIMIT = 56 * 1024 * 1024

F32 = jnp.float32
BF16 = jnp.bfloat16
NEG_BIG = -1e30
F32_LOWEST = float(np.finfo(np.float32).min)
KEY_LOWEST = int(np.array(F32_LOWEST, np.float32).view(np.int32)) ^ 0x7FFFFFFF
KEY_LOWEST = KEY_LOWEST - (1 << 32) if KEY_LOWEST >= (1 << 31) else KEY_LOWEST


def _vmem_full():
    return pl.BlockSpec(memory_space=pltpu.VMEM)


def _params(*sem):
    return pltpu.CompilerParams(dimension_semantics=sem, vmem_limit_bytes=VMEM_LIMIT)


def _rms(x, w):
    return x * lax.rsqrt(jnp.mean(x * x, axis=-1, keepdims=True) + EPS) * w


def _dot(a, b):
    return jnp.dot(a, b, preferred_element_type=F32)


def _dot_exact(a, b):
    return jnp.dot(a, b, preferred_element_type=F32, precision=lax.Precision.HIGHEST)


def _dot_nt(a, b):
    return lax.dot_general(a, b, (((1,), (1,)), ((), ())), preferred_element_type=F32)


def _in0_kernel(x_ref, g_ref, wq, wckv, wiq, wikw, wz, wxbc, wdt, kvn_ref,
                q_o, ckv_o, iq_o, ikw_o, z_o, xbc_o, dt_o):
    xn = _rms(x_ref[...], g_ref[...]).astype(BF16)
    q_o[...] = _dot(xn, wq[...]).astype(BF16)
    c = _dot(xn, wckv[...])
    ckv_o[...] = _rms(c, kvn_ref[...]).astype(BF16)
    iq_o[...] = _dot(xn, wiq[...]).astype(BF16)
    ikw_o[...] = _dot(xn, wikw[...])
    z_o[...] = _dot(xn, wz[...])
    xbc_o[...] = _dot(xn, wxbc[...])
    dt_o[...] = _dot(xn, wdt[...])


def _in_proj0(x2, g, in_w, kv_norm):
    n = x2.shape[0]
    cuts = np.cumsum((512, 256, 512, 64, 8, 1024, 1536, 16))
    w = in_w.astype(BF16)
    wq, wckv, wiq = w[:, :cuts[0]], w[:, cuts[0]:cuts[1]], w[:, cuts[1]:cuts[2]]
    wikw = jnp.pad(w[:, cuts[2]:cuts[4]], ((0, 0), (0, LANES - 72)))
    wz, wxbc = w[:, cuts[4]:cuts[5]], w[:, cuts[5]:cuts[6]]
    wdt = jnp.pad(w[:, cuts[6]:cuts[7]], ((0, 0), (0, LANES - SSM_HEADS)))
    tm = TOKEN_TILE
    row = lambda c: pl.BlockSpec((tm, c), lambda i: (i, 0))
    out_cols = (512, 256, 512, LANES, 1024, 1536, LANES)
    out_dt = (BF16, BF16, BF16, F32, F32, F32, F32)
    return pl.pallas_call(
        _in0_kernel,
        grid=(n // tm,),
        in_specs=[row(D_MODEL), _vmem_full()] + [_vmem_full()] * 7 + [_vmem_full()],
        out_specs=[row(c) for c in out_cols],
        out_shape=[jax.ShapeDtypeStruct((n, c), d) for c, d in zip(out_cols, out_dt)],
        compiler_params=_params("parallel"),
        name="in_proj0",
    )(x2, g.reshape(1, -1), wq, wckv, wiq, wikw, wz, wxbc, wdt, kv_norm.reshape(1, -1))


def _f2k(x):
    b = pltpu.bitcast(x, jnp.int32)
    return jnp.where(b < 0, b ^ jnp.int32(0x7FFFFFFF), b)


def _k2f(k):
    b = jnp.where(k < 0, k ^ jnp.int32(0x7FFFFFFF), k)
    return pltpu.bitcast(b, F32)


def _dsa_kernel(q_ref, iq_ref, ikw_ref, ikt_ref, ckv_ref, wuk_ref, wuv_ref, o_ref,
                score_scr, qlat_scr, iq4_scr, wb_scr, thr_scr, p_scr, acc_scr, m_scr, l_scr,
                *, topk, value_steps):
    i = pl.program_id(1)
    kc = KEY_CHUNK
    ncol = kc // LANES
    qb = Q_BLOCK
    nch = (i * qb) // kc + 1
    row_i = lax.broadcasted_iota(jnp.int32, (qb, LANES), 0)
    lane_i = lax.broadcasted_iota(jnp.int32, (qb, LANES), 1)
    rel_i = lane_i - row_i
    t0 = i * qb

    r = _dot(q_ref[0], wuk_ref[...])
    for h in range(N_HEADS_A):
        qlat_scr[h * qb:(h + 1) * qb, :] = (
            r[:, h * KV_LATENT:(h + 1) * KV_LATENT] * (HEAD_DIM_A ** -0.5)).astype(BF16)
    iq = iq_ref[0]
    for p in range(IDX_HEADS // 2):
        iq4_scr[p * qb:(p + 1) * qb, :] = iq[:, p * LANES:(p + 1) * LANES]
    ikw = ikw_ref[0]
    wscale = (IDX_HEADS ** -0.5) * (IDX_DIM ** -0.5)
    for h in range(IDX_HEADS):
        wb_scr[h] = jnp.broadcast_to(ikw[:, IDX_DIM + h:IDX_DIM + h + 1] * wscale, (qb, LANES))

    def p1(j, carry):
        mn, mx = carry
        ik2 = ikt_ref[0, j]
        iq4 = iq4_scr[...]
        r0 = _dot(iq4, ik2[:LANES])
        r1 = _dot(iq4, ik2[LANES:])
        for c in range(ncol):
            acc = None
            for p in range(IDX_HEADS // 2):
                for v, rv in enumerate((r0, r1)):
                    term = jnp.maximum(rv[p * qb:(p + 1) * qb, c * LANES:(c + 1) * LANES], 0.0) \
                        * wb_scr[2 * p + v]
                    acc = term if acc is None else acc + term
            valid = rel_i <= (t0 - j * kc - c * LANES)
            score_scr[j, :, c * LANES:(c + 1) * LANES] = jnp.where(valid, acc, -jnp.inf)
            mx = jnp.maximum(mx, jnp.where(valid, acc, -jnp.inf))
            mn = jnp.minimum(mn, jnp.where(valid, acc, jnp.inf))
        return mn, mx

    mn, mx = lax.fori_loop(0, nch, p1, (jnp.full((qb, LANES), jnp.inf, F32),
                                        jnp.full((qb, LANES), -jnp.inf, F32)))
    rowmin = jnp.min(mn, axis=1, keepdims=True)
    rowmax = jnp.max(mx, axis=1, keepdims=True)

    def count(pred):
        def body(j, acc):
            for c in range(ncol):
                blk = score_scr[j, :, c * LANES:(c + 1) * LANES]
                acc = acc + jnp.where(pred(blk, j * kc + c * LANES), 1.0, 0.0)
            return acc
        acc = lax.fori_loop(0, nch, body, jnp.zeros((qb, LANES), F32))
        return jnp.sum(acc, axis=1, keepdims=True).astype(jnp.int32)

    t_pos = t0 + lax.broadcasted_iota(jnp.int32, (qb, 1), 0)
    nvalid = t_pos + 1
    allsel = nvalid <= topk
    lo0 = jnp.where(allsel, jnp.int32(KEY_LOWEST), _f2k(rowmin))
    hi0 = jnp.where(allsel, jnp.int32(KEY_LOWEST + 1), _f2k(rowmax) + 1)
    cnt0 = jnp.where(allsel, jnp.int32(topk), nvalid)
    go0 = jnp.max(jnp.where(lo0 + 1 < hi0, 1, 0))

    def bis_cond(c):
        return c[4] > 0

    def bis_body(c):
        lo, hi, cnt_lo, it, _ = c
        active = lo + 1 < hi
        mid_k = (lo & hi) + ((lo ^ hi) >> 1)
        mid_v = _f2k(0.5 * _k2f(lo) + 0.5 * _k2f(hi - 1))
        mid_v = jnp.minimum(jnp.maximum(mid_v, lo + 1), hi - 1)
        mid = jnp.where(it < value_steps, mid_v, mid_k)
        cb = jnp.broadcast_to(_k2f(mid), (qb, LANES))
        cnt = count(lambda blk, base: blk >= cb)
        ge = cnt >= topk
        up = jnp.logical_and(active, ge)
        dn = jnp.logical_and(active, jnp.logical_not(ge))
        hit = jnp.logical_and(active, cnt == topk)
        lo = jnp.where(up, mid, lo)
        hi = jnp.where(hit, mid + 1, jnp.where(dn, mid, hi))
        cnt_lo = jnp.where(up, cnt, cnt_lo)
        go = jnp.max(jnp.where(lo + 1 < hi, 1, 0))
        return lo, hi, cnt_lo, it + 1, go

    lo, _, cnt_lo, _, _ = lax.while_loop(bis_cond, bis_body, (lo0, hi0, cnt0, jnp.int32(0), go0))
    thr = _k2f(lo)
    thr_scr[...] = jnp.broadcast_to(thr, (qb, LANES))
    tied = cnt_lo > topk

    @pl.when(jnp.max(jnp.where(tied, 1, 0)) > 0)
    def _():
        thb = thr_scr[...]
        need = topk - count(lambda blk, base: blk > thb)

        def idx_body(_, c):
            plo, phi = c
            pm = (plo + phi) >> 1
            pmb = jnp.broadcast_to(pm, (qb, LANES))
            cnt = count(lambda blk, base: jnp.logical_and(blk == thb, (lane_i + base) <= pmb))
            ok = cnt >= need
            return jnp.where(ok, plo, pm), jnp.where(ok, pm, phi)

        nbits = int(np.ceil(np.log2(score_scr.shape[0] * kc))) + 1
        _, cut = lax.fori_loop(0, nbits, idx_body,
                               (jnp.full((qb, 1), -1, jnp.int32), jnp.broadcast_to(nch * kc - 1, (qb, 1))))
        cut = jnp.broadcast_to(jnp.where(tied, cut, jnp.int32(2 ** 30)), (qb, LANES))

        def fix(j, _):
            for c in range(ncol):
                blk = score_scr[j, :, c * LANES:(c + 1) * LANES]
                drop = jnp.logical_and(blk == thb, (lane_i + (j * kc + c * LANES)) > cut)
                score_scr[j, :, c * LANES:(c + 1) * LANES] = jnp.where(drop, -jnp.inf, blk)
            return 0
        lax.fori_loop(0, nch, fix, 0)

    nh = N_HEADS_A
    m_scr[...] = jnp.full(m_scr.shape, NEG_BIG, F32)
    l_scr[...] = jnp.zeros(l_scr.shape, F32)
    acc_scr[...] = jnp.zeros(acc_scr.shape, F32)

    def p3(j, _):
        kv = ckv_ref[0, pl.ds(pl.multiple_of(j * kc, kc), kc), :]
        s = _dot_nt(qlat_scr[...], kv)
        thb = thr_scr[...]
        bias = jnp.concatenate(
            [jnp.where(score_scr[j, :, c * LANES:(c + 1) * LANES] >= thb, 0.0, NEG_BIG)
             for c in range(ncol)], axis=1)
        for h in range(nh):
            rows = slice(h * qb, (h + 1) * qb)
            sh = s[rows] + bias
            m_old = m_scr[rows]
            m_cur = jnp.max(sh, axis=1, keepdims=True)
            m_new = jnp.maximum(m_old, m_cur)
            p = jnp.exp(sh - pltpu.repeat(m_new, ncol, axis=1))
            alpha = jnp.exp(m_old - m_new)
            l_scr[rows] = alpha * l_scr[rows] + jnp.sum(p, axis=1, keepdims=True)
            m_scr[rows] = m_new
            p_scr[rows] = p.astype(BF16)
            acc_scr[rows] = acc_scr[rows] * pltpu.repeat(alpha, KV_LATENT // LANES, axis=1)
        acc_scr[...] += _dot(p_scr[...], kv)
        return 0

    lax.fori_loop(0, nch, p3, 0)

    inv_l = 1.0 / l_scr[...]
    olat = jnp.concatenate(
        [(acc_scr[h * qb:(h + 1) * qb] * pltpu.repeat(inv_l[h * qb:(h + 1) * qb], KV_LATENT // LANES, axis=1)
          ).astype(BF16) for h in range(nh)], axis=1)
    o_ref[0] = _dot(olat, wuv_ref[...]).astype(BF16)


def _dsa(q, iq, ikw, ckv, w_uk, w_uv, b, s):
    kc = KEY_CHUNK
    nb = s // Q_BLOCK
    sp = -(-s // kc) * kc
    nchunks = sp // kc
    topk = min(TOPK_MAX, s // 4)
    ikt = jnp.swapaxes(ikw.reshape(b, s, LANES)[:, :, :IDX_DIM], 1, 2).astype(BF16)
    ikt = jnp.pad(ikt, ((0, 0), (0, 0), (0, sp - s)))
    zeros = jnp.zeros_like(ikt)
    ikt2 = jnp.concatenate([ikt, zeros, zeros, ikt], axis=1)
    ikt2 = ikt2.reshape(b, 2 * LANES, nchunks, kc).transpose(0, 2, 1, 3)
    ckv_p = jnp.pad(ckv.reshape(b, s, KV_LATENT), ((0, 0), (0, sp - s), (0, 0)))
    eye = jnp.eye(N_HEADS_A, dtype=F32)
    wuk_bd = jnp.einsum('hdc,hg->hdgc', w_uk, eye).reshape(N_HEADS_A * HEAD_DIM_A, N_HEADS_A * KV_LATENT)
    wuv_bd = jnp.einsum('hcd,hg->hcgd', w_uv, eye).reshape(N_HEADS_A * KV_LATENT, N_HEADS_A * HEAD_DIM_A)
    hd = N_HEADS_A * HEAD_DIM_A
    blk = lambda c: pl.BlockSpec((1, Q_BLOCK, c), lambda bi, i: (bi, i, 0))
    kern = functools.partial(_dsa_kernel, topk=topk, value_steps=10)
    return pl.pallas_call(
        kern,
        grid=(b, nb),
        in_specs=[blk(hd), blk(IDX_HEADS * IDX_DIM), blk(LANES),
                  pl.BlockSpec((1, nchunks, 2 * LANES, kc), lambda bi, i: (bi, 0, 0, 0)),
                  pl.BlockSpec((1, sp, KV_LATENT), lambda bi, i: (bi, 0, 0)),
                  _vmem_full(), _vmem_full()],
        out_specs=blk(hd),
        out_shape=jax.ShapeDtypeStruct((b, s, hd), BF16),
        scratch_shapes=[
            pltpu.VMEM((nchunks, Q_BLOCK, kc), F32),
            pltpu.VMEM((N_HEADS_A * Q_BLOCK, KV_LATENT), BF16),
            pltpu.VMEM((IDX_HEADS // 2 * Q_BLOCK, LANES), BF16),
            pltpu.VMEM((IDX_HEADS, Q_BLOCK, LANES), F32),
            pltpu.VMEM((Q_BLOCK, LANES), F32),
            pltpu.VMEM((N_HEADS_A * Q_BLOCK, kc), BF16),
            pltpu.VMEM((N_HEADS_A * Q_BLOCK, KV_LATENT), F32),
            pltpu.VMEM((N_HEADS_A * Q_BLOCK, LANES), F32),
            pltpu.VMEM((N_HEADS_A * Q_BLOCK, LANES), F32),
        ],
        compiler_params=_params("parallel", "arbitrary"),
        name="dsa_attention",
    )(q.reshape(b, s, hd), iq.reshape(b, s, -1), ikw.reshape(b, s, LANES), ikt2, ckv_p,
      wuk_bd.astype(BF16), wuv_bd.astype(BF16))


def _shift_rows(x, prev, k):
    y = pltpu.roll(x, k, 0)
    pr = pltpu.roll(prev, k, 0)
    rows = lax.broadcasted_iota(jnp.int32, x.shape, 0)
    for r in range(k):
        y = jnp.where(rows == r, pr[r:r + 1, :], y)
    return y


def _ssd_kernel(xbc_ref, z_ref, dt_ref, cw_ref, cb_ref, dtb_ref, alog_ref, dexp_ref, nw_ref, e_ref,
                o_ref, tail_scr, state_scr):
    c = pl.program_id(1)
    L = SSM_CHUNK
    n = SSM_STATE
    gw = SSM_D_INNER // SSM_GROUPS

    @pl.when(c == 0)
    def _():
        tail_scr[...] = jnp.zeros(tail_scr.shape, F32)
        state_scr[...] = jnp.zeros(state_scr.shape, F32)

    xbc = xbc_ref[0]
    prev = tail_scr[...]
    cw = cw_ref[...]
    conv = xbc * cw[SSM_CONV - 1:SSM_CONV, :] + cb_ref[...]
    for k in range(1, SSM_CONV):
        conv = conv + _shift_rows(xbc, prev, k) * cw[SSM_CONV - 1 - k:SSM_CONV - k, :]
    tail_scr[...] = xbc[L - SUBLANES:, :]
    act = conv * jax.nn.sigmoid(conv)
    xs = act[:, :SSM_D_INNER]
    bm = act[:, SSM_D_INNER:SSM_D_INNER + SSM_GROUPS * n]
    cm = act[:, SSM_D_INNER + SSM_GROUPS * n:]

    dt = jax.nn.softplus(dt_ref[0] + dtb_ref[...])
    a = dt * (-jnp.exp(alog_ref[...]))
    ri = lax.broadcasted_iota(jnp.int32, (L, L), 0)
    ci = lax.broadcasted_iota(jnp.int32, (L, L), 1)
    tri = ri >= ci
    a_cs = _dot_exact(jnp.where(tri, 1.0, 0.0), a)
    a_cs_t = a_cs.T
    e = e_ref[...]
    dt_e = _dot_exact(dt, e)
    acs_e = _dot_exact(a_cs, e)
    last_e = acs_e[L - 1:L, :]
    xdt = xs * dt_e
    xdt_b = xdt.astype(BF16)
    xend_b = (xdt * jnp.exp(last_e - acs_e)).astype(BF16)
    grow = jnp.exp(acs_e)
    lane = lax.broadcasted_iota(jnp.int32, (L, LANES), 1)

    y_parts = []
    for g in range(SSM_GROUPS):
        bg = bm[:, g * n:(g + 1) * n]
        cg = cm[:, g * n:(g + 1) * n].astype(BF16)
        cbm = _dot_nt(cg, bg.astype(BF16))
        for pr in range(gw // LANES):
            col = g * gw + pr * LANES
            xpair = xdt_b[:, col:col + LANES]
            outs = []
            for v in range(2):
                h = (col // SSM_HEAD_DIM) + v
                seg = a_cs[:, h:h + 1] - a_cs_t[h:h + 1, :]
                dec = jnp.exp(jnp.where(tri, seg, -jnp.inf))
                outs.append(_dot((cbm * dec).astype(BF16), xpair))
            y_parts.append(jnp.where(lane < SSM_HEAD_DIM, outs[0], outs[1]))
    y = jnp.concatenate(y_parts, axis=1)

    offs = []
    for g in range(SSM_GROUPS):
        sl = slice(g * gw, (g + 1) * gw)
        bg_t = bm[:, g * n:(g + 1) * n].T.astype(BF16)
        cg = cm[:, g * n:(g + 1) * n].astype(BF16)
        st = state_scr[g]
        offs.append(_dot(cg, st.astype(BF16)))
        state_scr[g] = st * jnp.exp(last_e[:, sl]) + _dot(bg_t, xend_b[:, sl])
    y = y + jnp.concatenate(offs, axis=1) * grow + xs * dexp_ref[...]

    z = z_ref[0]
    y = y * (z * jax.nn.sigmoid(z))
    nw = nw_ref[...]
    outs = []
    for g in range(SSM_GROUPS):
        sl = slice(g * gw, (g + 1) * gw)
        outs.append(_rms(y[:, sl], nw[:, sl]))
    o_ref[0] = jnp.concatenate(outs, axis=1).astype(BF16)


def _ssd(xbc, z, dtp, conv_w, conv_b, dt_bias, a_log, d, norm_w, b, s):
    L = SSM_CHUNK
    cx = SSM_D_INNER + 2 * SSM_GROUPS * SSM_STATE
    pad = LANES - SSM_HEADS
    expand = jnp.repeat(jnp.eye(SSM_HEADS, dtype=F32), SSM_HEAD_DIM, axis=1)
    expand = jnp.pad(expand, ((0, pad), (0, 0)))
    blk = lambda c: pl.BlockSpec((1, L, c), lambda bi, i: (bi, i, 0))
    return pl.pallas_call(
        _ssd_kernel,
        grid=(b, s // L),
        in_specs=[blk(cx), blk(SSM_D_INNER), blk(LANES)] + [_vmem_full()] * 7,
        out_specs=blk(SSM_D_INNER),
        out_shape=jax.ShapeDtypeStruct((b, s, SSM_D_INNER), BF16),
        scratch_shapes=[pltpu.VMEM((SUBLANES, cx), F32),
                        pltpu.VMEM((SSM_GROUPS, SSM_STATE, SSM_D_INNER // SSM_GROUPS), F32)],
        compiler_params=_params("parallel", "arbitrary"),
        name="ssd_mixer",
    )(xbc.reshape(b, s, cx), z.reshape(b, s, SSM_D_INNER), dtp.reshape(b, s, LANES),
      conv_w, conv_b.reshape(1, -1), jnp.pad(dt_bias, (0, pad)).reshape(1, -1),
      jnp.pad(a_log, (0, pad)).reshape(1, -1), jnp.repeat(d, SSM_HEAD_DIM).reshape(1, -1),
      norm_w.reshape(1, -1), expand)


def _out0_kernel(x_ref, ya_ref, yb_ref, wa, wb, o_ref):
    o_ref[...] = x_ref[...] + _dot(ya_ref[...], wa[...]) + _dot(yb_ref[...], wb[...])


def _out_proj0(x2, ya, yb, out_w):
    n = x2.shape[0]
    tm = TOKEN_TILE
    ca = N_HEADS_A * HEAD_DIM_A
    w = out_w.astype(BF16)
    row = lambda c: pl.BlockSpec((tm, c), lambda i: (i, 0))
    return pl.pallas_call(
        _out0_kernel,
        grid=(n // tm,),
        in_specs=[row(D_MODEL), row(ca), row(SSM_D_INNER), _vmem_full(), _vmem_full()],
        out_specs=row(D_MODEL),
        out_shape=jax.ShapeDtypeStruct((n, D_MODEL), F32),
        compiler_params=_params("parallel"),
        name="out_proj0",
    )(x2, ya, yb, w[:ca], w[ca:])


def _ffn_kernel(x_ref, g_ref, wg, wu, wd, fn_ref, o_ref, *, final):
    x = x_ref[...]
    hn = _rms(x, g_ref[...]).astype(BF16)
    a = _dot(hn, wg[...])
    u = _dot(hn, wu[...])
    act = (a * jax.nn.sigmoid(a) * u).astype(BF16)
    y = x + _dot(act, wd[...])
    if final:
        y = _rms(y, fn_ref[...])
    o_ref[...] = y


def _ffn(x2, g, w_gate, w_up, w_down, final_norm, final):
    n = x2.shape[0]
    tm = TOKEN_TILE
    row = pl.BlockSpec((tm, D_MODEL), lambda i: (i, 0))
    return pl.pallas_call(
        functools.partial(_ffn_kernel, final=final),
        grid=(n // tm,),
        in_specs=[row] + [_vmem_full()] * 5,
        out_specs=row,
        out_shape=jax.ShapeDtypeStruct((n, D_MODEL), F32),
        compiler_params=_params("parallel"),
        name="ffn_final" if final else "ffn",
    )(x2, g.reshape(1, -1), w_gate.astype(BF16), w_up.astype(BF16), w_down.astype(BF16),
      final_norm.reshape(1, -1))


def _conv_in_kernel(x_ref, g_ref, wb, wc, wv, gb_o, u_o):
    xn = _rms(x_ref[...], g_ref[...]).astype(BF16)
    gb_o[...] = _dot(xn, wb[...])
    u_o[...] = _dot(xn, wc[...]) * _dot(xn, wv[...])


def _conv_out_kernel(x_ref, gb_ref, u_ref, halo_ref, cw_ref, wo, o_ref, *, tiles_per_seq):
    i = pl.program_id(0)
    u = u_ref[...]
    halo = jnp.where(i % tiles_per_seq == 0, 0.0, halo_ref[...])
    cw = cw_ref[...]
    conv = u * cw[SC_CONV - 1:SC_CONV, :]
    for k in range(1, SC_CONV):
        conv = conv + _shift_rows(u, halo, k) * cw[SC_CONV - 1 - k:SC_CONV - k, :]
    y = (gb_ref[...] * conv).astype(BF16)
    o_ref[...] = x_ref[...] + _dot(y, wo[...])


def _short_conv(x2, norm_w, in_w, conv_w, out_w, s):
    n = x2.shape[0]
    tm = TOKEN_TILE
    w = in_w.astype(BF16)
    row = pl.BlockSpec((tm, D_MODEL), lambda i: (i, 0))
    gb, u = pl.pallas_call(
        _conv_in_kernel,
        grid=(n // tm,),
        in_specs=[row] + [_vmem_full()] * 4,
        out_specs=[row, row],
        out_shape=[jax.ShapeDtypeStruct((n, SC_WIDTH), F32)] * 2,
        compiler_params=_params("parallel"),
        name="conv_in",
    )(x2, norm_w.reshape(1, -1), w[:, :SC_WIDTH], w[:, SC_WIDTH:2 * SC_WIDTH], w[:, 2 * SC_WIDTH:])
    per = tm // SUBLANES
    halo = pl.BlockSpec((SUBLANES, SC_WIDTH), lambda i: (jnp.maximum(i * per - 1, 0), 0))
    return pl.pallas_call(
        functools.partial(_conv_out_kernel, tiles_per_seq=s // tm),
        grid=(n // tm,),
        in_specs=[row, row, row, halo, _vmem_full(), _vmem_full()],
        out_specs=row,
        out_shape=jax.ShapeDtypeStruct((n, D_MODEL), F32),
        compiler_params=_params("parallel"),
        name="conv_out",
    )(x2, gb, u, u, conv_w, out_w.astype(BF16))


def kernel(x, l0_attn_norm, l0_in_w, l0_kv_norm, l0_w_uk, l0_w_uv, l0_conv_w, l0_conv_b, l0_dt_bias,
           l0_A_log, l0_D, l0_ssm_norm, l0_out_w, l0_ffn_norm, l0_w_gate, l0_w_up, l0_w_down,
           l1_conv_norm, l1_in_w, l1_conv_w, l1_out_w, l1_ffn_norm, l1_w_gate, l1_w_up, l1_w_down,
           final_norm):
    b, s, d = x.shape
    assert d == D_MODEL and s % TOKEN_TILE == 0 and s % Q_BLOCK == 0 and s % SSM_CHUNK == 0
    x2 = x.reshape(b * s, d)
    q, ckv, iq, ikw, z, xbc, dtp = _in_proj0(x2, l0_attn_norm, l0_in_w, l0_kv_norm)
    ya = _dsa(q, iq, ikw, ckv, l0_w_uk, l0_w_uv, b, s)
    yb = _ssd(xbc, z, dtp, l0_conv_w, l0_conv_b, l0_dt_bias, l0_A_log, l0_D, l0_ssm_norm, b, s)
    x2 = _out_proj0(x2, ya.reshape(b * s, -1), yb.reshape(b * s, -1), l0_out_w)
    x2 = _ffn(x2, l0_ffn_norm, l0_w_gate, l0_w_up, l0_w_down, final_norm, final=False)
    x2 = _short_conv(x2, l1_conv_norm, l1_in_w, l1_conv_w, l1_out_w, s)
    x2 = _ffn(x2, l1_ffn_norm, l1_w_gate, l1_w_up, l1_w_down, final_norm, final=True)
    return x2.reshape(b, s, d)
```

```python
import functools

import jax
import jax.numpy as jnp
import numpy as np
from jax import lax
from jax.experimental import pallas as pl
from jax.experimental.pallas import tpu as pltpu

D_MODEL = 1024
N_HEADS_A = 8
HEAD_DIM_A = 64
KV_LATENT = 256
IDX_HEADS = 8
IDX_DIM = 64
TOPK_MAX = 256
Q_BLOCK = 128
SSM_D_INNER = 1024
SSM_HEADS = 16
SSM_HEAD_DIM = SSM_D_INNER // SSM_HEADS
SSM_GROUPS = 2
SSM_STATE = 128
SSM_CONV = 4
SSM_CHUNK = 128
SC_WIDTH = D_MODEL
SC_CONV = 3
D_FF = -(-8 * D_MODEL // (3 * 256)) * 256
EPS = 1e-6

LANES = 128
SUBLANES = 8
TOKEN_TILE = 512
KEY_CHUNK = 512
VMEM_LIMIT = 56 * 1024 * 1024

F32 = jnp.float32
BF16 = jnp.bfloat16
NEG_BIG = -1e30
F32_LOWEST = float(np.finfo(np.float32).min)
KEY_LOWEST = int(np.array(F32_LOWEST, np.float32).view(np.int32)) ^ 0x7FFFFFFF


def _vmem_full():
    return pl.BlockSpec(memory_space=pltpu.VMEM)


def _params(*sem):
    return pltpu.CompilerParams(dimension_semantics=sem, vmem_limit_bytes=VMEM_LIMIT)


def _rms(x, w):
    return x * lax.rsqrt(jnp.mean(x * x, axis=-1, keepdims=True) + EPS) * w


def _dot(a, b):
    return jnp.dot(a, b, preferred_element_type=F32)


def _dot_exact(a, b):
    return jnp.dot(a, b, preferred_element_type=F32, precision=lax.Precision.HIGHEST)


def _dot_nt(a, b):
    return lax.dot_general(a, b, (((1,), (1,)), ((), ())), preferred_element_type=F32)


def _in0_kernel(x_ref, g_ref, wqt, wiqt, wikwt, wckv, wikw, wz, wxbc, wdt, kvn_ref,
                qt_o, iqt_o, ikwt_o, ckv_o, ckvt_o, ik_o, z_o, xbc_o, dt_o):
    xn = _rms(x_ref[...], g_ref[...]).astype(BF16)
    qt_o[...] = _dot_nt(wqt[...], xn).astype(BF16)
    iqt_o[...] = _dot_nt(wiqt[...], xn).astype(BF16)
    ikwt_o[...] = _dot_nt(wikwt[...], xn)
    c = _rms(_dot(xn, wckv[...]), kvn_ref[...])
    ckv_o[...] = c.astype(BF16)
    ckvt_o[0] = c.T.astype(BF16)
    ik_o[...] = _dot(xn, wikw[...]).astype(BF16)
    z_o[...] = _dot(xn, wz[...])
    xbc_o[...] = _dot(xn, wxbc[...])
    dt_o[...] = _dot(xn, wdt[...])


def _in_proj0(x2, g, in_w, kv_norm):
    n = x2.shape[0]
    cuts = np.cumsum((512, 256, 512, 64, 8, 1024, 1536, 16))
    w = in_w.astype(BF16)
    wq, wckv, wiq = w[:, :cuts[0]], w[:, cuts[0]:cuts[1]], w[:, cuts[1]:cuts[2]]
    wikw = jnp.pad(w[:, cuts[2]:cuts[4]], ((0, 0), (0, LANES - 72)))
    wz, wxbc = w[:, cuts[4]:cuts[5]], w[:, cuts[5]:cuts[6]]
    wdt = jnp.pad(w[:, cuts[6]:cuts[7]], ((0, 0), (0, LANES - SSM_HEADS)))
    tm = KEY_CHUNK
    row = lambda c: pl.BlockSpec((tm, c), lambda i: (i, 0))
    col = lambda r: pl.BlockSpec((r, tm), lambda i: (0, i))
    hd = N_HEADS_A * HEAD_DIM_A
    out_specs = [col(hd), col(hd), col(LANES), row(KV_LATENT),
                 pl.BlockSpec((1, KV_LATENT, tm), lambda i: (i, 0, 0)),
                 row(LANES), row(SSM_D_INNER), row(wxbc.shape[1]), row(LANES)]
    sds = jax.ShapeDtypeStruct
    out_shape = [sds((hd, n), BF16), sds((hd, n), BF16), sds((LANES, n), F32), sds((n, KV_LATENT), BF16),
                 sds((n // tm, KV_LATENT, tm), BF16), sds((n, LANES), BF16), sds((n, SSM_D_INNER), F32),
                 sds((n, wxbc.shape[1]), F32), sds((n, LANES), F32)]
    return pl.pallas_call(
        _in0_kernel,
        grid=(n // tm,),
        in_specs=[row(D_MODEL)] + [_vmem_full()] * 10,
        out_specs=out_specs,
        out_shape=out_shape,
        compiler_params=_params("parallel"),
        name="in_proj0",
    )(x2, g.reshape(1, -1), wq.T, wiq.T, wikw.T, wckv, wikw, wz, wxbc, wdt, kv_norm.reshape(1, -1))


def _f2k(x):
    b = pltpu.bitcast(x, jnp.int32)
    return jnp.where(b < 0, b ^ jnp.int32(0x7FFFFFFF), b)


def _k2f(k):
    b = jnp.where(k < 0, k ^ jnp.int32(0x7FFFFFFF), k)
    return pltpu.bitcast(b, F32)


FOLD_ROWS = 8 * SUBLANES
VALUE_PROBES = 10
PROBES_PER_CHECK = 3


def _fold(x, op):
    return op(x.reshape(x.shape[0] // FOLD_ROWS, FOLD_ROWS, LANES), axis=0)


def _any_lane(flag):
    return jnp.max(jnp.where(flag, 1.0, 0.0))


def _dsa_kernel(qt_ref, iqt_ref, wt_ref, ik_ref, ckv_ref, ckvt_ref, wukt_ref, wuvt_ref, o_ref,
                score_scr, qlat_scr, r_scr, p_scr, acc_scr, m_scr, l_scr, thr_scr,
                *, topk):
    i = pl.program_id(1)
    kc = KEY_CHUNK
    qb = Q_BLOCK
    nh = N_HEADS_A
    nch = (i * qb) // kc + 1
    t0 = i * qb
    key_i = lax.broadcasted_iota(jnp.int32, (kc, LANES), 0)
    rel_i = key_i - lax.broadcasted_iota(jnp.int32, (kc, LANES), 1)

    qt = qt_ref[...]
    qscale = (HEAD_DIM_A ** -0.5) * float(np.log2(np.e))
    for p in range(nh // 2):
        r = _dot(wukt_ref[p], qt[p * LANES:(p + 1) * LANES, :])
        for v in range(2):
            h = 2 * p + v
            qlat_scr[:, h * qb:(h + 1) * qb] = (r[v * KV_LATENT:(v + 1) * KV_LATENT, :] * qscale).astype(BF16)
    iqt = iqt_ref[...]
    r_scr[...] = jnp.zeros(r_scr.shape, BF16)
    for h in range(IDX_HEADS):
        r_scr[0:IDX_DIM, h * qb:(h + 1) * qb] = iqt[h * IDX_DIM:(h + 1) * IDX_DIM, :]
    wts = wt_ref[...] * ((IDX_HEADS ** -0.5) * (IDX_DIM ** -0.5))

    def p1(j, carry):
        mn, mx, npos, nnonneg = carry
        base = pl.multiple_of(j * kc, kc)
        lt = _dot(ik_ref[0, pl.ds(base, kc), :], r_scr[...])
        acc = None
        for h in range(IDX_HEADS):
            term = jnp.maximum(lt[:, h * qb:(h + 1) * qb], 0.0) * wts[h:h + 1, :]
            acc = term if acc is None else acc + term
        valid = rel_i <= (t0 - base)
        sc = jnp.where(valid, acc, -jnp.inf)
        score_scr[pl.ds(base, kc), :] = sc
        mx = jnp.maximum(mx, _fold(sc, jnp.max))
        mn = jnp.minimum(mn, _fold(jnp.where(valid, acc, jnp.inf), jnp.min))
        npos = npos + _fold(jnp.where(sc > 0.0, 1.0, 0.0), jnp.sum)
        nnonneg = nnonneg + _fold(jnp.where(sc >= 0.0, 1.0, 0.0), jnp.sum)
        return mn, mx, npos, nnonneg

    zero_acc = jnp.zeros((FOLD_ROWS, LANES), F32)
    mn, mx, npos, nnonneg = lax.fori_loop(
        0, nch, p1, (jnp.full((FOLD_ROWS, LANES), jnp.inf, F32), jnp.full((FOLD_ROWS, LANES), -jnp.inf, F32),
                     zero_acc, zero_acc))
    colmin = jnp.min(mn, axis=0, keepdims=True)
    colmax = jnp.max(mx, axis=0, keepdims=True)
    npos = jnp.sum(npos, axis=0, keepdims=True).astype(jnp.int32)
    nnonneg = jnp.sum(nnonneg, axis=0, keepdims=True).astype(jnp.int32)

    def count(pred):
        def body(j, acc):
            base = pl.multiple_of(j * kc, kc)
            blk = score_scr[pl.ds(base, kc), :]
            return acc + _fold(jnp.where(pred(blk, base), 1.0, 0.0), jnp.sum)
        acc = lax.fori_loop(0, nch, body, jnp.zeros((FOLD_ROWS, LANES), F32))
        return jnp.sum(acc, axis=0, keepdims=True).astype(jnp.int32)

    t_pos = t0 + lax.broadcasted_iota(jnp.int32, (1, qb), 1)
    nvalid = t_pos + 1
    allsel = nvalid <= topk
    key_zero, key_tiny, key_negzero = 0, 1, -1
    at_zero = jnp.logical_and(npos < topk, nnonneg >= topk)
    above = npos >= topk
    below = nnonneg < topk
    lo0 = jnp.where(at_zero, key_zero, jnp.where(above, key_tiny, _f2k(colmin)))
    cnt0 = jnp.where(at_zero, nnonneg, jnp.where(above, npos, nvalid))
    hi0 = jnp.where(at_zero, key_zero + 1, jnp.where(below, key_negzero, _f2k(colmax) + 1))
    lo0 = jnp.where(allsel, jnp.int32(KEY_LOWEST), lo0)
    hi0 = jnp.where(allsel, jnp.int32(KEY_LOWEST + 1), hi0)
    cnt0 = jnp.where(allsel, jnp.int32(topk), cnt0)
    go0 = _any_lane(lo0 + 1 < hi0)

    def search_step(lo, hi, cnt_lo, by_value):
        active = lo + 1 < hi
        mid_k = (lo & hi) + ((lo ^ hi) >> 1)
        mid_v = _f2k(0.5 * _k2f(lo) + 0.5 * _k2f(hi - 1))
        mid_v = jnp.minimum(jnp.maximum(mid_v, lo + 1), hi - 1)
        mid = jnp.where(by_value, mid_v, mid_k)
        cand = _k2f(mid)
        cnt = count(lambda blk, base: blk >= cand)
        ge = cnt >= topk
        up = jnp.logical_and(active, ge)
        dn = jnp.logical_and(active, jnp.logical_not(ge))
        hit = jnp.logical_and(active, cnt == topk)
        lo = jnp.where(up, mid, lo)
        hi = jnp.where(hit, mid + 1, jnp.where(dn, mid, hi))
        cnt_lo = jnp.where(up, cnt, cnt_lo)
        return lo, hi, cnt_lo

    def bis_cond(c):
        return c[4] > 0.0

    def bis_body(c):
        lo, hi, cnt_lo, it, _ = c
        for u in range(PROBES_PER_CHECK):
            lo, hi, cnt_lo = search_step(lo, hi, cnt_lo, it + u < VALUE_PROBES)
        return lo, hi, cnt_lo, it + PROBES_PER_CHECK, _any_lane(lo + 1 < hi)

    lo, _, cnt_lo, _, _ = lax.while_loop(bis_cond, bis_body, (lo0, hi0, cnt0, jnp.int32(0), go0))
    thr_scr[...] = _k2f(lo)
    tied = cnt_lo > topk

    @pl.when(_any_lane(tied) > 0.0)
    def _():
        thr = thr_scr[...]
        need = topk - count(lambda blk, base: blk > thr)

        def idx_body(_, c):
            plo, phi = c
            pm = (plo + phi) >> 1
            cnt = count(lambda blk, base: jnp.logical_and(blk == thr, (key_i + base) <= pm))
            ok = cnt >= need
            return jnp.where(ok, plo, pm), jnp.where(ok, pm, phi)

        nbits = int(np.ceil(np.log2(score_scr.shape[0]))) + 1
        _, cut = lax.fori_loop(0, nbits, idx_body,
                               (jnp.full((1, qb), -1, jnp.int32), jnp.broadcast_to(nch * kc - 1, (1, qb))))
        cut = jnp.where(tied, cut, jnp.int32(2 ** 30))

        def fix(j, _):
            base = pl.multiple_of(j * kc, kc)
            blk = score_scr[pl.ds(base, kc), :]
            drop = jnp.logical_and(blk == thr, (key_i + base) > cut)
            score_scr[pl.ds(base, kc), :] = jnp.where(drop, -jnp.inf, blk)
            return 0
        lax.fori_loop(0, nch, fix, 0)

    m_scr[...] = jnp.full(m_scr.shape, NEG_BIG, F32)
    l_scr[...] = jnp.zeros(l_scr.shape, F32)
    acc_scr[...] = jnp.zeros(acc_scr.shape, F32)

    def p3(j, _):
        base = pl.multiple_of(j * kc, kc)
        s = _dot(ckv_ref[0, pl.ds(base, kc), :], qlat_scr[...])
        bias = jnp.where(score_scr[pl.ds(base, kc), :] >= thr_scr[...], 0.0, NEG_BIG)
        alphas = []
        for h in range(nh):
            cols = slice(h * qb, (h + 1) * qb)
            sh = s[:, cols] + bias
            m_old = m_scr[h:h + 1, :]
            m_new = jnp.maximum(m_old, jnp.max(sh, axis=0, keepdims=True))
            p = jnp.exp2(sh - m_new)
            alpha = jnp.exp2(m_old - m_new)
            l_scr[h:h + 1, :] = alpha * l_scr[h:h + 1, :] + jnp.sum(p, axis=0, keepdims=True)
            m_scr[h:h + 1, :] = m_new
            p_scr[:, cols] = p.astype(BF16)
            alphas.append(alpha)
        alpha_all = jnp.concatenate(alphas, axis=1)
        acc_scr[...] = acc_scr[...] * alpha_all + _dot(ckvt_ref[0, j], p_scr[...])
        return 0

    lax.fori_loop(0, nch, p3, 0)

    inv_l = 1.0 / l_scr[...]
    outs = []
    for h in range(nh):
        olat = (acc_scr[:, h * qb:(h + 1) * qb] * inv_l[h:h + 1, :]).astype(BF16)
        outs.append(_dot(wuvt_ref[h], olat))
    o_ref[...] = jnp.concatenate(outs, axis=0).T.astype(BF16)


def _dsa(qt, iqt, ikwt, ik, ckv, ckvt, w_uk, w_uv, b, s):
    kc = KEY_CHUNK
    nb = s // Q_BLOCK
    nchunks = s // kc
    topk = min(TOPK_MAX, s // 4)
    hd = N_HEADS_A * HEAD_DIM_A
    wukt = jnp.swapaxes(w_uk, 1, 2).reshape(N_HEADS_A // 2, 2, KV_LATENT, HEAD_DIM_A)
    eye2 = jnp.eye(2, dtype=F32)
    wukt = jnp.einsum('pvcd,vu->pvcud', wukt, eye2).reshape(N_HEADS_A // 2, 2 * KV_LATENT, 2 * HEAD_DIM_A)
    wuvt = jnp.swapaxes(w_uv, 1, 2)
    qcol = lambda r: pl.BlockSpec((r, Q_BLOCK), lambda bi, i: (0, bi * nb + i))
    kern = functools.partial(_dsa_kernel, topk=topk)
    return pl.pallas_call(
        kern,
        grid=(b, nb),
        in_specs=[qcol(hd), qcol(IDX_HEADS * IDX_DIM),
                  pl.BlockSpec((IDX_HEADS, Q_BLOCK), lambda bi, i: (IDX_DIM // IDX_HEADS, bi * nb + i)),
                  pl.BlockSpec((1, s, LANES), lambda bi, i: (bi, 0, 0)),
                  pl.BlockSpec((1, s, KV_LATENT), lambda bi, i: (bi, 0, 0)),
                  pl.BlockSpec((1, nchunks, KV_LATENT, kc), lambda bi, i: (bi, 0, 0, 0)),
                  _vmem_full(), _vmem_full()],
        out_specs=pl.BlockSpec((Q_BLOCK, hd), lambda bi, i: (bi * nb + i, 0)),
        out_shape=jax.ShapeDtypeStruct((b * s, hd), BF16),
        scratch_shapes=[
            pltpu.VMEM((s, Q_BLOCK), F32),
            pltpu.VMEM((KV_LATENT, N_HEADS_A * Q_BLOCK), BF16),
            pltpu.VMEM((LANES, IDX_HEADS * Q_BLOCK), BF16),
            pltpu.VMEM((kc, N_HEADS_A * Q_BLOCK), BF16),
            pltpu.VMEM((KV_LATENT, N_HEADS_A * Q_BLOCK), F32),
            pltpu.VMEM((N_HEADS_A, Q_BLOCK), F32),
            pltpu.VMEM((N_HEADS_A, Q_BLOCK), F32),
            pltpu.VMEM((1, Q_BLOCK), F32),
        ],
        compiler_params=_params("parallel", "arbitrary"),
        name="dsa_attention",
    )(qt, iqt, ikwt, ik.reshape(b, s, LANES), ckv.reshape(b, s, KV_LATENT),
      ckvt.reshape(b, nchunks, KV_LATENT, kc), wukt.astype(BF16), wuvt.astype(BF16))


def _shift_rows(x, prev, k):
    y = pltpu.roll(x, k, 0)
    pr = pltpu.roll(prev, k, 0)
    rows = lax.broadcasted_iota(jnp.int32, x.shape, 0)
    for r in range(k):
        y = jnp.where(rows == r, pr[r:r + 1, :], y)
    return y


def _ssd_kernel(xbc_ref, z_ref, dt_ref, cw_ref, cb_ref, dtb_ref, alog_ref, dexp_ref, nw_ref, e_ref,
                o_ref, tail_scr, state_scr):
    c = pl.program_id(1)
    L = SSM_CHUNK
    n = SSM_STATE
    gw = SSM_D_INNER // SSM_GROUPS

    @pl.when(c == 0)
    def _():
        tail_scr[...] = jnp.zeros(tail_scr.shape, F32)
        state_scr[...] = jnp.zeros(state_scr.shape, F32)

    xbc = xbc_ref[0]
    prev = tail_scr[...]
    cw = cw_ref[...]
    conv = xbc * cw[SSM_CONV - 1:SSM_CONV, :] + cb_ref[...]
    for k in range(1, SSM_CONV):
        conv = conv + _shift_rows(xbc, prev, k) * cw[SSM_CONV - 1 - k:SSM_CONV - k, :]
    tail_scr[...] = xbc[L - SUBLANES:, :]
    act = conv * jax.nn.sigmoid(conv)
    xs = act[:, :SSM_D_INNER]
    bm = act[:, SSM_D_INNER:SSM_D_INNER + SSM_GROUPS * n]
    cm = act[:, SSM_D_INNER + SSM_GROUPS * n:]

    dt = jax.nn.softplus(dt_ref[0] + dtb_ref[...])
    a = dt * (-jnp.exp(alog_ref[...]))
    ri = lax.broadcasted_iota(jnp.int32, (L, L), 0)
    ci = lax.broadcasted_iota(jnp.int32, (L, L), 1)
    tri = ri >= ci
    a_cs = _dot_exact(jnp.where(tri, 1.0, 0.0), a)
    a_cs_t = a_cs.T
    e = e_ref[...]
    dt_e = _dot_exact(dt, e)
    acs_e = _dot_exact(a_cs, e)
    last_e = acs_e[L - 1:L, :]
    xdt = xs * dt_e
    xdt_b = xdt.astype(BF16)
    xend_b = (xdt * jnp.exp(last_e - acs_e)).astype(BF16)
    grow = jnp.exp(acs_e)
    lane = lax.broadcasted_iota(jnp.int32, (L, LANES), 1)

    y_parts = []
    for g in range(SSM_GROUPS):
        bg = bm[:, g * n:(g + 1) * n]
        cg = cm[:, g * n:(g + 1) * n].astype(BF16)
        cbm = _dot_nt(cg, bg.astype(BF16))
        for pr in range(gw // LANES):
            col = g * gw + pr * LANES
            xpair = xdt_b[:, col:col + LANES]
            outs = []
            for v in range(2):
                h = (col // SSM_HEAD_DIM) + v
                seg = a_cs[:, h:h + 1] - a_cs_t[h:h + 1, :]
                dec = jnp.exp(jnp.where(tri, seg, -jnp.inf))
                outs.append(_dot((cbm * dec).astype(BF16), xpair))
            y_parts.append(jnp.where(lane < SSM_HEAD_DIM, outs[0], outs[1]))
    y = jnp.concatenate(y_parts, axis=1)

    offs = []
    for g in range(SSM_GROUPS):
        sl = slice(g * gw, (g + 1) * gw)
        bg_t = bm[:, g * n:(g + 1) * n].T.astype(BF16)
        cg = cm[:, g * n:(g + 1) * n].astype(BF16)
        st = state_scr[g]
        offs.append(_dot(cg, st.astype(BF16)))
        state_scr[g] = st * jnp.exp(last_e[:, sl]) + _dot(bg_t, xend_b[:, sl])
    y = y + jnp.concatenate(offs, axis=1) * grow + xs * dexp_ref[...]

    z = z_ref[0]
    y = y * (z * jax.nn.sigmoid(z))
    nw = nw_ref[...]
    outs = []
    for g in range(SSM_GROUPS):
        sl = slice(g * gw, (g + 1) * gw)
        outs.append(_rms(y[:, sl], nw[:, sl]))
    o_ref[0] = jnp.concatenate(outs, axis=1).astype(BF16)


def _ssd(xbc, z, dtp, conv_w, conv_b, dt_bias, a_log, d, norm_w, b, s):
    L = SSM_CHUNK
    cx = SSM_D_INNER + 2 * SSM_GROUPS * SSM_STATE
    pad = LANES - SSM_HEADS
    expand = jnp.repeat(jnp.eye(SSM_HEADS, dtype=F32), SSM_HEAD_DIM, axis=1)
    expand = jnp.pad(expand, ((0, pad), (0, 0)))
    blk = lambda c: pl.BlockSpec((1, L, c), lambda bi, i: (bi, i, 0))
    return pl.pallas_call(
        _ssd_kernel,
        grid=(b, s // L),
        in_specs=[blk(cx), blk(SSM_D_INNER), blk(LANES)] + [_vmem_full()] * 7,
        out_specs=blk(SSM_D_INNER),
        out_shape=jax.ShapeDtypeStruct((b, s, SSM_D_INNER), BF16),
        scratch_shapes=[pltpu.VMEM((SUBLANES, cx), F32),
                        pltpu.VMEM((SSM_GROUPS, SSM_STATE, SSM_D_INNER // SSM_GROUPS), F32)],
        compiler_params=_params("parallel", "arbitrary"),
        name="ssd_mixer",
    )(xbc.reshape(b, s, cx), z.reshape(b, s, SSM_D_INNER), dtp.reshape(b, s, LANES),
      conv_w, conv_b.reshape(1, -1), jnp.pad(dt_bias, (0, pad)).reshape(1, -1),
      jnp.pad(a_log, (0, pad)).reshape(1, -1), jnp.repeat(d, SSM_HEAD_DIM).reshape(1, -1),
      norm_w.reshape(1, -1), expand)


def _out0_kernel(x_ref, ya_ref, yb_ref, wa, wb, o_ref):
    o_ref[...] = x_ref[...] + _dot(ya_ref[...], wa[...]) + _dot(yb_ref[...], wb[...])


def _out_proj0(x2, ya, yb, out_w):
    n = x2.shape[0]
    tm = TOKEN_TILE
    ca = N_HEADS_A * HEAD_DIM_A
    w = out_w.astype(BF16)
    row = lambda c: pl.BlockSpec((tm, c), lambda i: (i, 0))
    return pl.pallas_call(
        _out0_kernel,
        grid=(n // tm,),
        in_specs=[row(D_MODEL), row(ca), row(SSM_D_INNER), _vmem_full(), _vmem_full()],
        out_specs=row(D_MODEL),
        out_shape=jax.ShapeDtypeStruct((n, D_MODEL), F32),
        compiler_params=_params("parallel"),
        name="out_proj0",
    )(x2, ya, yb, w[:ca], w[ca:])


def _ffn_kernel(x_ref, g_ref, wg, wu, wd, fn_ref, o_ref, *, final):
    x = x_ref[...]
    hn = _rms(x, g_ref[...]).astype(BF16)
    a = _dot(hn, wg[...])
    u = _dot(hn, wu[...])
    act = (a * jax.nn.sigmoid(a) * u).astype(BF16)
    y = x + _dot(act, wd[...])
    if final:
        y = _rms(y, fn_ref[...])
    o_ref[...] = y


def _ffn(x2, g, w_gate, w_up, w_down, final_norm, final):
    n = x2.shape[0]
    tm = TOKEN_TILE
    row = pl.BlockSpec((tm, D_MODEL), lambda i: (i, 0))
    return pl.pallas_call(
        functools.partial(_ffn_kernel, final=final),
        grid=(n // tm,),
        in_specs=[row] + [_vmem_full()] * 5,
        out_specs=row,
        out_shape=jax.ShapeDtypeStruct((n, D_MODEL), F32),
        compiler_params=_params("parallel"),
        name="ffn_final" if final else "ffn",
    )(x2, g.reshape(1, -1), w_gate.astype(BF16), w_up.astype(BF16), w_down.astype(BF16),
      final_norm.reshape(1, -1))


def _conv_in_kernel(x_ref, g_ref, wb, wc, wv, gb_o, u_o):
    xn = _rms(x_ref[...], g_ref[...]).astype(BF16)
    gb_o[...] = _dot(xn, wb[...])
    u_o[...] = _dot(xn, wc[...]) * _dot(xn, wv[...])


def _conv_out_kernel(x_ref, gb_ref, u_ref, halo_ref, cw_ref, wo, o_ref, *, tiles_per_seq):
    i = pl.program_id(0)
    u = u_ref[...]
    halo = jnp.where(i % tiles_per_seq == 0, 0.0, halo_ref[...])
    cw = cw_ref[...]
    conv = u * cw[SC_CONV - 1:SC_CONV, :]
    for k in range(1, SC_CONV):
        conv = conv + _shift_rows(u, halo, k) * cw[SC_CONV - 1 - k:SC_CONV - k, :]
    y = (gb_ref[...] * conv).astype(BF16)
    o_ref[...] = x_ref[...] + _dot(y, wo[...])


def _short_conv(x2, norm_w, in_w, conv_w, out_w, s):
    n = x2.shape[0]
    tm = TOKEN_TILE
    w = in_w.astype(BF16)
    row = pl.BlockSpec((tm, D_MODEL), lambda i: (i, 0))
    gb, u = pl.pallas_call(
        _conv_in_kernel,
        grid=(n // tm,),
        in_specs=[row] + [_vmem_full()] * 4,
        out_specs=[row, row],
        out_shape=[jax.ShapeDtypeStruct((n, SC_WIDTH), F32)] * 2,
        compiler_params=_params("parallel"),
        name="conv_in",
    )(x2, norm_w.reshape(1, -1), w[:, :SC_WIDTH], w[:, SC_WIDTH:2 * SC_WIDTH], w[:, 2 * SC_WIDTH:])
    per = tm // SUBLANES
    halo = pl.BlockSpec((SUBLANES, SC_WIDTH), lambda i: (jnp.maximum(i * per - 1, 0), 0))
    return pl.pallas_call(
        functools.partial(_conv_out_kernel, tiles_per_seq=s // tm),
        grid=(n // tm,),
        in_specs=[row, row, row, halo, _vmem_full(), _vmem_full()],
        out_specs=row,
        out_shape=jax.ShapeDtypeStruct((n, D_MODEL), F32),
        compiler_params=_params("parallel"),
        name="conv_out",
    )(x2, gb, u, u, conv_w, out_w.astype(BF16))


def kernel(x, l0_attn_norm, l0_in_w, l0_kv_norm, l0_w_uk, l0_w_uv, l0_conv_w, l0_conv_b, l0_dt_bias,
           l0_A_log, l0_D, l0_ssm_norm, l0_out_w, l0_ffn_norm, l0_w_gate, l0_w_up, l0_w_down,
           l1_conv_norm, l1_in_w, l1_conv_w, l1_out_w, l1_ffn_norm, l1_w_gate, l1_w_up, l1_w_down,
           final_norm):
    b, s, d = x.shape
    assert d == D_MODEL and s % TOKEN_TILE == 0 and s % KEY_CHUNK == 0 and s % SSM_CHUNK == 0
    x2 = x.reshape(b * s, d)
    qt, iqt, ikwt, ckv, ckvt, ik, z, xbc, dtp = _in_proj0(x2, l0_attn_norm, l0_in_w, l0_kv_norm)
    ya = _dsa(qt, iqt, ikwt, ik, ckv, ckvt, l0_w_uk, l0_w_uv, b, s)
    yb = _ssd(xbc, z, dtp, l0_conv_w, l0_conv_b, l0_dt_bias, l0_A_log, l0_D, l0_ssm_norm, b, s)
    x2 = _out_proj0(x2, ya, yb.reshape(b * s, -1), l0_out_w)
    x2 = _ffn(x2, l0_ffn_norm, l0_w_gate, l0_w_up, l0_w_down, final_norm, final=False)
    x2 = _short_conv(x2, l1_conv_norm, l1_in_w, l1_conv_w, l1_out_w, s)
    x2 = _ffn(x2, l1_ffn_norm, l1_w_gate, l1_w_up, l1_w_down, final_norm, final=True)
    return x2.reshape(b, s, d)
```

```python
import functools

import jax
import jax.numpy as jnp
import numpy as np
from jax import lax
from jax.experimental import pallas as pl
from jax.experimental.pallas import tpu as pltpu

D_MODEL = 1024
N_HEADS_A = 8
HEAD_DIM_A = 64
KV_LATENT = 256
IDX_HEADS = 8
IDX_DIM = 64
TOPK_MAX = 256
Q_BLOCK = 128
SSM_D_INNER = 1024
SSM_HEADS = 16
SSM_HEAD_DIM = SSM_D_INNER // SSM_HEADS
SSM_GROUPS = 2
SSM_STATE = 128
SSM_CONV = 4
SSM_CHUNK = 128
SC_WIDTH = D_MODEL
SC_CONV = 3
D_FF = -(-8 * D_MODEL // (3 * 256)) * 256
EPS = 1e-6

LANES = 128
SUBLANES = 8
TOKEN_TILE = 512
KEY_CHUNK = 512
KVT_PAD = 16
VMEM_LIMIT = 56 * 1024 * 1024

F32 = jnp.float32
BF16 = jnp.bfloat16
NEG_BIG = -1e30
F32_LOWEST = float(np.finfo(np.float32).min)
KEY_LOWEST = int(np.array(F32_LOWEST, np.float32).view(np.int32)) ^ 0x7FFFFFFF


def _vmem_full():
    return pl.BlockSpec(memory_space=pltpu.VMEM)


def _params(*sem):
    return pltpu.CompilerParams(dimension_semantics=sem, vmem_limit_bytes=VMEM_LIMIT)


def _rms(x, w):
    return x * lax.rsqrt(jnp.mean(x * x, axis=-1, keepdims=True) + EPS) * w


def _dot(a, b):
    return jnp.dot(a, b, preferred_element_type=F32)


def _dot_nt(a, b):
    return lax.dot_general(a, b, (((1,), (1,)), ((), ())), preferred_element_type=F32)


def _in0_kernel(x_ref, g_ref, wqt, wiqt, wikwt, wckv, wikw, wz, wxbc, wdt, kvn_ref,
                qt_o, iqt_o, ikwt_o, ckv_o, ckvt_o, ik_o, z_o, xbc_o, dt_o):
    xn = _rms(x_ref[...], g_ref[...]).astype(BF16)
    qt_o[...] = _dot_nt(wqt[...], xn).astype(BF16)
    iqt_o[...] = _dot_nt(wiqt[...], xn).astype(BF16)
    ikwt_o[...] = _dot_nt(wikwt[...], xn)
    c = _rms(_dot(xn, wckv[...]), kvn_ref[...])
    ckv_o[...] = c.astype(BF16)
    ones_rows = (lax.broadcasted_iota(jnp.int32, (KVT_PAD, c.shape[0]), 0) == 0).astype(F32)
    ckvt_o[0] = jnp.concatenate([c.T, ones_rows], axis=0).astype(BF16)
    ik_o[...] = _dot(xn, wikw[...]).astype(BF16)
    z_o[...] = _dot(xn, wz[...])
    xbc_o[...] = _dot(xn, wxbc[...])
    dt_o[...] = _dot(xn, wdt[...])


def _in_proj0(x2, g, in_w, kv_norm):
    n = x2.shape[0]
    cuts = np.cumsum((512, 256, 512, 64, 8, 1024, 1536, 16))
    w = in_w.astype(BF16)
    wq, wckv, wiq = w[:, :cuts[0]], w[:, cuts[0]:cuts[1]], w[:, cuts[1]:cuts[2]]
    wikw = jnp.pad(w[:, cuts[2]:cuts[4]], ((0, 0), (0, LANES - 72)))
    wz, wxbc = w[:, cuts[4]:cuts[5]], w[:, cuts[5]:cuts[6]]
    wdt = jnp.pad(w[:, cuts[6]:cuts[7]], ((0, 0), (0, LANES - SSM_HEADS)))
    tm = KEY_CHUNK
    row = lambda c: pl.BlockSpec((tm, c), lambda i: (i, 0))
    col = lambda r: pl.BlockSpec((r, tm), lambda i: (0, i))
    hd = N_HEADS_A * HEAD_DIM_A
    out_specs = [col(hd), col(hd), col(LANES), row(KV_LATENT),
                 pl.BlockSpec((1, KV_LATENT + KVT_PAD, tm), lambda i: (i, 0, 0)),
                 row(LANES), row(SSM_D_INNER), row(wxbc.shape[1]), row(LANES)]
    sds = jax.ShapeDtypeStruct
    out_shape = [sds((hd, n), BF16), sds((hd, n), BF16), sds((LANES, n), F32), sds((n, KV_LATENT), BF16),
                 sds((n // tm, KV_LATENT + KVT_PAD, tm), BF16), sds((n, LANES), BF16), sds((n, SSM_D_INNER), F32),
                 sds((n, wxbc.shape[1]), F32), sds((n, LANES), F32)]
    return pl.pallas_call(
        _in0_kernel,
        grid=(n // tm,),
        in_specs=[row(D_MODEL)] + [_vmem_full()] * 10,
        out_specs=out_specs,
        out_shape=out_shape,
        compiler_params=_params("parallel"),
        name="in_proj0",
    )(x2, g.reshape(1, -1), wq.T, wiq.T, wikw.T, wckv, wikw, wz, wxbc, wdt, kv_norm.reshape(1, -1))


def _f2k(x):
    b = pltpu.bitcast(x, jnp.int32)
    return jnp.where(b < 0, b ^ jnp.int32(0x7FFFFFFF), b)


def _k2f(k):
    b = jnp.where(k < 0, k ^ jnp.int32(0x7FFFFFFF), k)
    return pltpu.bitcast(b, F32)


FOLD_ROWS = 8 * SUBLANES
VALUE_PROBES = 10
PROBES_PER_CHECK = 3


def _fold(x, op):
    return op(x.reshape(x.shape[0] // FOLD_ROWS, FOLD_ROWS, LANES), axis=0)


def _any_lane(flag):
    return jnp.max(jnp.where(flag, 1.0, 0.0))


def _dsa_kernel(qt_ref, iqt_ref, wt_ref, ik_ref, ckv_ref, ckvt_ref, wukt_ref, wuvt_ref, o_ref,
                score_scr, qlat_scr, r_scr, acc_scr, m_scr, thr_scr,
                *, topk):
    i = pl.program_id(1)
    kc = KEY_CHUNK
    qb = Q_BLOCK
    nh = N_HEADS_A
    nch = (i * qb) // kc + 1
    t0 = i * qb
    key_i = lax.broadcasted_iota(jnp.int32, (kc, LANES), 0)
    rel_i = key_i - lax.broadcasted_iota(jnp.int32, (kc, LANES), 1)

    qt = qt_ref[...]
    qscale = (HEAD_DIM_A ** -0.5) * float(np.log2(np.e))
    for p in range(nh // 2):
        r = _dot(wukt_ref[p], qt[p * LANES:(p + 1) * LANES, :])
        for v in range(2):
            h = 2 * p + v
            qlat_scr[:, h * qb:(h + 1) * qb] = (r[v * KV_LATENT:(v + 1) * KV_LATENT, :] * qscale).astype(BF16)
    iqt = iqt_ref[...]
    r_scr[...] = jnp.zeros(r_scr.shape, BF16)
    for h in range(IDX_HEADS):
        r_scr[0:IDX_DIM, h * qb:(h + 1) * qb] = iqt[h * IDX_DIM:(h + 1) * IDX_DIM, :]
    wts = wt_ref[...] * ((IDX_HEADS ** -0.5) * (IDX_DIM ** -0.5))

    def p1(j, carry):
        mn, mx, npos, nnonneg = carry
        base = pl.multiple_of(j * kc, kc)
        lt = _dot(ik_ref[0, pl.ds(base, kc), :], r_scr[...])
        acc = None
        for h in range(IDX_HEADS):
            term = jnp.maximum(lt[:, h * qb:(h + 1) * qb], 0.0) * wts[h:h + 1, :]
            acc = term if acc is None else acc + term
        valid = rel_i <= (t0 - base)
        sc = jnp.where(valid, acc, -jnp.inf)
        score_scr[pl.ds(base, kc), :] = sc
        mx = jnp.maximum(mx, _fold(sc, jnp.max))
        mn = jnp.minimum(mn, _fold(jnp.where(valid, acc, jnp.inf), jnp.min))
        npos = npos + _fold(jnp.where(sc > 0.0, 1.0, 0.0), jnp.sum)
        nnonneg = nnonneg + _fold(jnp.where(sc >= 0.0, 1.0, 0.0), jnp.sum)
        return mn, mx, npos, nnonneg

    zero_acc = jnp.zeros((FOLD_ROWS, LANES), F32)
    mn, mx, npos, nnonneg = lax.fori_loop(
        0, nch, p1, (jnp.full((FOLD_ROWS, LANES), jnp.inf, F32), jnp.full((FOLD_ROWS, LANES), -jnp.inf, F32),
                     zero_acc, zero_acc))
    colmin = jnp.min(mn, axis=0, keepdims=True)
    colmax = jnp.max(mx, axis=0, keepdims=True)
    npos = jnp.sum(npos, axis=0, keepdims=True).astype(jnp.int32)
    nnonneg = jnp.sum(nnonneg, axis=0, keepdims=True).astype(jnp.int32)

    def count(pred):
        def body(j, acc):
            base = pl.multiple_of(j * kc, kc)
            blk = score_scr[pl.ds(base, kc), :]
            return acc + _fold(jnp.where(pred(blk, base), 1.0, 0.0), jnp.sum)
        acc = lax.fori_loop(0, nch, body, jnp.zeros((FOLD_ROWS, LANES), F32))
        return jnp.sum(acc, axis=0, keepdims=True).astype(jnp.int32)

    t_pos = t0 + lax.broadcasted_iota(jnp.int32, (1, qb), 1)
    nvalid = t_pos + 1
    allsel = nvalid <= topk
    key_zero, key_tiny, key_negzero = 0, 1, -1
    at_zero = jnp.logical_and(npos < topk, nnonneg >= topk)
    above = npos >= topk
    below = nnonneg < topk
    lo0 = jnp.where(at_zero, key_zero, jnp.where(above, key_tiny, _f2k(colmin)))
    cnt0 = jnp.where(at_zero, nnonneg, jnp.where(above, npos, nvalid))
    hi0 = jnp.where(at_zero, key_zero + 1, jnp.where(below, key_negzero, _f2k(colmax) + 1))
    lo0 = jnp.where(allsel, jnp.int32(KEY_LOWEST), lo0)
    hi0 = jnp.where(allsel, jnp.int32(KEY_LOWEST + 1), hi0)
    cnt0 = jnp.where(allsel, jnp.int32(topk), cnt0)
    go0 = _any_lane(lo0 + 1 < hi0)

    def search_step(lo, hi, cnt_lo, by_value):
        active = lo + 1 < hi
        mid_k = (lo & hi) + ((lo ^ hi) >> 1)
        mid_v = _f2k(0.5 * _k2f(lo) + 0.5 * _k2f(hi - 1))
        mid_v = jnp.minimum(jnp.maximum(mid_v, lo + 1), hi - 1)
        mid = jnp.where(by_value, mid_v, mid_k)
        cand = _k2f(mid)
        cnt = count(lambda blk, base: blk >= cand)
        ge = cnt >= topk
        up = jnp.logical_and(active, ge)
        dn = jnp.logical_and(active, jnp.logical_not(ge))
        hit = jnp.logical_and(active, cnt == topk)
        lo = jnp.where(up, mid, lo)
        hi = jnp.where(hit, mid + 1, jnp.where(dn, mid, hi))
        cnt_lo = jnp.where(up, cnt, cnt_lo)
        return lo, hi, cnt_lo

    def bis_cond(c):
        return c[4] > 0.0

    def bis_body(c):
        lo, hi, cnt_lo, it, _ = c
        for u in range(PROBES_PER_CHECK):
            lo, hi, cnt_lo = search_step(lo, hi, cnt_lo, it + u < VALUE_PROBES)
        return lo, hi, cnt_lo, it + PROBES_PER_CHECK, _any_lane(lo + 1 < hi)

    lo, _, cnt_lo, _, _ = lax.while_loop(bis_cond, bis_body, (lo0, hi0, cnt0, jnp.int32(0), go0))
    thr_scr[...] = _k2f(lo)
    tied = cnt_lo > topk

    @pl.when(_any_lane(tied) > 0.0)
    def _():
        thr = thr_scr[...]
        need = topk - count(lambda blk, base: blk > thr)

        def idx_body(_, c):
            plo, phi = c
            pm = (plo + phi) >> 1
            cnt = count(lambda blk, base: jnp.logical_and(blk == thr, (key_i + base) <= pm))
            ok = cnt >= need
            return jnp.where(ok, plo, pm), jnp.where(ok, pm, phi)

        nbits = int(np.ceil(np.log2(score_scr.shape[0]))) + 1
        _, cut = lax.fori_loop(0, nbits, idx_body,
                               (jnp.full((1, qb), -1, jnp.int32), jnp.broadcast_to(nch * kc - 1, (1, qb))))
        cut = jnp.where(tied, cut, jnp.int32(2 ** 30))

        def fix(j, _):
            base = pl.multiple_of(j * kc, kc)
            blk = score_scr[pl.ds(base, kc), :]
            drop = jnp.logical_and(blk == thr, (key_i + base) > cut)
            score_scr[pl.ds(base, kc), :] = jnp.where(drop, -jnp.inf, blk)
            return 0
        lax.fori_loop(0, nch, fix, 0)

    m_scr[...] = jnp.full(m_scr.shape, NEG_BIG, F32)
    acc_scr[...] = jnp.zeros(acc_scr.shape, F32)

    def p3(j, _):
        base = pl.multiple_of(j * kc, kc)
        kv = ckv_ref[0, pl.ds(base, kc), :]
        kvt = ckvt_ref[0, j]
        bias = jnp.where(score_scr[pl.ds(base, kc), :] >= thr_scr[...], 0.0, NEG_BIG)
        s = _dot(kv, qlat_scr[...])
        ps, alphas = [], []
        for h in range(nh):
            sh = s[:, h * qb:(h + 1) * qb] + bias
            m_old = m_scr[h:h + 1, :]
            m_new = jnp.maximum(m_old, jnp.max(sh, axis=0, keepdims=True))
            ps.append(jnp.exp2(sh - m_new).astype(BF16))
            alphas.append(jnp.exp2(m_old - m_new))
            m_scr[h:h + 1, :] = m_new
        pv = _dot(kvt, jnp.concatenate(ps, axis=1))
        acc_scr[...] = acc_scr[...] * jnp.concatenate(alphas, axis=1) + pv
        return 0

    lax.fori_loop(0, nch, p3, 0)

    inv_l = 1.0 / acc_scr[KV_LATENT:KV_LATENT + 1, :]
    outs = []
    for h in range(nh):
        cols = slice(h * qb, (h + 1) * qb)
        olat = (acc_scr[0:KV_LATENT, cols] * inv_l[:, cols]).astype(BF16)
        outs.append(_dot(wuvt_ref[h], olat))
    o_ref[...] = jnp.concatenate(outs, axis=0).T.astype(BF16)


def _dsa(qt, iqt, ikwt, ik, ckv, ckvt, w_uk, w_uv, b, s):
    kc = KEY_CHUNK
    nb = s // Q_BLOCK
    nchunks = s // kc
    topk = min(TOPK_MAX, s // 4)
    hd = N_HEADS_A * HEAD_DIM_A
    wukt = jnp.swapaxes(w_uk, 1, 2).reshape(N_HEADS_A // 2, 2, KV_LATENT, HEAD_DIM_A)
    eye2 = jnp.eye(2, dtype=F32)
    wukt = jnp.einsum('pvcd,vu->pvcud', wukt, eye2).reshape(N_HEADS_A // 2, 2 * KV_LATENT, 2 * HEAD_DIM_A)
    wuvt = jnp.swapaxes(w_uv, 1, 2)
    qcol = lambda r: pl.BlockSpec((r, Q_BLOCK), lambda bi, i: (0, bi * nb + i))
    kern = functools.partial(_dsa_kernel, topk=topk)
    return pl.pallas_call(
        kern,
        grid=(b, nb),
        in_specs=[qcol(hd), qcol(IDX_HEADS * IDX_DIM),
                  pl.BlockSpec((IDX_HEADS, Q_BLOCK), lambda bi, i: (IDX_DIM // IDX_HEADS, bi * nb + i)),
                  pl.BlockSpec((1, s, LANES), lambda bi, i: (bi, 0, 0)),
                  pl.BlockSpec((1, s, KV_LATENT), lambda bi, i: (bi, 0, 0)),
                  pl.BlockSpec((1, nchunks, KV_LATENT + KVT_PAD, kc), lambda bi, i: (bi, 0, 0, 0)),
                  _vmem_full(), _vmem_full()],
        out_specs=pl.BlockSpec((Q_BLOCK, hd), lambda bi, i: (bi * nb + i, 0)),
        out_shape=jax.ShapeDtypeStruct((b * s, hd), BF16),
        scratch_shapes=[
            pltpu.VMEM((s, Q_BLOCK), F32),
            pltpu.VMEM((KV_LATENT, N_HEADS_A * Q_BLOCK), BF16),
            pltpu.VMEM((LANES, IDX_HEADS * Q_BLOCK), BF16),
            pltpu.VMEM((KV_LATENT + KVT_PAD, N_HEADS_A * Q_BLOCK), F32),
            pltpu.VMEM((N_HEADS_A, Q_BLOCK), F32),
            pltpu.VMEM((1, Q_BLOCK), F32),
        ],
        compiler_params=_params("parallel", "arbitrary"),
        name="dsa_attention",
    )(qt, iqt, ikwt, ik.reshape(b, s, LANES), ckv.reshape(b, s, KV_LATENT),
      ckvt.reshape(b, nchunks, KV_LATENT + KVT_PAD, kc), wukt.astype(BF16), wuvt.astype(BF16))


def _causal_conv(x, w, halo):
    taps = w.shape[0]
    top_row = lax.broadcasted_iota(jnp.int32, halo.shape, 0)
    y = x * w[taps - 1:taps, :]
    for k in range(1, taps):
        xk = pltpu.roll(x, k, 0)
        top = jnp.where(top_row < k, pltpu.roll(halo, k, 0), xk[:SUBLANES, :])
        xk = jnp.concatenate([top, xk[SUBLANES:, :]], axis=0)
        y = y + xk * w[taps - 1 - k:taps - k, :]
    return y


def _split3(x):
    p1 = x.astype(BF16)
    r = x - p1.astype(F32)
    p2 = r.astype(BF16)
    return p1, p2, (r - p2.astype(F32)).astype(BF16)


def _ssd_kernel(xbc_ref, z_ref, dt_ref, cw_ref, cb_ref, dtb_ref, alog_ref, dexp_ref, nw_ref, e_ref,
                o_ref, halo_scr, state_scr):
    c = pl.program_id(1)
    L = SSM_CHUNK
    n = SSM_STATE
    gw = SSM_D_INNER // SSM_GROUPS

    @pl.when(c == 0)
    def _():
        halo_scr[...] = jnp.zeros(halo_scr.shape, F32)
        state_scr[...] = jnp.zeros(state_scr.shape, F32)

    xbc = xbc_ref[0]
    conv = _causal_conv(xbc, cw_ref[...], halo_scr[...]) + cb_ref[...]
    halo_scr[...] = xbc[L - SUBLANES:, :]
    act = conv * jax.nn.sigmoid(conv)
    xs = act[:, :SSM_D_INNER]
    bm = act[:, SSM_D_INNER:SSM_D_INNER + SSM_GROUPS * n]
    cm = act[:, SSM_D_INNER + SSM_GROUPS * n:]

    dt = jax.nn.softplus(dt_ref[0] + dtb_ref[...])
    a = dt * (-jnp.exp(alog_ref[...]))
    ri = lax.broadcasted_iota(jnp.int32, (L, L), 0)
    ci = lax.broadcasted_iota(jnp.int32, (L, L), 1)
    tri = ri >= ci
    tri_b = jnp.where(tri, 1.0, 0.0).astype(BF16)
    cs3 = _dot(tri_b, jnp.concatenate(_split3(a), axis=1))
    a_cs = cs3[:, :LANES] + cs3[:, LANES:2 * LANES] + cs3[:, 2 * LANES:]
    a_cs_t = a_cs.T
    w_end = dt * jnp.exp(a_cs[L - 1:L, :] - a_cs)
    grow_c = jnp.exp(a_cs)
    pieces = [p for arr in (dt, w_end, grow_c) for p in _split3(arr)]
    ex = _dot(jnp.concatenate(pieces, axis=0), e_ref[...])
    dt_e, wend_e, grow = (ex[3 * i * L:(3 * i + 1) * L] + ex[(3 * i + 1) * L:(3 * i + 2) * L]
                          + ex[(3 * i + 2) * L:(3 * i + 3) * L] for i in range(3))
    xdt_b = (xs * dt_e).astype(BF16)
    xend_b = (xs * wend_e).astype(BF16)
    lane = lax.broadcasted_iota(jnp.int32, (L, LANES), 1)

    y_parts = []
    for g in range(SSM_GROUPS):
        bg = bm[:, g * n:(g + 1) * n]
        cg = cm[:, g * n:(g + 1) * n].astype(BF16)
        cbm = _dot_nt(cg, bg.astype(BF16))
        for pr in range(gw // LANES):
            col = g * gw + pr * LANES
            xpair = xdt_b[:, col:col + LANES]
            outs = []
            for v in range(2):
                h = (col // SSM_HEAD_DIM) + v
                seg = a_cs[:, h:h + 1] - a_cs_t[h:h + 1, :]
                dec = jnp.exp(jnp.where(tri, seg, -jnp.inf))
                outs.append(_dot((cbm * dec).astype(BF16), xpair))
            y_parts.append(jnp.where(lane < SSM_HEAD_DIM, outs[0], outs[1]))
    y = jnp.concatenate(y_parts, axis=1)

    offs = []
    for g in range(SSM_GROUPS):
        sl = slice(g * gw, (g + 1) * gw)
        bg_t = bm[:, g * n:(g + 1) * n].T.astype(BF16)
        cg = cm[:, g * n:(g + 1) * n].astype(BF16)
        st = state_scr[g]
        offs.append(_dot(cg, st.astype(BF16)))
        state_scr[g] = st * grow[L - 1:L, sl] + _dot(bg_t, xend_b[:, sl])
    y = y + jnp.concatenate(offs, axis=1) * grow + xs * dexp_ref[...]

    z = z_ref[0]
    y = y * (z * jax.nn.sigmoid(z))
    nw = nw_ref[...]
    outs = []
    for g in range(SSM_GROUPS):
        sl = slice(g * gw, (g + 1) * gw)
        outs.append(_rms(y[:, sl], nw[:, sl]))
    o_ref[0] = jnp.concatenate(outs, axis=1).astype(BF16)


def _ssd(xbc, z, dtp, conv_w, conv_b, dt_bias, a_log, d, norm_w, b, s):
    L = SSM_CHUNK
    cx = SSM_D_INNER + 2 * SSM_GROUPS * SSM_STATE
    pad = LANES - SSM_HEADS
    expand = jnp.repeat(jnp.eye(SSM_HEADS, dtype=F32), SSM_HEAD_DIM, axis=1)
    expand = jnp.pad(expand, ((0, pad), (0, 0)))
    blk = lambda c: pl.BlockSpec((1, L, c), lambda bi, i: (bi, i, 0))
    return pl.pallas_call(
        _ssd_kernel,
        grid=(b, s // L),
        in_specs=[blk(cx), blk(SSM_D_INNER), blk(LANES)] + [_vmem_full()] * 7,
        out_specs=blk(SSM_D_INNER),
        out_shape=jax.ShapeDtypeStruct((b, s, SSM_D_INNER), BF16),
        scratch_shapes=[pltpu.VMEM((SUBLANES, cx), F32),
                        pltpu.VMEM((SSM_GROUPS, SSM_STATE, SSM_D_INNER // SSM_GROUPS), F32)],
        compiler_params=_params("parallel", "arbitrary"),
        name="ssd_mixer",
    )(xbc.reshape(b, s, cx), z.reshape(b, s, SSM_D_INNER), dtp.reshape(b, s, LANES),
      conv_w, conv_b.reshape(1, -1), jnp.pad(dt_bias, (0, pad)).reshape(1, -1),
      jnp.pad(a_log, (0, pad)).reshape(1, -1), jnp.repeat(d, SSM_HEAD_DIM).reshape(1, -1),
      norm_w.reshape(1, -1), expand.astype(BF16))


def _out0_kernel(x_ref, ya_ref, yb_ref, wa, wb, o_ref):
    o_ref[...] = x_ref[...] + _dot(ya_ref[...], wa[...]) + _dot(yb_ref[...], wb[...])


def _out_proj0(x2, ya, yb, out_w):
    n = x2.shape[0]
    tm = TOKEN_TILE
    ca = N_HEADS_A * HEAD_DIM_A
    w = out_w.astype(BF16)
    row = lambda c: pl.BlockSpec((tm, c), lambda i: (i, 0))
    return pl.pallas_call(
        _out0_kernel,
        grid=(n // tm,),
        in_specs=[row(D_MODEL), row(ca), row(SSM_D_INNER), _vmem_full(), _vmem_full()],
        out_specs=row(D_MODEL),
        out_shape=jax.ShapeDtypeStruct((n, D_MODEL), F32),
        compiler_params=_params("parallel"),
        name="out_proj0",
    )(x2, ya, yb, w[:ca], w[ca:])


def _ffn_kernel(x_ref, g_ref, wg, wu, wd, fn_ref, o_ref, *, final):
    x = x_ref[...]
    hn = _rms(x, g_ref[...]).astype(BF16)
    a = _dot(hn, wg[...])
    u = _dot(hn, wu[...])
    act = (a * jax.nn.sigmoid(a) * u).astype(BF16)
    y = x + _dot(act, wd[...])
    if final:
        y = _rms(y, fn_ref[...])
    o_ref[...] = y


def _ffn(x2, g, w_gate, w_up, w_down, final_norm, final):
    n = x2.shape[0]
    tm = TOKEN_TILE
    row = pl.BlockSpec((tm, D_MODEL), lambda i: (i, 0))
    return pl.pallas_call(
        functools.partial(_ffn_kernel, final=final),
        grid=(n // tm,),
        in_specs=[row] + [_vmem_full()] * 5,
        out_specs=row,
        out_shape=jax.ShapeDtypeStruct((n, D_MODEL), F32),
        compiler_params=_params("parallel"),
        name="ffn_final" if final else "ffn",
    )(x2, g.reshape(1, -1), w_gate.astype(BF16), w_up.astype(BF16), w_down.astype(BF16),
      final_norm.reshape(1, -1))


def _conv_in_kernel(x_ref, g_ref, wb, wc, wv, gb_o, u_o):
    xn = _rms(x_ref[...], g_ref[...]).astype(BF16)
    gb_o[...] = _dot(xn, wb[...])
    u_o[...] = _dot(xn, wc[...]) * _dot(xn, wv[...])


def _conv_out_kernel(x_ref, gb_ref, u_ref, halo_ref, cw_ref, wo, o_ref, *, tiles_per_seq):
    i = pl.program_id(0)
    halo = jnp.where(i % tiles_per_seq == 0, 0.0, halo_ref[...])
    conv = _causal_conv(u_ref[...], cw_ref[...], halo)
    y = (gb_ref[...] * conv).astype(BF16)
    o_ref[...] = x_ref[...] + _dot(y, wo[...])


def _short_conv(x2, norm_w, in_w, conv_w, out_w, s):
    n = x2.shape[0]
    tm = TOKEN_TILE
    w = in_w.astype(BF16)
    row = pl.BlockSpec((tm, D_MODEL), lambda i: (i, 0))
    gb, u = pl.pallas_call(
        _conv_in_kernel,
        grid=(n // tm,),
        in_specs=[row] + [_vmem_full()] * 4,
        out_specs=[row, row],
        out_shape=[jax.ShapeDtypeStruct((n, SC_WIDTH), F32)] * 2,
        compiler_params=_params("parallel"),
        name="conv_in",
    )(x2, norm_w.reshape(1, -1), w[:, :SC_WIDTH], w[:, SC_WIDTH:2 * SC_WIDTH], w[:, 2 * SC_WIDTH:])
    per = tm // SUBLANES
    halo = pl.BlockSpec((SUBLANES, SC_WIDTH), lambda i: (jnp.maximum(i * per - 1, 0), 0))
    return pl.pallas_call(
        functools.partial(_conv_out_kernel, tiles_per_seq=s // tm),
        grid=(n // tm,),
        in_specs=[row, row, row, halo, _vmem_full(), _vmem_full()],
        out_specs=row,
        out_shape=jax.ShapeDtypeStruct((n, D_MODEL), F32),
        compiler_params=_params("parallel"),
        name="conv_out",
    )(x2, gb, u, u, conv_w, out_w.astype(BF16))


def kernel(x, l0_attn_norm, l0_in_w, l0_kv_norm, l0_w_uk, l0_w_uv, l0_conv_w, l0_conv_b, l0_dt_bias,
           l0_A_log, l0_D, l0_ssm_norm, l0_out_w, l0_ffn_norm, l0_w_gate, l0_w_up, l0_w_down,
           l1_conv_norm, l1_in_w, l1_conv_w, l1_out_w, l1_ffn_norm, l1_w_gate, l1_w_up, l1_w_down,
           final_norm):
    b, s, d = x.shape
    assert d == D_MODEL and s % TOKEN_TILE == 0 and s % KEY_CHUNK == 0 and s % SSM_CHUNK == 0
    x2 = x.reshape(b * s, d)
    qt, iqt, ikwt, ckv, ckvt, ik, z, xbc, dtp = _in_proj0(x2, l0_attn_norm, l0_in_w, l0_kv_norm)
    ya = _dsa(qt, iqt, ikwt, ik, ckv, ckvt, l0_w_uk, l0_w_uv, b, s)
    yb = _ssd(xbc, z, dtp, l0_conv_w, l0_conv_b, l0_dt_bias, l0_A_log, l0_D, l0_ssm_norm, b, s)
    x2 = _out_proj0(x2, ya, yb.reshape(b * s, -1), l0_out_w)
    x2 = _ffn(x2, l0_ffn_norm, l0_w_gate, l0_w_up, l0_w_down, final_norm, final=False)
    x2 = _short_conv(x2, l1_conv_norm, l1_in_w, l1_conv_w, l1_out_w, s)
    x2 = _ffn(x2, l1_ffn_norm, l1_w_gate, l1_w_up, l1_w_down, final_norm, final=True)
    return x2.reshape(b, s, d)
```

```python
import functools

import jax
import jax.numpy as jnp
import numpy as np
from jax import lax
from jax.experimental import pallas as pl
from jax.experimental.pallas import tpu as pltpu

D_MODEL = 1024
N_HEADS_A = 8
HEAD_DIM_A = 64
KV_LATENT = 256
IDX_HEADS = 8
IDX_DIM = 64
TOPK_MAX = 256
Q_BLOCK = 128
SSM_D_INNER = 1024
SSM_HEADS = 16
SSM_HEAD_DIM = SSM_D_INNER // SSM_HEADS
SSM_GROUPS = 2
SSM_STATE = 128
SSM_CONV = 4
SSM_CHUNK = 128
SC_WIDTH = D_MODEL
SC_CONV = 3
D_FF = -(-8 * D_MODEL // (3 * 256)) * 256
EPS = 1e-6

LANES = 128
SUBLANES = 8
TOKEN_TILE = 512
KEY_CHUNK = 512
KVT_PAD = 16
VMEM_LIMIT = 56 * 1024 * 1024

F32 = jnp.float32
BF16 = jnp.bfloat16
NEG_BIG = -1e30
F32_LOWEST = float(np.finfo(np.float32).min)
KEY_LOWEST = int(np.array(F32_LOWEST, np.float32).view(np.int32)) ^ 0x7FFFFFFF


def _vmem_full():
    return pl.BlockSpec(memory_space=pltpu.VMEM)


def _params(*sem):
    return pltpu.CompilerParams(dimension_semantics=sem, vmem_limit_bytes=VMEM_LIMIT)


def _rms(x, w):
    return x * lax.rsqrt(jnp.mean(x * x, axis=-1, keepdims=True) + EPS) * w


def _dot(a, b):
    return jnp.dot(a, b, preferred_element_type=F32)


def _dot_nt(a, b):
    return lax.dot_general(a, b, (((1,), (1,)), ((), ())), preferred_element_type=F32)


def _in0_kernel(x_ref, g_ref, wqt, wiqt, wikwt, wckv, wikw, wz, wxbc, wdt, kvn_ref,
                qt_o, iqt_o, ikwt_o, ckv_o, ckvt_o, ik_o, z_o, xbc_o, dt_o):
    xn = _rms(x_ref[...], g_ref[...]).astype(BF16)
    qt_o[...] = _dot_nt(wqt[...], xn).astype(BF16)
    iqt_o[...] = _dot_nt(wiqt[...], xn).astype(BF16)
    ikwt_o[...] = _dot_nt(wikwt[...], xn)
    c = _rms(_dot(xn, wckv[...]), kvn_ref[...])
    ckv_o[...] = c.astype(BF16)
    ones_rows = (lax.broadcasted_iota(jnp.int32, (KVT_PAD, c.shape[0]), 0) == 0).astype(F32)
    ckvt_o[0] = jnp.concatenate([c.T, ones_rows], axis=0).astype(BF16)
    ik_o[...] = _dot(xn, wikw[...]).astype(BF16)
    z_o[...] = _dot(xn, wz[...])
    xbc_o[...] = _dot(xn, wxbc[...])
    dt_o[...] = _dot(xn, wdt[...])


def _in_proj0(x2, g, in_w, kv_norm):
    n = x2.shape[0]
    cuts = np.cumsum((512, 256, 512, 64, 8, 1024, 1536, 16))
    w = in_w.astype(BF16)
    wq, wckv, wiq = w[:, :cuts[0]], w[:, cuts[0]:cuts[1]], w[:, cuts[1]:cuts[2]]
    wikw = jnp.pad(w[:, cuts[2]:cuts[4]], ((0, 0), (0, LANES - 72)))
    wz, wxbc = w[:, cuts[4]:cuts[5]], w[:, cuts[5]:cuts[6]]
    wdt = jnp.pad(w[:, cuts[6]:cuts[7]], ((0, 0), (0, LANES - SSM_HEADS)))
    tm = KEY_CHUNK
    row = lambda c: pl.BlockSpec((tm, c), lambda i: (i, 0))
    col = lambda r: pl.BlockSpec((r, tm), lambda i: (0, i))
    hd = N_HEADS_A * HEAD_DIM_A
    out_specs = [col(hd), col(hd), col(LANES), row(KV_LATENT),
                 pl.BlockSpec((1, KV_LATENT + KVT_PAD, tm), lambda i: (i, 0, 0)),
                 row(LANES), row(SSM_D_INNER), row(wxbc.shape[1]), row(LANES)]
    sds = jax.ShapeDtypeStruct
    out_shape = [sds((hd, n), BF16), sds((hd, n), BF16), sds((LANES, n), F32), sds((n, KV_LATENT), BF16),
                 sds((n // tm, KV_LATENT + KVT_PAD, tm), BF16), sds((n, LANES), BF16), sds((n, SSM_D_INNER), F32),
                 sds((n, wxbc.shape[1]), F32), sds((n, LANES), F32)]
    return pl.pallas_call(
        _in0_kernel,
        grid=(n // tm,),
        in_specs=[row(D_MODEL)] + [_vmem_full()] * 10,
        out_specs=out_specs,
        out_shape=out_shape,
        compiler_params=_params("parallel"),
        name="in_proj0",
    )(x2, g.reshape(1, -1), wq.T, wiq.T, wikw.T, wckv, wikw, wz, wxbc, wdt, kv_norm.reshape(1, -1))


def _f2k(x):
    b = pltpu.bitcast(x, jnp.int32)
    return jnp.where(b < 0, b ^ jnp.int32(0x7FFFFFFF), b)


def _k2f(k):
    b = jnp.where(k < 0, k ^ jnp.int32(0x7FFFFFFF), k)
    return pltpu.bitcast(b, F32)


FOLD_ROWS = 8 * SUBLANES
VALUE_PROBES = 10
PROBES_PER_CHECK = 3


def _fold(x, op):
    return op(x.reshape(x.shape[0] // FOLD_ROWS, FOLD_ROWS, LANES), axis=0)


def _any_lane(flag):
    return jnp.max(jnp.where(flag, 1.0, 0.0))


def _dsa_kernel(qt_ref, iqt_ref, wt_ref, iqtn_ref, wtn_ref, ik_ref, ckv_ref, ckvt_ref, wukt_ref, wuvt_ref, o_ref,
                score_scr, qlat_scr, r_scr, acc_scr, m_scr, thr_scr, part_scr, stat_scr,
                *, topk, nblocks):
    i = pl.program_id(1)
    kc = KEY_CHUNK
    qb = Q_BLOCK
    nh = N_HEADS_A
    cur = i % 2
    nxt = 1 - cur
    inext = jnp.minimum(i + 1, nblocks - 1)
    nch = (i * qb) // kc + 1
    nch_next = (inext * qb) // kc + 1
    t0 = i * qb
    key_i = lax.broadcasted_iota(jnp.int32, (kc, LANES), 0)
    rel_i = key_i - lax.broadcasted_iota(jnp.int32, (kc, LANES), 1)

    wscale = (IDX_HEADS ** -0.5) * (IDX_DIM ** -0.5)

    def load_indexer_queries(iqt):
        r_scr[...] = jnp.zeros(r_scr.shape, BF16)
        for h in range(IDX_HEADS):
            r_scr[0:IDX_DIM, h * qb:(h + 1) * qb] = iqt[h * IDX_DIM:(h + 1) * IDX_DIM, :]

    def reset_partials():
        part_scr[0] = jnp.full((FOLD_ROWS, LANES), jnp.inf, F32)
        part_scr[1] = jnp.full((FOLD_ROWS, LANES), -jnp.inf, F32)
        part_scr[2] = jnp.zeros((FOLD_ROWS, LANES), F32)
        part_scr[3] = jnp.zeros((FOLD_ROWS, LANES), F32)

    def score_chunk(j, buf, wts, tq0):
        base = pl.multiple_of(j * kc, kc)
        lt = _dot(ik_ref[0, pl.ds(base, kc), :], r_scr[...])
        acc = None
        for h in range(IDX_HEADS):
            term = jnp.maximum(lt[:, h * qb:(h + 1) * qb], 0.0) * wts[h:h + 1, :]
            acc = term if acc is None else acc + term
        sc = jnp.where(rel_i <= (tq0 - base), acc, -jnp.inf)
        score_scr[buf, pl.ds(base, kc), :] = sc
        part_scr[0] = jnp.minimum(part_scr[0], _fold(acc, jnp.min))
        part_scr[1] = jnp.maximum(part_scr[1], _fold(acc, jnp.max))
        part_scr[2] = part_scr[2] + _fold(jnp.where(sc > 0.0, 1.0, 0.0), jnp.sum)
        part_scr[3] = part_scr[3] + _fold(jnp.where(sc >= 0.0, 1.0, 0.0), jnp.sum)

    def publish_stats():
        stat_scr[0:1, :] = jnp.min(part_scr[0], axis=0, keepdims=True)
        stat_scr[1:2, :] = jnp.max(part_scr[1], axis=0, keepdims=True)
        stat_scr[2:3, :] = jnp.sum(part_scr[2], axis=0, keepdims=True)
        stat_scr[3:4, :] = jnp.sum(part_scr[3], axis=0, keepdims=True)

    @pl.when(i == 0)
    def _():
        load_indexer_queries(iqt_ref[...])
        reset_partials()
        score_chunk(0, cur, wt_ref[...] * wscale, t0)
        publish_stats()

    colmin = stat_scr[0:1, :]
    colmax = stat_scr[1:2, :]
    npos = stat_scr[2:3, :].astype(jnp.int32)
    nnonneg = stat_scr[3:4, :].astype(jnp.int32)

    def count(pred):
        def body(j, acc):
            base = pl.multiple_of(j * kc, kc)
            blk = score_scr[cur, pl.ds(base, kc), :]
            return acc + _fold(jnp.where(pred(blk, base), 1.0, 0.0), jnp.sum)
        acc = lax.fori_loop(0, nch, body, jnp.zeros((FOLD_ROWS, LANES), F32))
        return jnp.sum(acc, axis=0, keepdims=True).astype(jnp.int32)

    t_pos = t0 + lax.broadcasted_iota(jnp.int32, (1, qb), 1)
    nvalid = t_pos + 1
    allsel = nvalid <= topk
    key_zero, key_tiny, key_negzero = 0, 1, -1
    at_zero = jnp.logical_and(npos < topk, nnonneg >= topk)
    above = npos >= topk
    below = nnonneg < topk
    lo0 = jnp.where(at_zero, key_zero, jnp.where(above, key_tiny, _f2k(colmin)))
    cnt0 = jnp.where(at_zero, nnonneg, jnp.where(above, npos, nvalid))
    hi0 = jnp.where(at_zero, key_zero + 1, jnp.where(below, key_negzero, _f2k(colmax) + 1))
    lo0 = jnp.where(allsel, jnp.int32(KEY_LOWEST), lo0)
    hi0 = jnp.where(allsel, jnp.int32(KEY_LOWEST + 1), hi0)
    cnt0 = jnp.where(allsel, jnp.int32(topk), cnt0)
    go0 = _any_lane(lo0 + 1 < hi0)

    def search_step(lo, hi, cnt_lo, by_value):
        active = lo + 1 < hi
        mid_k = (lo & hi) + ((lo ^ hi) >> 1)
        mid_v = _f2k(0.5 * _k2f(lo) + 0.5 * _k2f(hi - 1))
        mid_v = jnp.minimum(jnp.maximum(mid_v, lo + 1), hi - 1)
        mid = jnp.where(by_value, mid_v, mid_k)
        cand = _k2f(mid)
        cnt = count(lambda blk, base: blk >= cand)
        ge = cnt >= topk
        up = jnp.logical_and(active, ge)
        dn = jnp.logical_and(active, jnp.logical_not(ge))
        hit = jnp.logical_and(active, cnt == topk)
        lo = jnp.where(up, mid, lo)
        hi = jnp.where(hit, mid + 1, jnp.where(dn, mid, hi))
        cnt_lo = jnp.where(up, cnt, cnt_lo)
        return lo, hi, cnt_lo

    def bis_cond(c):
        return c[4] > 0.0

    def bis_body(c):
        lo, hi, cnt_lo, it, _ = c
        for u in range(PROBES_PER_CHECK):
            lo, hi, cnt_lo = search_step(lo, hi, cnt_lo, it + u < VALUE_PROBES)
        return lo, hi, cnt_lo, it + PROBES_PER_CHECK, _any_lane(lo + 1 < hi)

    lo, _, cnt_lo, _, _ = lax.while_loop(bis_cond, bis_body, (lo0, hi0, cnt0, jnp.int32(0), go0))
    thr_scr[...] = _k2f(lo)
    tied = cnt_lo > topk

    @pl.when(_any_lane(tied) > 0.0)
    def _():
        thr = thr_scr[...]
        need = topk - count(lambda blk, base: blk > thr)

        def idx_body(_, c):
            plo, phi = c
            pm = (plo + phi) >> 1
            cnt = count(lambda blk, base: jnp.logical_and(blk == thr, (key_i + base) <= pm))
            ok = cnt >= need
            return jnp.where(ok, plo, pm), jnp.where(ok, pm, phi)

        nbits = int(np.ceil(np.log2(score_scr.shape[1]))) + 1
        _, cut = lax.fori_loop(0, nbits, idx_body,
                               (jnp.full((1, qb), -1, jnp.int32), jnp.broadcast_to(nch * kc - 1, (1, qb))))
        cut = jnp.where(tied, cut, jnp.int32(2 ** 30))

        def fix(j, _):
            base = pl.multiple_of(j * kc, kc)
            blk = score_scr[cur, pl.ds(base, kc), :]
            drop = jnp.logical_and(blk == thr, (key_i + base) > cut)
            score_scr[cur, pl.ds(base, kc), :] = jnp.where(drop, -jnp.inf, blk)
            return 0
        lax.fori_loop(0, nch, fix, 0)

    qt = qt_ref[...]
    qscale = (HEAD_DIM_A ** -0.5) * float(np.log2(np.e))
    for p in range(nh // 2):
        r = _dot(wukt_ref[p], qt[p * LANES:(p + 1) * LANES, :])
        for v in range(2):
            h = 2 * p + v
            qlat_scr[:, h * qb:(h + 1) * qb] = (r[v * KV_LATENT:(v + 1) * KV_LATENT, :] * qscale).astype(BF16)
    load_indexer_queries(iqtn_ref[...])
    wts_next = wtn_ref[...] * wscale
    tn0 = inext * qb
    reset_partials()

    m_scr[...] = jnp.full(m_scr.shape, NEG_BIG, F32)
    acc_scr[...] = jnp.zeros(acc_scr.shape, F32)

    def sweep(j, _):
        base = pl.multiple_of(j * kc, kc)
        kv = ckv_ref[0, pl.ds(base, kc), :]
        kvt = ckvt_ref[0, j]
        bias = jnp.where(score_scr[cur, pl.ds(base, kc), :] >= thr_scr[...], 0.0, NEG_BIG)
        s = _dot(kv, qlat_scr[...])
        ps, alphas = [], []
        for h in range(nh):
            sh = s[:, h * qb:(h + 1) * qb] + bias
            m_old = m_scr[h:h + 1, :]
            m_new = jnp.maximum(m_old, jnp.max(sh, axis=0, keepdims=True))
            ps.append(jnp.exp2(sh - m_new).astype(BF16))
            alphas.append(jnp.exp2(m_old - m_new))
            m_scr[h:h + 1, :] = m_new
        pv = _dot(kvt, jnp.concatenate(ps, axis=1))
        acc_scr[...] = acc_scr[...] * jnp.concatenate(alphas, axis=1) + pv
        score_chunk(j, nxt, wts_next, tn0)
        return 0

    lax.fori_loop(0, nch, sweep, 0)

    @pl.when(nch_next > nch)
    def _():
        score_chunk(nch, nxt, wts_next, tn0)

    publish_stats()

    inv_l = 1.0 / acc_scr[KV_LATENT:KV_LATENT + 1, :]
    outs = []
    for h in range(nh):
        cols = slice(h * qb, (h + 1) * qb)
        olat = (acc_scr[0:KV_LATENT, cols] * inv_l[:, cols]).astype(BF16)
        outs.append(_dot(wuvt_ref[h], olat))
    o_ref[...] = jnp.concatenate(outs, axis=0).T.astype(BF16)


def _dsa(qt, iqt, ikwt, ik, ckv, ckvt, w_uk, w_uv, b, s):
    kc = KEY_CHUNK
    nb = s // Q_BLOCK
    nchunks = s // kc
    topk = min(TOPK_MAX, s // 4)
    hd = N_HEADS_A * HEAD_DIM_A
    wukt = jnp.swapaxes(w_uk, 1, 2).reshape(N_HEADS_A // 2, 2, KV_LATENT, HEAD_DIM_A)
    eye2 = jnp.eye(2, dtype=F32)
    wukt = jnp.einsum('pvcd,vu->pvcud', wukt, eye2).reshape(N_HEADS_A // 2, 2 * KV_LATENT, 2 * HEAD_DIM_A)
    wuvt = jnp.swapaxes(w_uv, 1, 2)
    this_blk = lambda bi, i: bi * nb + i
    next_blk = lambda bi, i: bi * nb + jnp.minimum(i + 1, nb - 1)
    qcol = lambda r, blk: pl.BlockSpec((r, Q_BLOCK), lambda bi, i: (0, blk(bi, i)))
    wrow = lambda blk: pl.BlockSpec((IDX_HEADS, Q_BLOCK), lambda bi, i: (IDX_DIM // IDX_HEADS, blk(bi, i)))
    kern = functools.partial(_dsa_kernel, topk=topk, nblocks=nb)
    return pl.pallas_call(
        kern,
        grid=(b, nb),
        in_specs=[qcol(hd, this_blk), qcol(IDX_HEADS * IDX_DIM, this_blk), wrow(this_blk),
                  qcol(IDX_HEADS * IDX_DIM, next_blk), wrow(next_blk),
                  pl.BlockSpec((1, s, LANES), lambda bi, i: (bi, 0, 0)),
                  pl.BlockSpec((1, s, KV_LATENT), lambda bi, i: (bi, 0, 0)),
                  pl.BlockSpec((1, nchunks, KV_LATENT + KVT_PAD, kc), lambda bi, i: (bi, 0, 0, 0)),
                  _vmem_full(), _vmem_full()],
        out_specs=pl.BlockSpec((Q_BLOCK, hd), lambda bi, i: (bi * nb + i, 0)),
        out_shape=jax.ShapeDtypeStruct((b * s, hd), BF16),
        scratch_shapes=[
            pltpu.VMEM((2, s, Q_BLOCK), F32),
            pltpu.VMEM((KV_LATENT, N_HEADS_A * Q_BLOCK), BF16),
            pltpu.VMEM((LANES, IDX_HEADS * Q_BLOCK), BF16),
            pltpu.VMEM((KV_LATENT + KVT_PAD, N_HEADS_A * Q_BLOCK), F32),
            pltpu.VMEM((N_HEADS_A, Q_BLOCK), F32),
            pltpu.VMEM((1, Q_BLOCK), F32),
            pltpu.VMEM((4, FOLD_ROWS, Q_BLOCK), F32),
            pltpu.VMEM((SUBLANES, Q_BLOCK), F32),
        ],
        compiler_params=_params("parallel", "arbitrary"),
        name="dsa_attention",
    )(qt, iqt, ikwt, iqt, ikwt, ik.reshape(b, s, LANES), ckv.reshape(b, s, KV_LATENT),
      ckvt.reshape(b, nchunks, KV_LATENT + KVT_PAD, kc), wukt.astype(BF16), wuvt.astype(BF16))


def _causal_conv(x, w, halo):
    taps = w.shape[0]
    top_row = lax.broadcasted_iota(jnp.int32, halo.shape, 0)
    y = x * w[taps - 1:taps, :]
    for k in range(1, taps):
        xk = pltpu.roll(x, k, 0)
        top = jnp.where(top_row < k, pltpu.roll(halo, k, 0), xk[:SUBLANES, :])
        xk = jnp.concatenate([top, xk[SUBLANES:, :]], axis=0)
        y = y + xk * w[taps - 1 - k:taps - k, :]
    return y


def _split3(x):
    p1 = x.astype(BF16)
    r = x - p1.astype(F32)
    p2 = r.astype(BF16)
    return p1, p2, (r - p2.astype(F32)).astype(BF16)


def _ssd_kernel(xbc_ref, z_ref, dt_ref, cw_ref, cb_ref, dtb_ref, alog_ref, dexp_ref, nw_ref, e_ref,
                o_ref, halo_scr, state_scr):
    c = pl.program_id(1)
    L = SSM_CHUNK
    n = SSM_STATE
    gw = SSM_D_INNER // SSM_GROUPS

    @pl.when(c == 0)
    def _():
        halo_scr[...] = jnp.zeros(halo_scr.shape, F32)
        state_scr[...] = jnp.zeros(state_scr.shape, F32)

    xbc = xbc_ref[0]
    conv = _causal_conv(xbc, cw_ref[...], halo_scr[...]) + cb_ref[...]
    halo_scr[...] = xbc[L - SUBLANES:, :]
    act = conv * jax.nn.sigmoid(conv)
    xs = act[:, :SSM_D_INNER]
    bm = act[:, SSM_D_INNER:SSM_D_INNER + SSM_GROUPS * n]
    cm = act[:, SSM_D_INNER + SSM_GROUPS * n:]

    dt = jax.nn.softplus(dt_ref[0] + dtb_ref[...])
    a = dt * (-jnp.exp(alog_ref[...]))
    ri = lax.broadcasted_iota(jnp.int32, (L, L), 0)
    ci = lax.broadcasted_iota(jnp.int32, (L, L), 1)
    tri = ri >= ci
    tri_b = jnp.where(tri, 1.0, 0.0).astype(BF16)
    cs3 = _dot(tri_b, jnp.concatenate(_split3(a), axis=1))
    a_cs = cs3[:, :LANES] + cs3[:, LANES:2 * LANES] + cs3[:, 2 * LANES:]
    a_cs_t = a_cs.T
    w_end = dt * jnp.exp(a_cs[L - 1:L, :] - a_cs)
    grow_c = jnp.exp(a_cs)
    pieces = [p for arr in (dt, w_end, grow_c) for p in _split3(arr)]
    ex = _dot(jnp.concatenate(pieces, axis=0), e_ref[...])
    dt_e, wend_e, grow = (ex[3 * i * L:(3 * i + 1) * L] + ex[(3 * i + 1) * L:(3 * i + 2) * L]
                          + ex[(3 * i + 2) * L:(3 * i + 3) * L] for i in range(3))
    xdt_b = (xs * dt_e).astype(BF16)
    xend_b = (xs * wend_e).astype(BF16)
    lane = lax.broadcasted_iota(jnp.int32, (L, LANES), 1)

    y_parts = []
    for g in range(SSM_GROUPS):
        bg = bm[:, g * n:(g + 1) * n]
        cg = cm[:, g * n:(g + 1) * n].astype(BF16)
        cbm = _dot_nt(cg, bg.astype(BF16))
        for pr in range(gw // LANES):
            col = g * gw + pr * LANES
            xpair = xdt_b[:, col:col + LANES]
            outs = []
            for v in range(2):
                h = (col // SSM_HEAD_DIM) + v
                seg = a_cs[:, h:h + 1] - a_cs_t[h:h + 1, :]
                dec = jnp.exp(jnp.where(tri, seg, -jnp.inf))
                outs.append(_dot((cbm * dec).astype(BF16), xpair))
            y_parts.append(jnp.where(lane < SSM_HEAD_DIM, outs[0], outs[1]))
    y = jnp.concatenate(y_parts, axis=1)

    offs = []
    for g in range(SSM_GROUPS):
        sl = slice(g * gw, (g + 1) * gw)
        bg_t = bm[:, g * n:(g + 1) * n].T.astype(BF16)
        cg = cm[:, g * n:(g + 1) * n].astype(BF16)
        st = state_scr[g]
        offs.append(_dot(cg, st.astype(BF16)))
        state_scr[g] = st * grow[L - 1:L, sl] + _dot(bg_t, xend_b[:, sl])
    y = y + jnp.concatenate(offs, axis=1) * grow + xs * dexp_ref[...]

    z = z_ref[0]
    y = y * (z * jax.nn.sigmoid(z))
    nw = nw_ref[...]
    outs = []
    for g in range(SSM_GROUPS):
        sl = slice(g * gw, (g + 1) * gw)
        outs.append(_rms(y[:, sl], nw[:, sl]))
    o_ref[0] = jnp.concatenate(outs, axis=1).astype(BF16)


def _ssd(xbc, z, dtp, conv_w, conv_b, dt_bias, a_log, d, norm_w, b, s):
    L = SSM_CHUNK
    cx = SSM_D_INNER + 2 * SSM_GROUPS * SSM_STATE
    pad = LANES - SSM_HEADS
    expand = jnp.repeat(jnp.eye(SSM_HEADS, dtype=F32), SSM_HEAD_DIM, axis=1)
    expand = jnp.pad(expand, ((0, pad), (0, 0)))
    blk = lambda c: pl.BlockSpec((1, L, c), lambda bi, i: (bi, i, 0))
    return pl.pallas_call(
        _ssd_kernel,
        grid=(b, s // L),
        in_specs=[blk(cx), blk(SSM_D_INNER), blk(LANES)] + [_vmem_full()] * 7,
        out_specs=blk(SSM_D_INNER),
        out_shape=jax.ShapeDtypeStruct((b, s, SSM_D_INNER), BF16),
        scratch_shapes=[pltpu.VMEM((SUBLANES, cx), F32),
                        pltpu.VMEM((SSM_GROUPS, SSM_STATE, SSM_D_INNER // SSM_GROUPS), F32)],
        compiler_params=_params("parallel", "arbitrary"),
        name="ssd_mixer",
    )(xbc.reshape(b, s, cx), z.reshape(b, s, SSM_D_INNER), dtp.reshape(b, s, LANES),
      conv_w, conv_b.reshape(1, -1), jnp.pad(dt_bias, (0, pad)).reshape(1, -1),
      jnp.pad(a_log, (0, pad)).reshape(1, -1), jnp.repeat(d, SSM_HEAD_DIM).reshape(1, -1),
      norm_w.reshape(1, -1), expand.astype(BF16))


def _out0_kernel(x_ref, ya_ref, yb_ref, wa, wb, o_ref):
    o_ref[...] = x_ref[...] + _dot(ya_ref[...], wa[...]) + _dot(yb_ref[...], wb[...])


def _out_proj0(x2, ya, yb, out_w):
    n = x2.shape[0]
    tm = TOKEN_TILE
    ca = N_HEADS_A * HEAD_DIM_A
    w = out_w.astype(BF16)
    row = lambda c: pl.BlockSpec((tm, c), lambda i: (i, 0))
    return pl.pallas_call(
        _out0_kernel,
        grid=(n // tm,),
        in_specs=[row(D_MODEL), row(ca), row(SSM_D_INNER), _vmem_full(), _vmem_full()],
        out_specs=row(D_MODEL),
        out_shape=jax.ShapeDtypeStruct((n, D_MODEL), F32),
        compiler_params=_params("parallel"),
        name="out_proj0",
    )(x2, ya, yb, w[:ca], w[ca:])


def _ffn_kernel(x_ref, g_ref, wg, wu, wd, fn_ref, o_ref, *, final):
    x = x_ref[...]
    hn = _rms(x, g_ref[...]).astype(BF16)
    a = _dot(hn, wg[...])
    u = _dot(hn, wu[...])
    act = (a * jax.nn.sigmoid(a) * u).astype(BF16)
    y = x + _dot(act, wd[...])
    if final:
        y = _rms(y, fn_ref[...])
    o_ref[...] = y


def _ffn(x2, g, w_gate, w_up, w_down, final_norm, final):
    n = x2.shape[0]
    tm = TOKEN_TILE
    row = pl.BlockSpec((tm, D_MODEL), lambda i: (i, 0))
    return pl.pallas_call(
        functools.partial(_ffn_kernel, final=final),
        grid=(n // tm,),
        in_specs=[row] + [_vmem_full()] * 5,
        out_specs=row,
        out_shape=jax.ShapeDtypeStruct((n, D_MODEL), F32),
        compiler_params=_params("parallel"),
        name="ffn_final" if final else "ffn",
    )(x2, g.reshape(1, -1), w_gate.astype(BF16), w_up.astype(BF16), w_down.astype(BF16),
      final_norm.reshape(1, -1))


def _conv_in_kernel(x_ref, g_ref, wb, wc, wv, gb_o, u_o):
    xn = _rms(x_ref[...], g_ref[...]).astype(BF16)
    gb_o[...] = _dot(xn, wb[...])
    u_o[...] = _dot(xn, wc[...]) * _dot(xn, wv[...])


def _conv_out_kernel(x_ref, gb_ref, u_ref, halo_ref, cw_ref, wo, o_ref, *, tiles_per_seq):
    i = pl.program_id(0)
    halo = jnp.where(i % tiles_per_seq == 0, 0.0, halo_ref[...])
    conv = _causal_conv(u_ref[...], cw_ref[...], halo)
    y = (gb_ref[...] * conv).astype(BF16)
    o_ref[...] = x_ref[...] + _dot(y, wo[...])


def _short_conv(x2, norm_w, in_w, conv_w, out_w, s):
    n = x2.shape[0]
    tm = TOKEN_TILE
    w = in_w.astype(BF16)
    row = pl.BlockSpec((tm, D_MODEL), lambda i: (i, 0))
    gb, u = pl.pallas_call(
        _conv_in_kernel,
        grid=(n // tm,),
        in_specs=[row] + [_vmem_full()] * 4,
        out_specs=[row, row],
        out_shape=[jax.ShapeDtypeStruct((n, SC_WIDTH), F32)] * 2,
        compiler_params=_params("parallel"),
        name="conv_in",
    )(x2, norm_w.reshape(1, -1), w[:, :SC_WIDTH], w[:, SC_WIDTH:2 * SC_WIDTH], w[:, 2 * SC_WIDTH:])
    per = tm // SUBLANES
    halo = pl.BlockSpec((SUBLANES, SC_WIDTH), lambda i: (jnp.maximum(i * per - 1, 0), 0))
    return pl.pallas_call(
        functools.partial(_conv_out_kernel, tiles_per_seq=s // tm),
        grid=(n // tm,),
        in_specs=[row, row, row, halo, _vmem_full(), _vmem_full()],
        out_specs=row,
        out_shape=jax.ShapeDtypeStruct((n, D_MODEL), F32),
        compiler_params=_params("parallel"),
        name="conv_out",
    )(x2, gb, u, u, conv_w, out_w.astype(BF16))


def kernel(x, l0_attn_norm, l0_in_w, l0_kv_norm, l0_w_uk, l0_w_uv, l0_conv_w, l0_conv_b, l0_dt_bias,
           l0_A_log, l0_D, l0_ssm_norm, l0_out_w, l0_ffn_norm, l0_w_gate, l0_w_up, l0_w_down,
           l1_conv_norm, l1_in_w, l1_conv_w, l1_out_w, l1_ffn_norm, l1_w_gate, l1_w_up, l1_w_down,
           final_norm):
    b, s, d = x.shape
    assert d == D_MODEL and s % TOKEN_TILE == 0 and s % KEY_CHUNK == 0 and s % SSM_CHUNK == 0
    x2 = x.reshape(b * s, d)
    qt, iqt, ikwt, ckv, ckvt, ik, z, xbc, dtp = _in_proj0(x2, l0_attn_norm, l0_in_w, l0_kv_norm)
    ya = _dsa(qt, iqt, ikwt, ik, ckv, ckvt, l0_w_uk, l0_w_uv, b, s)
    yb = _ssd(xbc, z, dtp, l0_conv_w, l0_conv_b, l0_dt_bias, l0_A_log, l0_D, l0_ssm_norm, b, s)
    x2 = _out_proj0(x2, ya, yb.reshape(b * s, -1), l0_out_w)
    x2 = _ffn(x2, l0_ffn_norm, l0_w_gate, l0_w_up, l0_w_down, final_norm, final=False)
    x2 = _short_conv(x2, l1_conv_norm, l1_in_w, l1_conv_w, l1_out_w, s)
    x2 = _ffn(x2, l1_ffn_norm, l1_w_gate, l1_w_up, l1_w_down, final_norm, final=True)
    return x2.reshape(b, s, d)
```

```python
import functools

import jax
import jax.numpy as jnp
import numpy as np
from jax import lax
from jax.experimental import pallas as pl
from jax.experimental.pallas import tpu as pltpu

D_MODEL = 1024
N_HEADS_A = 8
HEAD_DIM_A = 64
KV_LATENT = 256
IDX_HEADS = 8
IDX_DIM = 64
TOPK_MAX = 256
Q_BLOCK = 128
SSM_D_INNER = 1024
SSM_HEADS = 16
SSM_HEAD_DIM = SSM_D_INNER // SSM_HEADS
SSM_GROUPS = 2
SSM_STATE = 128
SSM_CONV = 4
SSM_CHUNK = 128
SC_WIDTH = D_MODEL
SC_CONV = 3
D_FF = -(-8 * D_MODEL // (3 * 256)) * 256
EPS = 1e-6

LANES = 128
SUBLANES = 8
TOKEN_TILE = 512
CONV_TOKEN_TILE = 512
KEY_CHUNK = 512
KVT_PAD = 16
VMEM_LIMIT = 56 * 1024 * 1024

F32 = jnp.float32
BF16 = jnp.bfloat16
NEG_BIG = -1e30
F32_LOWEST = float(np.finfo(np.float32).min)
KEY_LOWEST = int(np.array(F32_LOWEST, np.float32).view(np.int32)) ^ 0x7FFFFFFF


def _vmem_full():
    return pl.BlockSpec(memory_space=pltpu.VMEM)


def _params(*sem):
    return pltpu.CompilerParams(dimension_semantics=sem, vmem_limit_bytes=VMEM_LIMIT)


def _rms(x, w):
    return x * lax.rsqrt(jnp.mean(x * x, axis=-1, keepdims=True) + EPS) * w


def _dot(a, b):
    return jnp.dot(a, b, preferred_element_type=F32)


def _dot_nt(a, b):
    return lax.dot_general(a, b, (((1,), (1,)), ((), ())), preferred_element_type=F32)


def _in0_kernel(x_ref, g_ref, wqt, wiqt, wikwt, wckv, wikw, wz, wxbc, wdt, kvn_ref,
                qt_o, iqt_o, ikwt_o, ckv_o, ckvt_o, ik_o, z_o, xbc_o, dt_o):
    xn = _rms(x_ref[...], g_ref[...]).astype(BF16)
    qt_o[...] = _dot_nt(wqt[...], xn).astype(BF16)
    iqt_o[...] = _dot_nt(wiqt[...], xn).astype(BF16)
    ikwt_o[...] = _dot_nt(wikwt[...], xn)
    c = _rms(_dot(xn, wckv[...]), kvn_ref[...])
    ckv_o[...] = c.astype(BF16)
    ones_rows = (lax.broadcasted_iota(jnp.int32, (KVT_PAD, c.shape[0]), 0) == 0).astype(F32)
    ckvt_o[0] = jnp.concatenate([c.T, ones_rows], axis=0).astype(BF16)
    ik_o[...] = _dot(xn, wikw[...]).astype(BF16)
    z_o[...] = _dot(xn, wz[...])
    xbc_o[...] = _dot(xn, wxbc[...])
    dt_o[...] = _dot(xn, wdt[...])


def _in_proj0(x2, g, in_w, kv_norm):
    n = x2.shape[0]
    cuts = np.cumsum((512, 256, 512, 64, 8, 1024, 1536, 16))
    w = in_w.astype(BF16)
    wq, wckv, wiq = w[:, :cuts[0]], w[:, cuts[0]:cuts[1]], w[:, cuts[1]:cuts[2]]
    wikw = jnp.pad(w[:, cuts[2]:cuts[4]], ((0, 0), (0, LANES - 72)))
    wz, wxbc = w[:, cuts[4]:cuts[5]], w[:, cuts[5]:cuts[6]]
    wdt = jnp.pad(w[:, cuts[6]:cuts[7]], ((0, 0), (0, LANES - SSM_HEADS)))
    tm = KEY_CHUNK
    row = lambda c: pl.BlockSpec((tm, c), lambda i: (i, 0))
    col = lambda r: pl.BlockSpec((r, tm), lambda i: (0, i))
    hd = N_HEADS_A * HEAD_DIM_A
    out_specs = [col(hd), col(hd), col(LANES), row(KV_LATENT),
                 pl.BlockSpec((1, KV_LATENT + KVT_PAD, tm), lambda i: (i, 0, 0)),
                 row(LANES), row(SSM_D_INNER), row(wxbc.shape[1]), row(LANES)]
    sds = jax.ShapeDtypeStruct
    out_shape = [sds((hd, n), BF16), sds((hd, n), BF16), sds((LANES, n), F32), sds((n, KV_LATENT), BF16),
                 sds((n // tm, KV_LATENT + KVT_PAD, tm), BF16), sds((n, LANES), BF16), sds((n, SSM_D_INNER), F32),
                 sds((n, wxbc.shape[1]), F32), sds((n, LANES), F32)]
    return pl.pallas_call(
        _in0_kernel,
        grid=(n // tm,),
        in_specs=[row(D_MODEL)] + [_vmem_full()] * 10,
        out_specs=out_specs,
        out_shape=out_shape,
        compiler_params=_params("parallel"),
        name="in_proj0",
    )(x2, g.reshape(1, -1), wq.T, wiq.T, wikw.T, wckv, wikw, wz, wxbc, wdt, kv_norm.reshape(1, -1))


def _f2k(x):
    b = pltpu.bitcast(x, jnp.int32)
    return jnp.where(b < 0, b ^ jnp.int32(0x7FFFFFFF), b)


def _k2f(k):
    b = jnp.where(k < 0, k ^ jnp.int32(0x7FFFFFFF), k)
    return pltpu.bitcast(b, F32)


FOLD_ROWS = 8 * SUBLANES
VALUE_PROBES = 10
PROBES_PER_CHECK = 3
ATTN_HEAD_GROUP = 8


def _fold(x, op):
    return op(x.reshape(x.shape[0] // FOLD_ROWS, FOLD_ROWS, LANES), axis=0)


def _any_lane(flag):
    return jnp.max(jnp.where(flag, 1.0, 0.0))


def _dsa_kernel(qt_ref, iqt_ref, wt_ref, iqtn_ref, wtn_ref, ik_ref, ckv_ref, ckvt_ref, wukt_ref, wuvt_ref, o_ref,
                score_scr, qlat_scr, r_scr, acc_scr, m_scr, thr_scr, part_scr, stat_scr,
                *, topk, nblocks):
    i = pl.program_id(1)
    kc = KEY_CHUNK
    qb = Q_BLOCK
    nh = N_HEADS_A
    cur = i % 2
    nxt = 1 - cur
    inext = jnp.minimum(i + 1, nblocks - 1)
    nch = (i * qb) // kc + 1
    nch_next = (inext * qb) // kc + 1
    t0 = i * qb
    key_i = lax.broadcasted_iota(jnp.int32, (kc, LANES), 0)
    rel_i = key_i - lax.broadcasted_iota(jnp.int32, (kc, LANES), 1)

    wscale = (IDX_HEADS ** -0.5) * (IDX_DIM ** -0.5)

    def load_indexer_queries(iqt):
        r_scr[...] = jnp.zeros(r_scr.shape, BF16)
        for h in range(IDX_HEADS):
            r_scr[0:IDX_DIM, h * qb:(h + 1) * qb] = iqt[h * IDX_DIM:(h + 1) * IDX_DIM, :]

    def reset_partials():
        part_scr[0] = jnp.full((FOLD_ROWS, LANES), jnp.inf, F32)
        part_scr[1] = jnp.full((FOLD_ROWS, LANES), -jnp.inf, F32)
        part_scr[2] = jnp.zeros((FOLD_ROWS, LANES), F32)
        part_scr[3] = jnp.zeros((FOLD_ROWS, LANES), F32)

    def score_chunk(j, buf, wts, tq0):
        base = pl.multiple_of(j * kc, kc)
        lt = _dot(ik_ref[0, pl.ds(base, kc), :], r_scr[...])
        acc = None
        for h in range(IDX_HEADS):
            term = jnp.maximum(lt[:, h * qb:(h + 1) * qb], 0.0) * wts[h:h + 1, :]
            acc = term if acc is None else acc + term
        sc = jnp.where(rel_i <= (tq0 - base), acc, -jnp.inf)
        score_scr[buf, pl.ds(base, kc), :] = sc
        part_scr[0] = jnp.minimum(part_scr[0], _fold(acc, jnp.min))
        part_scr[1] = jnp.maximum(part_scr[1], _fold(acc, jnp.max))
        part_scr[2] = part_scr[2] + _fold(jnp.where(sc > 0.0, 1.0, 0.0), jnp.sum)
        part_scr[3] = part_scr[3] + _fold(jnp.where(sc >= 0.0, 1.0, 0.0), jnp.sum)

    def publish_stats():
        stat_scr[0:1, :] = jnp.min(part_scr[0], axis=0, keepdims=True)
        stat_scr[1:2, :] = jnp.max(part_scr[1], axis=0, keepdims=True)
        stat_scr[2:3, :] = jnp.sum(part_scr[2], axis=0, keepdims=True)
        stat_scr[3:4, :] = jnp.sum(part_scr[3], axis=0, keepdims=True)

    @pl.when(i == 0)
    def _():
        load_indexer_queries(iqt_ref[...])
        reset_partials()
        score_chunk(0, cur, wt_ref[...] * wscale, t0)
        publish_stats()

    colmin = stat_scr[0:1, :]
    colmax = stat_scr[1:2, :]
    npos = stat_scr[2:3, :].astype(jnp.int32)
    nnonneg = stat_scr[3:4, :].astype(jnp.int32)

    def count(pred):
        def body(j, acc):
            base = pl.multiple_of(j * kc, kc)
            blk = score_scr[cur, pl.ds(base, kc), :]
            return acc + _fold(jnp.where(pred(blk, base), 1.0, 0.0), jnp.sum)
        acc = lax.fori_loop(0, nch, body, jnp.zeros((FOLD_ROWS, LANES), F32))
        return jnp.sum(acc, axis=0, keepdims=True).astype(jnp.int32)

    t_pos = t0 + lax.broadcasted_iota(jnp.int32, (1, qb), 1)
    nvalid = t_pos + 1
    allsel = nvalid <= topk
    key_zero, key_tiny, key_negzero = 0, 1, -1
    at_zero = jnp.logical_and(npos < topk, nnonneg >= topk)
    above = npos >= topk
    below = nnonneg < topk
    lo0 = jnp.where(at_zero, key_zero, jnp.where(above, key_tiny, _f2k(colmin)))
    cnt0 = jnp.where(at_zero, nnonneg, jnp.where(above, npos, nvalid))
    hi0 = jnp.where(at_zero, key_zero + 1, jnp.where(below, key_negzero, _f2k(colmax) + 1))
    lo0 = jnp.where(allsel, jnp.int32(KEY_LOWEST), lo0)
    hi0 = jnp.where(allsel, jnp.int32(KEY_LOWEST + 1), hi0)
    cnt0 = jnp.where(allsel, jnp.int32(topk), cnt0)
    go0 = _any_lane(lo0 + 1 < hi0)

    def search_step(lo, hi, cnt_lo, by_value):
        active = lo + 1 < hi
        mid_k = (lo & hi) + ((lo ^ hi) >> 1)
        mid_v = _f2k(0.5 * _k2f(lo) + 0.5 * _k2f(hi - 1))
        mid_v = jnp.minimum(jnp.maximum(mid_v, lo + 1), hi - 1)
        mid = jnp.where(by_value, mid_v, mid_k)
        cand = _k2f(mid)
        cnt = count(lambda blk, base: blk >= cand)
        ge = cnt >= topk
        up = jnp.logical_and(active, ge)
        dn = jnp.logical_and(active, jnp.logical_not(ge))
        hit = jnp.logical_and(active, cnt == topk)
        lo = jnp.where(up, mid, lo)
        hi = jnp.where(hit, mid + 1, jnp.where(dn, mid, hi))
        cnt_lo = jnp.where(up, cnt, cnt_lo)
        return lo, hi, cnt_lo

    def bis_cond(c):
        return c[4] > 0.0

    def bis_body(c):
        lo, hi, cnt_lo, it, _ = c
        for u in range(PROBES_PER_CHECK):
            lo, hi, cnt_lo = search_step(lo, hi, cnt_lo, it + u < VALUE_PROBES)
        return lo, hi, cnt_lo, it + PROBES_PER_CHECK, _any_lane(lo + 1 < hi)

    lo, _, cnt_lo, _, _ = lax.while_loop(bis_cond, bis_body, (lo0, hi0, cnt0, jnp.int32(0), go0))
    thr_scr[...] = _k2f(lo)
    tied = cnt_lo > topk

    @pl.when(_any_lane(tied) > 0.0)
    def _():
        thr = thr_scr[...]
        need = topk - count(lambda blk, base: blk > thr)

        def idx_body(_, c):
            plo, phi = c
            pm = (plo + phi) >> 1
            cnt = count(lambda blk, base: jnp.logical_and(blk == thr, (key_i + base) <= pm))
            ok = cnt >= need
            return jnp.where(ok, plo, pm), jnp.where(ok, pm, phi)

        nbits = int(np.ceil(np.log2(score_scr.shape[1]))) + 1
        _, cut = lax.fori_loop(0, nbits, idx_body,
                               (jnp.full((1, qb), -1, jnp.int32), jnp.broadcast_to(nch * kc - 1, (1, qb))))
        cut = jnp.where(tied, cut, jnp.int32(2 ** 30))

        def fix(j, _):
            base = pl.multiple_of(j * kc, kc)
            blk = score_scr[cur, pl.ds(base, kc), :]
            drop = jnp.logical_and(blk == thr, (key_i + base) > cut)
            score_scr[cur, pl.ds(base, kc), :] = jnp.where(drop, -jnp.inf, blk)
            return 0
        lax.fori_loop(0, nch, fix, 0)

    qt = qt_ref[...]
    qscale = (HEAD_DIM_A ** -0.5) * float(np.log2(np.e))
    for p in range(nh // 2):
        r = _dot(wukt_ref[p], qt[p * LANES:(p + 1) * LANES, :])
        for v in range(2):
            h = 2 * p + v
            qlat_scr[:, h * qb:(h + 1) * qb] = (r[v * KV_LATENT:(v + 1) * KV_LATENT, :] * qscale).astype(BF16)
    load_indexer_queries(iqtn_ref[...])
    wts_next = wtn_ref[...] * wscale
    tn0 = inext * qb
    reset_partials()

    m_scr[...] = jnp.full(m_scr.shape, NEG_BIG, F32)
    acc_scr[...] = jnp.zeros(acc_scr.shape, F32)

    def sweep(j, _):
        base = pl.multiple_of(j * kc, kc)
        kv = ckv_ref[0, pl.ds(base, kc), :]
        kvt = ckvt_ref[0, j]
        bias = jnp.where(score_scr[cur, pl.ds(base, kc), :] >= thr_scr[...], 0.0, NEG_BIG)
        hg = ATTN_HEAD_GROUP
        for g in range(nh // hg):
            gcols = slice(g * hg * qb, (g + 1) * hg * qb)
            s = _dot(kv, qlat_scr[:, gcols])
            if g == 0:
                score_chunk(j, nxt, wts_next, tn0)
            ps, alphas = [], []
            for v in range(hg):
                h = g * hg + v
                sh = s[:, v * qb:(v + 1) * qb] + bias
                m_old = m_scr[h:h + 1, :]
                m_new = jnp.maximum(m_old, jnp.max(sh, axis=0, keepdims=True))
                ps.append(jnp.exp2(sh - m_new).astype(BF16))
                alphas.append(jnp.exp2(m_old - m_new))
                m_scr[h:h + 1, :] = m_new
            pv = _dot(kvt, jnp.concatenate(ps, axis=1))
            acc_scr[:, gcols] = acc_scr[:, gcols] * jnp.concatenate(alphas, axis=1) + pv
        return 0

    lax.fori_loop(0, nch, sweep, 0)

    @pl.when(nch_next > nch)
    def _():
        score_chunk(nch, nxt, wts_next, tn0)

    publish_stats()

    inv_l = 1.0 / acc_scr[KV_LATENT:KV_LATENT + 1, :]
    outs = []
    for h in range(nh):
        cols = slice(h * qb, (h + 1) * qb)
        olat = (acc_scr[0:KV_LATENT, cols] * inv_l[:, cols]).astype(BF16)
        outs.append(_dot(wuvt_ref[h], olat))
    o_ref[...] = jnp.concatenate(outs, axis=0).T.astype(BF16)


def _dsa(qt, iqt, ikwt, ik, ckv, ckvt, w_uk, w_uv, b, s):
    kc = KEY_CHUNK
    nb = s // Q_BLOCK
    nchunks = s // kc
    topk = min(TOPK_MAX, s // 4)
    hd = N_HEADS_A * HEAD_DIM_A
    wukt = jnp.swapaxes(w_uk, 1, 2).reshape(N_HEADS_A // 2, 2, KV_LATENT, HEAD_DIM_A)
    eye2 = jnp.eye(2, dtype=F32)
    wukt = jnp.einsum('pvcd,vu->pvcud', wukt, eye2).reshape(N_HEADS_A // 2, 2 * KV_LATENT, 2 * HEAD_DIM_A)
    wuvt = jnp.swapaxes(w_uv, 1, 2)
    this_blk = lambda bi, i: bi * nb + i
    next_blk = lambda bi, i: bi * nb + jnp.minimum(i + 1, nb - 1)
    qcol = lambda r, blk: pl.BlockSpec((r, Q_BLOCK), lambda bi, i: (0, blk(bi, i)))
    wrow = lambda blk: pl.BlockSpec((IDX_HEADS, Q_BLOCK), lambda bi, i: (IDX_DIM // IDX_HEADS, blk(bi, i)))
    kern = functools.partial(_dsa_kernel, topk=topk, nblocks=nb)
    return pl.pallas_call(
        kern,
        grid=(b, nb),
        in_specs=[qcol(hd, this_blk), qcol(IDX_HEADS * IDX_DIM, this_blk), wrow(this_blk),
                  qcol(IDX_HEADS * IDX_DIM, next_blk), wrow(next_blk),
                  pl.BlockSpec((1, s, LANES), lambda bi, i: (bi, 0, 0)),
                  pl.BlockSpec((1, s, KV_LATENT), lambda bi, i: (bi, 0, 0)),
                  pl.BlockSpec((1, nchunks, KV_LATENT + KVT_PAD, kc), lambda bi, i: (bi, 0, 0, 0)),
                  _vmem_full(), _vmem_full()],
        out_specs=pl.BlockSpec((Q_BLOCK, hd), lambda bi, i: (bi * nb + i, 0)),
        out_shape=jax.ShapeDtypeStruct((b * s, hd), BF16),
        scratch_shapes=[
            pltpu.VMEM((2, s, Q_BLOCK), F32),
            pltpu.VMEM((KV_LATENT, N_HEADS_A * Q_BLOCK), BF16),
            pltpu.VMEM((LANES, IDX_HEADS * Q_BLOCK), BF16),
            pltpu.VMEM((KV_LATENT + KVT_PAD, N_HEADS_A * Q_BLOCK), F32),
            pltpu.VMEM((N_HEADS_A, Q_BLOCK), F32),
            pltpu.VMEM((1, Q_BLOCK), F32),
            pltpu.VMEM((4, FOLD_ROWS, Q_BLOCK), F32),
            pltpu.VMEM((SUBLANES, Q_BLOCK), F32),
        ],
        compiler_params=_params("parallel", "arbitrary"),
        name="dsa_attention",
    )(qt, iqt, ikwt, iqt, ikwt, ik.reshape(b, s, LANES), ckv.reshape(b, s, KV_LATENT),
      ckvt.reshape(b, nchunks, KV_LATENT + KVT_PAD, kc), wukt.astype(BF16), wuvt.astype(BF16))


def _causal_conv(x, w, halo):
    taps = w.shape[0]
    top_row = lax.broadcasted_iota(jnp.int32, halo.shape, 0)
    y = x * w[taps - 1:taps, :]
    for k in range(1, taps):
        xk = pltpu.roll(x, k, 0)
        top = jnp.where(top_row < k, pltpu.roll(halo, k, 0), xk[:SUBLANES, :])
        xk = jnp.concatenate([top, xk[SUBLANES:, :]], axis=0)
        y = y + xk * w[taps - 1 - k:taps - k, :]
    return y


def _split3(x):
    p1 = x.astype(BF16)
    r = x - p1.astype(F32)
    p2 = r.astype(BF16)
    return p1, p2, (r - p2.astype(F32)).astype(BF16)


def _ssd_kernel(xbc_ref, z_ref, dt_ref, cw_ref, cb_ref, dtb_ref, alog_ref, dexp_ref, nw_ref, e_ref,
                o_ref, halo_scr, state_scr):
    c = pl.program_id(1)
    L = SSM_CHUNK
    n = SSM_STATE
    gw = SSM_D_INNER // SSM_GROUPS

    @pl.when(c == 0)
    def _():
        halo_scr[...] = jnp.zeros(halo_scr.shape, F32)
        state_scr[...] = jnp.zeros(state_scr.shape, F32)

    xbc = xbc_ref[0]
    conv = _causal_conv(xbc, cw_ref[...], halo_scr[...]) + cb_ref[...]
    halo_scr[...] = xbc[L - SUBLANES:, :]
    act = conv * jax.nn.sigmoid(conv)
    xs = act[:, :SSM_D_INNER]
    bm = act[:, SSM_D_INNER:SSM_D_INNER + SSM_GROUPS * n]
    cm = act[:, SSM_D_INNER + SSM_GROUPS * n:]

    dt = jax.nn.softplus(dt_ref[0] + dtb_ref[...])
    a = dt * (-jnp.exp(alog_ref[...]))
    ri = lax.broadcasted_iota(jnp.int32, (L, L), 0)
    ci = lax.broadcasted_iota(jnp.int32, (L, L), 1)
    tri = ri >= ci
    tri_b = jnp.where(tri, 1.0, 0.0).astype(BF16)
    cs3 = _dot(tri_b, jnp.concatenate(_split3(a), axis=1))
    a_cs = cs3[:, :LANES] + cs3[:, LANES:2 * LANES] + cs3[:, 2 * LANES:]
    a_cs_t = a_cs.T
    w_end = dt * jnp.exp(a_cs[L - 1:L, :] - a_cs)
    grow_c = jnp.exp(a_cs)
    pieces = [p for arr in (dt, w_end, grow_c) for p in _split3(arr)]
    ex = _dot(jnp.concatenate(pieces, axis=0), e_ref[...])
    dt_e, wend_e, grow = (ex[3 * i * L:(3 * i + 1) * L] + ex[(3 * i + 1) * L:(3 * i + 2) * L]
                          + ex[(3 * i + 2) * L:(3 * i + 3) * L] for i in range(3))
    xdt_b = (xs * dt_e).astype(BF16)
    xend_b = (xs * wend_e).astype(BF16)
    lane = lax.broadcasted_iota(jnp.int32, (L, LANES), 1)

    y_parts = []
    for g in range(SSM_GROUPS):
        bg = bm[:, g * n:(g + 1) * n]
        cg = cm[:, g * n:(g + 1) * n].astype(BF16)
        cbm = _dot_nt(cg, bg.astype(BF16))
        for pr in range(gw // LANES):
            col = g * gw + pr * LANES
            xpair = xdt_b[:, col:col + LANES]
            outs = []
            for v in range(2):
                h = (col // SSM_HEAD_DIM) + v
                seg = a_cs[:, h:h + 1] - a_cs_t[h:h + 1, :]
                dec = jnp.exp(jnp.where(tri, seg, -jnp.inf))
                outs.append(_dot((cbm * dec).astype(BF16), xpair))
            y_parts.append(jnp.where(lane < SSM_HEAD_DIM, outs[0], outs[1]))
    y = jnp.concatenate(y_parts, axis=1)

    offs = []
    for g in range(SSM_GROUPS):
        sl = slice(g * gw, (g + 1) * gw)
        bg_t = bm[:, g * n:(g + 1) * n].T.astype(BF16)
        cg = cm[:, g * n:(g + 1) * n].astype(BF16)
        st = state_scr[g]
        offs.append(_dot(cg, st.astype(BF16)))
        state_scr[g] = st * grow[L - 1:L, sl] + _dot(bg_t, xend_b[:, sl])
    y = y + jnp.concatenate(offs, axis=1) * grow + xs * dexp_ref[...]

    z = z_ref[0]
    y = y * (z * jax.nn.sigmoid(z))
    nw = nw_ref[...]
    outs = []
    for g in range(SSM_GROUPS):
        sl = slice(g * gw, (g + 1) * gw)
        outs.append(_rms(y[:, sl], nw[:, sl]))
    o_ref[0] = jnp.concatenate(outs, axis=1).astype(BF16)


def _ssd(xbc, z, dtp, conv_w, conv_b, dt_bias, a_log, d, norm_w, b, s):
    L = SSM_CHUNK
    cx = SSM_D_INNER + 2 * SSM_GROUPS * SSM_STATE
    pad = LANES - SSM_HEADS
    expand = jnp.repeat(jnp.eye(SSM_HEADS, dtype=F32), SSM_HEAD_DIM, axis=1)
    expand = jnp.pad(expand, ((0, pad), (0, 0)))
    blk = lambda c: pl.BlockSpec((1, L, c), lambda bi, i: (bi, i, 0))
    return pl.pallas_call(
        _ssd_kernel,
        grid=(b, s // L),
        in_specs=[blk(cx), blk(SSM_D_INNER), blk(LANES)] + [_vmem_full()] * 7,
        out_specs=blk(SSM_D_INNER),
        out_shape=jax.ShapeDtypeStruct((b, s, SSM_D_INNER), BF16),
        scratch_shapes=[pltpu.VMEM((SUBLANES, cx), F32),
                        pltpu.VMEM((SSM_GROUPS, SSM_STATE, SSM_D_INNER // SSM_GROUPS), F32)],
        compiler_params=_params("parallel", "arbitrary"),
        name="ssd_mixer",
    )(xbc.reshape(b, s, cx), z.reshape(b, s, SSM_D_INNER), dtp.reshape(b, s, LANES),
      conv_w, conv_b.reshape(1, -1), jnp.pad(dt_bias, (0, pad)).reshape(1, -1),
      jnp.pad(a_log, (0, pad)).reshape(1, -1), jnp.repeat(d, SSM_HEAD_DIM).reshape(1, -1),
      norm_w.reshape(1, -1), expand.astype(BF16))


def _swiglu(x, g, wg, wu, wd):
    hn = _rms(x, g).astype(BF16)
    a = _dot(hn, wg[...])
    u = _dot(hn, wu[...])
    act = (a * jax.nn.sigmoid(a) * u).astype(BF16)
    return x + _dot(act, wd[...])


def _mix_ffn_kernel(x_ref, ya_ref, yb_ref, wa, wb, g_ref, wg, wu, wd, o_ref):
    x = x_ref[...] + _dot(ya_ref[...], wa[...]) + _dot(yb_ref[...], wb[...])
    o_ref[...] = _swiglu(x, g_ref[...], wg, wu, wd)


def _mix_ffn(x2, ya, yb, out_w, g, w_gate, w_up, w_down):
    n = x2.shape[0]
    tm = TOKEN_TILE
    ca = N_HEADS_A * HEAD_DIM_A
    w = out_w.astype(BF16)
    row = lambda c: pl.BlockSpec((tm, c), lambda i: (i, 0))
    return pl.pallas_call(
        _mix_ffn_kernel,
        grid=(n // tm,),
        in_specs=[row(D_MODEL), row(ca), row(SSM_D_INNER)] + [_vmem_full()] * 6,
        out_specs=row(D_MODEL),
        out_shape=jax.ShapeDtypeStruct((n, D_MODEL), F32),
        compiler_params=_params("parallel"),
        name="mix_ffn",
    )(x2, ya, yb, w[:ca], w[ca:], g.reshape(1, -1), w_gate.astype(BF16), w_up.astype(BF16), w_down.astype(BF16))


def _conv_ffn_kernel(x_ref, xh_ref, gc_ref, wb, wc, wv, cw_ref, wo, g_ref, wg, wu, wd, fn_ref, o_ref,
                     *, tiles_per_seq):
    i = pl.program_id(0)
    x = x_ref[...]
    xn = _rms(x, gc_ref[...]).astype(BF16)
    u = _dot(xn, wc[...]) * _dot(xn, wv[...])
    xhn = _rms(xh_ref[...], gc_ref[...]).astype(BF16)
    halo = jnp.where(i % tiles_per_seq == 0, 0.0, _dot(xhn, wc[...]) * _dot(xhn, wv[...]))
    y = (_dot(xn, wb[...]) * _causal_conv(u, cw_ref[...], halo)).astype(BF16)
    x = x + _dot(y, wo[...])
    o_ref[...] = _rms(_swiglu(x, g_ref[...], wg, wu, wd), fn_ref[...])


def _conv_ffn(x2, norm_w, in_w, conv_w, out_w, g, w_gate, w_up, w_down, final_norm, s):
    n = x2.shape[0]
    tm = CONV_TOKEN_TILE
    w = in_w.astype(BF16)
    row = pl.BlockSpec((tm, D_MODEL), lambda i: (i, 0))
    per = tm // SUBLANES
    halo = pl.BlockSpec((SUBLANES, D_MODEL), lambda i: (jnp.maximum(i * per - 1, 0), 0))
    return pl.pallas_call(
        functools.partial(_conv_ffn_kernel, tiles_per_seq=s // tm),
        grid=(n // tm,),
        in_specs=[row, halo] + [_vmem_full()] * 11,
        out_specs=row,
        out_shape=jax.ShapeDtypeStruct((n, D_MODEL), F32),
        compiler_params=_params("parallel"),
        name="conv_ffn",
    )(x2, x2, norm_w.reshape(1, -1), w[:, :SC_WIDTH], w[:, SC_WIDTH:2 * SC_WIDTH], w[:, 2 * SC_WIDTH:],
      conv_w, out_w.astype(BF16), g.reshape(1, -1), w_gate.astype(BF16), w_up.astype(BF16), w_down.astype(BF16),
      final_norm.reshape(1, -1))


def kernel(x, l0_attn_norm, l0_in_w, l0_kv_norm, l0_w_uk, l0_w_uv, l0_conv_w, l0_conv_b, l0_dt_bias,
           l0_A_log, l0_D, l0_ssm_norm, l0_out_w, l0_ffn_norm, l0_w_gate, l0_w_up, l0_w_down,
           l1_conv_norm, l1_in_w, l1_conv_w, l1_out_w, l1_ffn_norm, l1_w_gate, l1_w_up, l1_w_down,
           final_norm):
    b, s, d = x.shape
    assert d == D_MODEL and s % TOKEN_TILE == 0 and s % CONV_TOKEN_TILE == 0 and s % KEY_CHUNK == 0 and s % SSM_CHUNK == 0
    x2 = x.reshape(b * s, d)
    qt, iqt, ikwt, ckv, ckvt, ik, z, xbc, dtp = _in_proj0(x2, l0_attn_norm, l0_in_w, l0_kv_norm)
    ya = _dsa(qt, iqt, ikwt, ik, ckv, ckvt, l0_w_uk, l0_w_uv, b, s)
    yb = _ssd(xbc, z, dtp, l0_conv_w, l0_conv_b, l0_dt_bias, l0_A_log, l0_D, l0_ssm_norm, b, s)
    x2 = _mix_ffn(x2, ya, yb.reshape(b * s, -1), l0_out_w, l0_ffn_norm, l0_w_gate, l0_w_up, l0_w_down)
    x2 = _conv_ffn(x2, l1_conv_norm, l1_in_w, l1_conv_w, l1_out_w, l1_ffn_norm, l1_w_gate, l1_w_up, l1_w_down,
                   final_norm, s)
    return x2.reshape(b, s, d)
```

```python
import functools

import jax
import jax.numpy as jnp
import numpy as np
from jax import lax
from jax.experimental import pallas as pl
from jax.experimental.pallas import tpu as pltpu

D_MODEL = 1024
N_HEADS_A = 8
HEAD_DIM_A = 64
KV_LATENT = 256
IDX_HEADS = 8
IDX_DIM = 64
TOPK_MAX = 256
Q_BLOCK = 128
SSM_D_INNER = 1024
SSM_HEADS = 16
SSM_HEAD_DIM = SSM_D_INNER // SSM_HEADS
SSM_GROUPS = 2
SSM_STATE = 128
SSM_CONV = 4
SSM_CHUNK = 128
SC_WIDTH = D_MODEL
SC_CONV = 3
D_FF = -(-8 * D_MODEL // (3 * 256)) * 256
EPS = 1e-6

LANES = 128
SUBLANES = 8
TOKEN_TILE = 512
CONV_TOKEN_TILE = 512
KEY_CHUNK = 512
KVT_PAD = 16
VMEM_LIMIT = 56 * 1024 * 1024

F32 = jnp.float32
BF16 = jnp.bfloat16
NEG_BIG = -1e30
F32_LOWEST = float(np.finfo(np.float32).min)
KEY_LOWEST = int(np.array(F32_LOWEST, np.float32).view(np.int32)) ^ 0x7FFFFFFF


def _vmem_full():
    return pl.BlockSpec(memory_space=pltpu.VMEM)


def _params(*sem):
    return pltpu.CompilerParams(dimension_semantics=sem, vmem_limit_bytes=VMEM_LIMIT)


def _rms(x, w):
    return x * lax.rsqrt(jnp.mean(x * x, axis=-1, keepdims=True) + EPS) * w


def _dot(a, b):
    return jnp.dot(a, b, preferred_element_type=F32)


def _dot_nt(a, b):
    return lax.dot_general(a, b, (((1,), (1,)), ((), ())), preferred_element_type=F32)


def _in0_kernel(x_ref, g_ref, wqt, wiqt, wikwt, wckv, wikw, wz, wxbc, wdt, kvn_ref,
                qt_o, iqt_o, ikwt_o, ckv_o, ckvt_o, ik_o, z_o, xbc_o, dt_o):
    xn = _rms(x_ref[...], g_ref[...]).astype(BF16)
    qt_o[...] = _dot_nt(wqt[...], xn).astype(BF16)
    iqt_o[...] = _dot_nt(wiqt[...], xn).astype(BF16)
    ikwt_o[...] = _dot_nt(wikwt[...], xn)
    c = _rms(_dot(xn, wckv[...]), kvn_ref[...])
    ckv_o[...] = c.astype(BF16)
    ones_rows = (lax.broadcasted_iota(jnp.int32, (KVT_PAD, c.shape[0]), 0) == 0).astype(F32)
    ckvt_o[0] = jnp.concatenate([c.T, ones_rows], axis=0).astype(BF16)
    ik_o[...] = _dot(xn, wikw[...]).astype(BF16)
    z_o[...] = _dot(xn, wz[...])
    xbc_o[...] = _dot(xn, wxbc[...])
    dt_o[...] = _dot(xn, wdt[...])


def _in_proj0(x2, g, in_w, kv_norm):
    n = x2.shape[0]
    cuts = np.cumsum((512, 256, 512, 64, 8, 1024, 1536, 16))
    w = in_w.astype(BF16)
    wq, wckv, wiq = w[:, :cuts[0]], w[:, cuts[0]:cuts[1]], w[:, cuts[1]:cuts[2]]
    wikw = jnp.pad(w[:, cuts[2]:cuts[4]], ((0, 0), (0, LANES - 72)))
    wz, wxbc = w[:, cuts[4]:cuts[5]], w[:, cuts[5]:cuts[6]]
    wdt = jnp.pad(w[:, cuts[6]:cuts[7]], ((0, 0), (0, LANES - SSM_HEADS)))
    tm = KEY_CHUNK
    row = lambda c: pl.BlockSpec((tm, c), lambda i: (i, 0))
    col = lambda r: pl.BlockSpec((r, tm), lambda i: (0, i))
    hd = N_HEADS_A * HEAD_DIM_A
    out_specs = [col(hd), col(hd), col(LANES), row(KV_LATENT),
                 pl.BlockSpec((1, KV_LATENT + KVT_PAD, tm), lambda i: (i, 0, 0)),
                 row(LANES), row(SSM_D_INNER), row(wxbc.shape[1]), row(LANES)]
    sds = jax.ShapeDtypeStruct
    out_shape = [sds((hd, n), BF16), sds((hd, n), BF16), sds((LANES, n), F32), sds((n, KV_LATENT), BF16),
                 sds((n // tm, KV_LATENT + KVT_PAD, tm), BF16), sds((n, LANES), BF16), sds((n, SSM_D_INNER), F32),
                 sds((n, wxbc.shape[1]), F32), sds((n, LANES), F32)]
    return pl.pallas_call(
        _in0_kernel,
        grid=(n // tm,),
        in_specs=[row(D_MODEL)] + [_vmem_full()] * 10,
        out_specs=out_specs,
        out_shape=out_shape,
        compiler_params=_params("parallel"),
        name="in_proj0",
    )(x2, g.reshape(1, -1), wq.T, wiq.T, wikw.T, wckv, wikw, wz, wxbc, wdt, kv_norm.reshape(1, -1))


def _f2k(x):
    b = pltpu.bitcast(x, jnp.int32)
    return jnp.where(b < 0, b ^ jnp.int32(0x7FFFFFFF), b)


def _k2f(k):
    b = jnp.where(k < 0, k ^ jnp.int32(0x7FFFFFFF), k)
    return pltpu.bitcast(b, F32)


FOLD_ROWS = 8 * SUBLANES
VALUE_PROBES = 10
PROBES_PER_CHECK = 3
UNCHECKED_ROUNDS = 5
ATTN_HEAD_GROUP = 8


def _fold(x, op):
    return op(x.reshape(x.shape[0] // FOLD_ROWS, FOLD_ROWS, LANES), axis=0)


def _any_lane(flag):
    return jnp.max(jnp.where(flag, 1.0, 0.0))


def _dsa_kernel(qt_ref, iqt_ref, wt_ref, iqtn_ref, wtn_ref, ik_ref, ckv_ref, ckvt_ref, wukt_ref, wuvt_ref, o_ref,
                score_scr, qlat_scr, r_scr, acc_scr, m_scr, thr_scr, part_scr, stat_scr,
                *, topk, nblocks):
    i = pl.program_id(1)
    kc = KEY_CHUNK
    qb = Q_BLOCK
    nh = N_HEADS_A
    cur = i % 2
    nxt = 1 - cur
    inext = jnp.minimum(i + 1, nblocks - 1)
    nch = (i * qb) // kc + 1
    nch_next = (inext * qb) // kc + 1
    t0 = i * qb
    key_i = lax.broadcasted_iota(jnp.int32, (kc, LANES), 0)
    rel_i = key_i - lax.broadcasted_iota(jnp.int32, (kc, LANES), 1)

    wscale = (IDX_HEADS ** -0.5) * (IDX_DIM ** -0.5)

    def load_indexer_queries(iqt):
        r_scr[...] = jnp.zeros(r_scr.shape, BF16)
        for h in range(IDX_HEADS):
            r_scr[0:IDX_DIM, h * qb:(h + 1) * qb] = iqt[h * IDX_DIM:(h + 1) * IDX_DIM, :]

    def reset_partials():
        part_scr[0] = jnp.full((FOLD_ROWS, LANES), jnp.inf, F32)
        part_scr[1] = jnp.full((FOLD_ROWS, LANES), -jnp.inf, F32)
        part_scr[2] = jnp.zeros((FOLD_ROWS, LANES), F32)
        part_scr[3] = jnp.zeros((FOLD_ROWS, LANES), F32)

    def score_chunk(j, buf, wts, tq0):
        base = pl.multiple_of(j * kc, kc)
        lt = _dot(ik_ref[0, pl.ds(base, kc), :], r_scr[...])
        acc = None
        for h in range(IDX_HEADS):
            term = jnp.maximum(lt[:, h * qb:(h + 1) * qb], 0.0) * wts[h:h + 1, :]
            acc = term if acc is None else acc + term
        sc = jnp.where(rel_i <= (tq0 - base), acc, -jnp.inf)
        score_scr[buf, pl.ds(base, kc), :] = sc
        part_scr[0] = jnp.minimum(part_scr[0], _fold(acc, jnp.min))
        part_scr[1] = jnp.maximum(part_scr[1], _fold(acc, jnp.max))
        part_scr[2] = part_scr[2] + _fold(jnp.where(sc > 0.0, 1.0, 0.0), jnp.sum)
        part_scr[3] = part_scr[3] + _fold(jnp.where(sc >= 0.0, 1.0, 0.0), jnp.sum)

    def publish_stats():
        stat_scr[0:1, :] = jnp.min(part_scr[0], axis=0, keepdims=True)
        stat_scr[1:2, :] = jnp.max(part_scr[1], axis=0, keepdims=True)
        stat_scr[2:3, :] = jnp.sum(part_scr[2], axis=0, keepdims=True)
        stat_scr[3:4, :] = jnp.sum(part_scr[3], axis=0, keepdims=True)

    @pl.when(i == 0)
    def _():
        load_indexer_queries(iqt_ref[...])
        reset_partials()
        score_chunk(0, cur, wt_ref[...] * wscale, t0)
        publish_stats()

    colmin = stat_scr[0:1, :]
    colmax = stat_scr[1:2, :]
    npos = stat_scr[2:3, :].astype(jnp.int32)
    nnonneg = stat_scr[3:4, :].astype(jnp.int32)

    def count(pred):
        def body(j, acc):
            base = pl.multiple_of(j * kc, kc)
            blk = score_scr[cur, pl.ds(base, kc), :]
            return acc + _fold(jnp.where(pred(blk, base), 1.0, 0.0), jnp.sum)
        acc = lax.fori_loop(0, nch, body, jnp.zeros((FOLD_ROWS, LANES), F32))
        return jnp.sum(acc, axis=0, keepdims=True).astype(jnp.int32)

    t_pos = t0 + lax.broadcasted_iota(jnp.int32, (1, qb), 1)
    nvalid = t_pos + 1
    allsel = nvalid <= topk
    key_zero, key_tiny, key_negzero = 0, 1, -1
    at_zero = jnp.logical_and(npos < topk, nnonneg >= topk)
    above = npos >= topk
    below = nnonneg < topk
    lo0 = jnp.where(at_zero, key_zero, jnp.where(above, key_tiny, _f2k(colmin)))
    cnt0 = jnp.where(at_zero, nnonneg, jnp.where(above, npos, nvalid))
    hi0 = jnp.where(at_zero, key_zero + 1, jnp.where(below, key_negzero, _f2k(colmax) + 1))
    lo0 = jnp.where(allsel, jnp.int32(KEY_LOWEST), lo0)
    hi0 = jnp.where(allsel, jnp.int32(KEY_LOWEST + 1), hi0)
    cnt0 = jnp.where(allsel, jnp.int32(topk), cnt0)
    go0 = _any_lane(lo0 + 1 < hi0)
    cnt_hi0 = jnp.where(below, nnonneg, 0)

    def search_step(lo, hi, cnt_lo, cnt_hi, by_value):
        active = lo + 1 < hi
        mid_k = (lo & hi) + ((lo ^ hi) >> 1)
        mid_v = _f2k(0.5 * _k2f(lo) + 0.5 * _k2f(hi - 1))
        mid_v = jnp.minimum(jnp.maximum(mid_v, lo + 1), hi - 1)
        mid = jnp.where(by_value, mid_v, mid_k)
        cand = _k2f(mid)
        cnt = count(lambda blk, base: blk >= cand)
        ge = cnt >= topk
        up = jnp.logical_and(active, ge)
        dn = jnp.logical_and(active, jnp.logical_not(ge))
        hit = jnp.logical_and(active, cnt == topk)
        lo = jnp.where(up, mid, lo)
        hi = jnp.where(hit, mid + 1, jnp.where(dn, mid, hi))
        cnt_lo = jnp.where(up, cnt, cnt_lo)
        cnt_hi = jnp.where(dn, cnt, cnt_hi)
        return lo, hi, cnt_lo, cnt_hi

    def resolve_small(lo, hi, cnt_lo, cnt_hi):
        active = lo + 1 < hi
        v_below, v_hi = _k2f(lo - 1), _k2f(hi)

        def body(j, c):
            mn, mx = c
            base = pl.multiple_of(j * kc, kc)
            blk = score_scr[cur, pl.ds(base, kc), :]
            mn = jnp.minimum(mn, _fold(jnp.where(blk > v_below, blk, jnp.inf), jnp.min))
            mx = jnp.maximum(mx, _fold(jnp.where(blk < v_hi, blk, -jnp.inf), jnp.max))
            return mn, mx
        mn, mx = lax.fori_loop(0, nch, body, (jnp.full((FOLD_ROWS, LANES), jnp.inf, F32),
                                              jnp.full((FOLD_ROWS, LANES), -jnp.inf, F32)))
        mn = jnp.min(mn, axis=0, keepdims=True)
        mx = jnp.max(mx, axis=0, keepdims=True)
        inside = cnt_lo - cnt_hi
        small = jnp.logical_and(active, inside <= 2)
        want_top = (topk - cnt_hi) == 1
        twins = jnp.logical_and(mx == mn, inside == 2)
        key_t = _f2k(jnp.where(want_top, mx, mn))
        cnt_t = jnp.where(want_top, cnt_hi + 1 + jnp.where(twins, 1, 0), cnt_lo)
        return (jnp.where(small, key_t, lo), jnp.where(small, key_t + 1, hi), jnp.where(small, cnt_t, cnt_lo), cnt_hi)

    def bis_cond(c):
        return c[5] > 0.0

    def probe_round(r, c):
        for u in range(PROBES_PER_CHECK):
            c = search_step(*c, r * PROBES_PER_CHECK + u < VALUE_PROBES)
        return c

    def bis_body(c):
        r = c[4]
        st = resolve_small(*probe_round(r, c[:4]))
        return (*st, r + 1, _any_lane(st[0] + 1 < st[1]))

    first = jnp.where(go0 > 0.0, UNCHECKED_ROUNDS, 0)
    st = resolve_small(*lax.fori_loop(0, first, probe_round, (lo0, hi0, cnt0, cnt_hi0)))
    lo, _, cnt_lo, _, _, _ = lax.while_loop(bis_cond, bis_body, (*st, first, _any_lane(st[0] + 1 < st[1])))
    thr_scr[...] = _k2f(lo)
    tied = cnt_lo > topk

    @pl.when(_any_lane(tied) > 0.0)
    def _():
        thr = thr_scr[...]
        need = topk - count(lambda blk, base: blk > thr)

        def idx_body(_, c):
            plo, phi = c
            pm = (plo + phi) >> 1
            cnt = count(lambda blk, base: jnp.logical_and(blk == thr, (key_i + base) <= pm))
            ok = cnt >= need
            return jnp.where(ok, plo, pm), jnp.where(ok, pm, phi)

        nbits = int(np.ceil(np.log2(score_scr.shape[1]))) + 1
        _, cut = lax.fori_loop(0, nbits, idx_body,
                               (jnp.full((1, qb), -1, jnp.int32), jnp.broadcast_to(nch * kc - 1, (1, qb))))
        cut = jnp.where(tied, cut, jnp.int32(2 ** 30))

        def fix(j, _):
            base = pl.multiple_of(j * kc, kc)
            blk = score_scr[cur, pl.ds(base, kc), :]
            drop = jnp.logical_and(blk == thr, (key_i + base) > cut)
            score_scr[cur, pl.ds(base, kc), :] = jnp.where(drop, -jnp.inf, blk)
            return 0
        lax.fori_loop(0, nch, fix, 0)

    qt = qt_ref[...]
    qscale = (HEAD_DIM_A ** -0.5) * float(np.log2(np.e))
    for p in range(nh // 2):
        r = _dot(wukt_ref[p], qt[p * LANES:(p + 1) * LANES, :])
        for v in range(2):
            h = 2 * p + v
            qlat_scr[:, h * qb:(h + 1) * qb] = (r[v * KV_LATENT:(v + 1) * KV_LATENT, :] * qscale).astype(BF16)
    load_indexer_queries(iqtn_ref[...])
    wts_next = wtn_ref[...] * wscale
    tn0 = inext * qb
    reset_partials()

    m_scr[...] = jnp.full(m_scr.shape, NEG_BIG, F32)
    acc_scr[...] = jnp.zeros(acc_scr.shape, F32)

    def sweep(j, _):
        base = pl.multiple_of(j * kc, kc)
        kv = ckv_ref[0, pl.ds(base, kc), :]
        kvt = ckvt_ref[0, j]
        bias = jnp.where(score_scr[cur, pl.ds(base, kc), :] >= thr_scr[...], 0.0, NEG_BIG)
        hg = ATTN_HEAD_GROUP
        for g in range(nh // hg):
            gcols = slice(g * hg * qb, (g + 1) * hg * qb)
            s = _dot(kv, qlat_scr[:, gcols])
            if g == 0:
                score_chunk(j, nxt, wts_next, tn0)
            ps, alphas = [], []
            for v in range(hg):
                h = g * hg + v
                sh = s[:, v * qb:(v + 1) * qb] + bias
                m_old = m_scr[h:h + 1, :]
                m_new = jnp.maximum(m_old, jnp.max(sh, axis=0, keepdims=True))
                ps.append(jnp.exp2(sh - m_new).astype(BF16))
                alphas.append(jnp.exp2(m_old - m_new))
                m_scr[h:h + 1, :] = m_new
            pv = _dot(kvt, jnp.concatenate(ps, axis=1))
            acc_scr[:, gcols] = acc_scr[:, gcols] * jnp.concatenate(alphas, axis=1) + pv
        return 0

    lax.fori_loop(0, nch, sweep, 0)

    @pl.when(nch_next > nch)
    def _():
        score_chunk(nch, nxt, wts_next, tn0)

    publish_stats()

    inv_l = 1.0 / acc_scr[KV_LATENT:KV_LATENT + 1, :]
    outs = []
    for h in range(nh):
        cols = slice(h * qb, (h + 1) * qb)
        olat = (acc_scr[0:KV_LATENT, cols] * inv_l[:, cols]).astype(BF16)
        outs.append(_dot(wuvt_ref[h], olat))
    o_ref[...] = jnp.concatenate(outs, axis=0).T.astype(BF16)


def _dsa(qt, iqt, ikwt, ik, ckv, ckvt, w_uk, w_uv, b, s):
    kc = KEY_CHUNK
    nb = s // Q_BLOCK
    nchunks = s // kc
    topk = min(TOPK_MAX, s // 4)
    hd = N_HEADS_A * HEAD_DIM_A
    wukt = jnp.swapaxes(w_uk, 1, 2).reshape(N_HEADS_A // 2, 2, KV_LATENT, HEAD_DIM_A)
    eye2 = jnp.eye(2, dtype=F32)
    wukt = jnp.einsum('pvcd,vu->pvcud', wukt, eye2).reshape(N_HEADS_A // 2, 2 * KV_LATENT, 2 * HEAD_DIM_A)
    wuvt = jnp.swapaxes(w_uv, 1, 2)
    this_blk = lambda bi, i: bi * nb + i
    next_blk = lambda bi, i: bi * nb + jnp.minimum(i + 1, nb - 1)
    qcol = lambda r, blk: pl.BlockSpec((r, Q_BLOCK), lambda bi, i: (0, blk(bi, i)))
    wrow = lambda blk: pl.BlockSpec((IDX_HEADS, Q_BLOCK), lambda bi, i: (IDX_DIM // IDX_HEADS, blk(bi, i)))
    kern = functools.partial(_dsa_kernel, topk=topk, nblocks=nb)
    return pl.pallas_call(
        kern,
        grid=(b, nb),
        in_specs=[qcol(hd, this_blk), qcol(IDX_HEADS * IDX_DIM, this_blk), wrow(this_blk),
                  qcol(IDX_HEADS * IDX_DIM, next_blk), wrow(next_blk),
                  pl.BlockSpec((1, s, LANES), lambda bi, i: (bi, 0, 0)),
                  pl.BlockSpec((1, s, KV_LATENT), lambda bi, i: (bi, 0, 0)),
                  pl.BlockSpec((1, nchunks, KV_LATENT + KVT_PAD, kc), lambda bi, i: (bi, 0, 0, 0)),
                  _vmem_full(), _vmem_full()],
        out_specs=pl.BlockSpec((Q_BLOCK, hd), lambda bi, i: (bi * nb + i, 0)),
        out_shape=jax.ShapeDtypeStruct((b * s, hd), BF16),
        scratch_shapes=[
            pltpu.VMEM((2, s, Q_BLOCK), F32),
            pltpu.VMEM((KV_LATENT, N_HEADS_A * Q_BLOCK), BF16),
            pltpu.VMEM((LANES, IDX_HEADS * Q_BLOCK), BF16),
            pltpu.VMEM((KV_LATENT + KVT_PAD, N_HEADS_A * Q_BLOCK), F32),
            pltpu.VMEM((N_HEADS_A, Q_BLOCK), F32),
            pltpu.VMEM((1, Q_BLOCK), F32),
            pltpu.VMEM((4, FOLD_ROWS, Q_BLOCK), F32),
            pltpu.VMEM((SUBLANES, Q_BLOCK), F32),
        ],
        compiler_params=_params("parallel", "arbitrary"),
        name="dsa_attention",
    )(qt, iqt, ikwt, iqt, ikwt, ik.reshape(b, s, LANES), ckv.reshape(b, s, KV_LATENT),
      ckvt.reshape(b, nchunks, KV_LATENT + KVT_PAD, kc), wukt.astype(BF16), wuvt.astype(BF16))


def _causal_conv(x, w, halo):
    taps = w.shape[0]
    top_row = lax.broadcasted_iota(jnp.int32, halo.shape, 0)
    y = x * w[taps - 1:taps, :]
    for k in range(1, taps):
        xk = pltpu.roll(x, k, 0)
        top = jnp.where(top_row < k, pltpu.roll(halo, k, 0), xk[:SUBLANES, :])
        xk = jnp.concatenate([top, xk[SUBLANES:, :]], axis=0)
        y = y + xk * w[taps - 1 - k:taps - k, :]
    return y


def _split3(x):
    p1 = x.astype(BF16)
    r = x - p1.astype(F32)
    p2 = r.astype(BF16)
    return p1, p2, (r - p2.astype(F32)).astype(BF16)


def _ssd_kernel(xbc_ref, z_ref, dt_ref, cw_ref, cb_ref, dtb_ref, alog_ref, dexp_ref, nw_ref, e_ref,
                o_ref, halo_scr, state_scr):
    c = pl.program_id(1)
    L = SSM_CHUNK
    n = SSM_STATE
    gw = SSM_D_INNER // SSM_GROUPS

    @pl.when(c == 0)
    def _():
        halo_scr[...] = jnp.zeros(halo_scr.shape, F32)
        state_scr[...] = jnp.zeros(state_scr.shape, F32)

    xbc = xbc_ref[0]
    conv = _causal_conv(xbc, cw_ref[...], halo_scr[...]) + cb_ref[...]
    halo_scr[...] = xbc[L - SUBLANES:, :]
    act = conv * jax.nn.sigmoid(conv)
    xs = act[:, :SSM_D_INNER]
    bm = act[:, SSM_D_INNER:SSM_D_INNER + SSM_GROUPS * n]
    cm = act[:, SSM_D_INNER + SSM_GROUPS * n:]

    dt = jax.nn.softplus(dt_ref[0] + dtb_ref[...])
    a = dt * (-jnp.exp(alog_ref[...]))
    ri = lax.broadcasted_iota(jnp.int32, (L, L), 0)
    ci = lax.broadcasted_iota(jnp.int32, (L, L), 1)
    tri = ri >= ci
    tri_b = jnp.where(tri, 1.0, 0.0).astype(BF16)
    cs3 = _dot(tri_b, jnp.concatenate(_split3(a), axis=1))
    a_cs = cs3[:, :LANES] + cs3[:, LANES:2 * LANES] + cs3[:, 2 * LANES:]
    a_cs_t = a_cs.T
    w_end = dt * jnp.exp(a_cs[L - 1:L, :] - a_cs)
    grow_c = jnp.exp(a_cs)
    pieces = [p for arr in (dt, w_end, grow_c) for p in _split3(arr)]
    ex = _dot(jnp.concatenate(pieces, axis=0), e_ref[...])
    dt_e, wend_e, grow = (ex[3 * i * L:(3 * i + 1) * L] + ex[(3 * i + 1) * L:(3 * i + 2) * L]
                          + ex[(3 * i + 2) * L:(3 * i + 3) * L] for i in range(3))
    xdt_b = (xs * dt_e).astype(BF16)
    xend_b = (xs * wend_e).astype(BF16)
    lane = lax.broadcasted_iota(jnp.int32, (L, LANES), 1)

    y_parts = []
    for g in range(SSM_GROUPS):
        bg = bm[:, g * n:(g + 1) * n]
        cg = cm[:, g * n:(g + 1) * n].astype(BF16)
        cbm = _dot_nt(cg, bg.astype(BF16))
        for pr in range(gw // LANES):
            col = g * gw + pr * LANES
            xpair = xdt_b[:, col:col + LANES]
            outs = []
            for v in range(2):
                h = (col // SSM_HEAD_DIM) + v
                seg = a_cs[:, h:h + 1] - a_cs_t[h:h + 1, :]
                dec = jnp.exp(jnp.where(tri, seg, -jnp.inf))
                outs.append(_dot((cbm * dec).astype(BF16), xpair))
            y_parts.append(jnp.where(lane < SSM_HEAD_DIM, outs[0], outs[1]))
    y = jnp.concatenate(y_parts, axis=1)

    offs = []
    for g in range(SSM_GROUPS):
        sl = slice(g * gw, (g + 1) * gw)
        bg_t = bm[:, g * n:(g + 1) * n].T.astype(BF16)
        cg = cm[:, g * n:(g + 1) * n].astype(BF16)
        st = state_scr[g]
        offs.append(_dot(cg, st.astype(BF16)))
        state_scr[g] = st * grow[L - 1:L, sl] + _dot(bg_t, xend_b[:, sl])
    y = y + jnp.concatenate(offs, axis=1) * grow + xs * dexp_ref[...]

    z = z_ref[0]
    y = y * (z * jax.nn.sigmoid(z))
    nw = nw_ref[...]
    outs = []
    for g in range(SSM_GROUPS):
        sl = slice(g * gw, (g + 1) * gw)
        outs.append(_rms(y[:, sl], nw[:, sl]))
    o_ref[0] = jnp.concatenate(outs, axis=1).astype(BF16)


def _ssd(xbc, z, dtp, conv_w, conv_b, dt_bias, a_log, d, norm_w, b, s):
    L = SSM_CHUNK
    cx = SSM_D_INNER + 2 * SSM_GROUPS * SSM_STATE
    pad = LANES - SSM_HEADS
    expand = jnp.repeat(jnp.eye(SSM_HEADS, dtype=F32), SSM_HEAD_DIM, axis=1)
    expand = jnp.pad(expand, ((0, pad), (0, 0)))
    blk = lambda c: pl.BlockSpec((1, L, c), lambda bi, i: (bi, i, 0))
    return pl.pallas_call(
        _ssd_kernel,
        grid=(b, s // L),
        in_specs=[blk(cx), blk(SSM_D_INNER), blk(LANES)] + [_vmem_full()] * 7,
        out_specs=blk(SSM_D_INNER),
        out_shape=jax.ShapeDtypeStruct((b, s, SSM_D_INNER), BF16),
        scratch_shapes=[pltpu.VMEM((SUBLANES, cx), F32),
                        pltpu.VMEM((SSM_GROUPS, SSM_STATE, SSM_D_INNER // SSM_GROUPS), F32)],
        compiler_params=_params("parallel", "arbitrary"),
        name="ssd_mixer",
    )(xbc.reshape(b, s, cx), z.reshape(b, s, SSM_D_INNER), dtp.reshape(b, s, LANES),
      conv_w, conv_b.reshape(1, -1), jnp.pad(dt_bias, (0, pad)).reshape(1, -1),
      jnp.pad(a_log, (0, pad)).reshape(1, -1), jnp.repeat(d, SSM_HEAD_DIM).reshape(1, -1),
      norm_w.reshape(1, -1), expand.astype(BF16))


def _swiglu(x, g, wg, wu, wd):
    hn = _rms(x, g).astype(BF16)
    a = _dot(hn, wg[...])
    u = _dot(hn, wu[...])
    act = (a * jax.nn.sigmoid(a) * u).astype(BF16)
    return x + _dot(act, wd[...])


def _mix_ffn_kernel(x_ref, ya_ref, yb_ref, wa, wb, g_ref, wg, wu, wd, o_ref):
    x = x_ref[...] + _dot(ya_ref[...], wa[...]) + _dot(yb_ref[...], wb[...])
    o_ref[...] = _swiglu(x, g_ref[...], wg, wu, wd)


def _mix_ffn(x2, ya, yb, out_w, g, w_gate, w_up, w_down):
    n = x2.shape[0]
    tm = TOKEN_TILE
    ca = N_HEADS_A * HEAD_DIM_A
    w = out_w.astype(BF16)
    row = lambda c: pl.BlockSpec((tm, c), lambda i: (i, 0))
    return pl.pallas_call(
        _mix_ffn_kernel,
        grid=(n // tm,),
        in_specs=[row(D_MODEL), row(ca), row(SSM_D_INNER)] + [_vmem_full()] * 6,
        out_specs=row(D_MODEL),
        out_shape=jax.ShapeDtypeStruct((n, D_MODEL), F32),
        compiler_params=_params("parallel"),
        name="mix_ffn",
    )(x2, ya, yb, w[:ca], w[ca:], g.reshape(1, -1), w_gate.astype(BF16), w_up.astype(BF16), w_down.astype(BF16))


def _conv_ffn_kernel(x_ref, xh_ref, gc_ref, wb, wc, wv, cw_ref, wo, g_ref, wg, wu, wd, fn_ref, o_ref,
                     *, tiles_per_seq):
    i = pl.program_id(0)
    x = x_ref[...]
    xn = _rms(x, gc_ref[...]).astype(BF16)
    u = _dot(xn, wc[...]) * _dot(xn, wv[...])
    xhn = _rms(xh_ref[...], gc_ref[...]).astype(BF16)
    halo = jnp.where(i % tiles_per_seq == 0, 0.0, _dot(xhn, wc[...]) * _dot(xhn, wv[...]))
    y = (_dot(xn, wb[...]) * _causal_conv(u, cw_ref[...], halo)).astype(BF16)
    x = x + _dot(y, wo[...])
    o_ref[...] = _rms(_swiglu(x, g_ref[...], wg, wu, wd), fn_ref[...])


def _conv_ffn(x2, norm_w, in_w, conv_w, out_w, g, w_gate, w_up, w_down, final_norm, s):
    n = x2.shape[0]
    tm = CONV_TOKEN_TILE
    w = in_w.astype(BF16)
    row = pl.BlockSpec((tm, D_MODEL), lambda i: (i, 0))
    per = tm // SUBLANES
    halo = pl.BlockSpec((SUBLANES, D_MODEL), lambda i: (jnp.maximum(i * per - 1, 0), 0))
    return pl.pallas_call(
        functools.partial(_conv_ffn_kernel, tiles_per_seq=s // tm),
        grid=(n // tm,),
        in_specs=[row, halo] + [_vmem_full()] * 11,
        out_specs=row,
        out_shape=jax.ShapeDtypeStruct((n, D_MODEL), F32),
        compiler_params=_params("parallel"),
        name="conv_ffn",
    )(x2, x2, norm_w.reshape(1, -1), w[:, :SC_WIDTH], w[:, SC_WIDTH:2 * SC_WIDTH], w[:, 2 * SC_WIDTH:],
      conv_w, out_w.astype(BF16), g.reshape(1, -1), w_gate.astype(BF16), w_up.astype(BF16), w_down.astype(BF16),
      final_norm.reshape(1, -1))


def kernel(x, l0_attn_norm, l0_in_w, l0_kv_norm, l0_w_uk, l0_w_uv, l0_conv_w, l0_conv_b, l0_dt_bias,
           l0_A_log, l0_D, l0_ssm_norm, l0_out_w, l0_ffn_norm, l0_w_gate, l0_w_up, l0_w_down,
           l1_conv_norm, l1_in_w, l1_conv_w, l1_out_w, l1_ffn_norm, l1_w_gate, l1_w_up, l1_w_down,
           final_norm):
    b, s, d = x.shape
    assert d == D_MODEL and s % TOKEN_TILE == 0 and s % CONV_TOKEN_TILE == 0 and s % KEY_CHUNK == 0 and s % SSM_CHUNK == 0
    x2 = x.reshape(b * s, d)
    qt, iqt, ikwt, ckv, ckvt, ik, z, xbc, dtp = _in_proj0(x2, l0_attn_norm, l0_in_w, l0_kv_norm)
    ya = _dsa(qt, iqt, ikwt, ik, ckv, ckvt, l0_w_uk, l0_w_uv, b, s)
    yb = _ssd(xbc, z, dtp, l0_conv_w, l0_conv_b, l0_dt_bias, l0_A_log, l0_D, l0_ssm_norm, b, s)
    x2 = _mix_ffn(x2, ya, yb.reshape(b * s, -1), l0_out_w, l0_ffn_norm, l0_w_gate, l0_w_up, l0_w_down)
    x2 = _conv_ffn(x2, l1_conv_norm, l1_in_w, l1_conv_w, l1_out_w, l1_ffn_norm, l1_w_gate, l1_w_up, l1_w_down,
                   final_norm, s)
    return x2.reshape(b, s, d)
```

```python
import functools

import jax
import jax.numpy as jnp
import numpy as np
from jax import lax
from jax.experimental import pallas as pl
from jax.experimental.pallas import tpu as pltpu

D_MODEL = 1024
N_HEADS_A = 8
HEAD_DIM_A = 64
KV_LATENT = 256
IDX_HEADS = 8
IDX_DIM = 64
TOPK_MAX = 256
Q_BLOCK = 128
SSM_D_INNER = 1024
SSM_HEADS = 16
SSM_HEAD_DIM = SSM_D_INNER // SSM_HEADS
SSM_GROUPS = 2
SSM_STATE = 128
SSM_CONV = 4
SSM_CHUNK = 128
SC_WIDTH = D_MODEL
SC_CONV = 3
D_FF = -(-8 * D_MODEL // (3 * 256)) * 256
EPS = 1e-6

LANES = 128
SUBLANES = 8
TOKEN_TILE = 512
CONV_TOKEN_TILE = 512
KEY_CHUNK = 512
KVT_PAD = 16
VMEM_LIMIT = 56 * 1024 * 1024

F32 = jnp.float32
BF16 = jnp.bfloat16
NEG_BIG = -1e30
F32_LOWEST = float(np.finfo(np.float32).min)
KEY_LOWEST = int(np.array(F32_LOWEST, np.float32).view(np.int32)) ^ 0x7FFFFFFF


def _vmem_full():
    return pl.BlockSpec(memory_space=pltpu.VMEM)


def _params(*sem):
    return pltpu.CompilerParams(dimension_semantics=sem, vmem_limit_bytes=VMEM_LIMIT)


def _rms(x, w):
    return x * lax.rsqrt(jnp.mean(x * x, axis=-1, keepdims=True) + EPS) * w


def _dot(a, b):
    return jnp.dot(a, b, preferred_element_type=F32)


def _dot_nt(a, b):
    return lax.dot_general(a, b, (((1,), (1,)), ((), ())), preferred_element_type=F32)


def _in0_kernel(x_ref, g_ref, wqt, wiqt, wikwt, wckv, wikw, wz, wxbc, wdt, kvn_ref,
                qt_o, iqt_o, ikwt_o, ckv_o, ckvt_o, ik_o, z_o, xbc_o, dt_o):
    xn = _rms(x_ref[...], g_ref[...]).astype(BF16)
    qt_o[...] = _dot_nt(wqt[...], xn).astype(BF16)
    iqt_o[...] = _dot_nt(wiqt[...], xn).astype(BF16)
    ikwt_o[...] = _dot_nt(wikwt[...], xn)
    c = _rms(_dot(xn, wckv[...]), kvn_ref[...])
    ckv_o[...] = c.astype(BF16)
    ones_rows = (lax.broadcasted_iota(jnp.int32, (KVT_PAD, c.shape[0]), 0) == 0).astype(F32)
    ckvt_o[0] = jnp.concatenate([c.T, ones_rows], axis=0).astype(BF16)
    ik_o[...] = _dot(xn, wikw[...]).astype(BF16)
    z_o[...] = _dot(xn, wz[...])
    xbc_o[...] = _dot(xn, wxbc[...])
    dt_o[...] = _dot(xn, wdt[...])


def _in_proj0(x2, g, in_w, kv_norm):
    n = x2.shape[0]
    cuts = np.cumsum((N_HEADS_A * HEAD_DIM_A, KV_LATENT, IDX_HEADS * IDX_DIM, IDX_DIM, IDX_HEADS,
                      SSM_D_INNER, SSM_D_INNER + 2 * SSM_GROUPS * SSM_STATE, SSM_HEADS))
    w = in_w.astype(BF16)
    wq, wckv, wiq = w[:, :cuts[0]], w[:, cuts[0]:cuts[1]], w[:, cuts[1]:cuts[2]]
    wikw = jnp.pad(w[:, cuts[2]:cuts[4]], ((0, 0), (0, LANES - IDX_DIM - IDX_HEADS)))
    wz, wxbc = w[:, cuts[4]:cuts[5]], w[:, cuts[5]:cuts[6]]
    wdt = jnp.pad(w[:, cuts[6]:cuts[7]], ((0, 0), (0, LANES - SSM_HEADS)))
    tm = KEY_CHUNK
    row = lambda c: pl.BlockSpec((tm, c), lambda i: (i, 0))
    col = lambda r: pl.BlockSpec((r, tm), lambda i: (0, i))
    hd = N_HEADS_A * HEAD_DIM_A
    out_specs = [col(hd), col(hd), col(LANES), row(KV_LATENT),
                 pl.BlockSpec((1, KV_LATENT + KVT_PAD, tm), lambda i: (i, 0, 0)),
                 row(LANES), row(SSM_D_INNER), row(wxbc.shape[1]), row(LANES)]
    sds = jax.ShapeDtypeStruct
    out_shape = [sds((hd, n), BF16), sds((hd, n), BF16), sds((LANES, n), F32), sds((n, KV_LATENT), BF16),
                 sds((n // tm, KV_LATENT + KVT_PAD, tm), BF16), sds((n, LANES), BF16), sds((n, SSM_D_INNER), F32),
                 sds((n, wxbc.shape[1]), F32), sds((n, LANES), F32)]
    return pl.pallas_call(
        _in0_kernel,
        grid=(n // tm,),
        in_specs=[row(D_MODEL)] + [_vmem_full()] * 10,
        out_specs=out_specs,
        out_shape=out_shape,
        compiler_params=_params("parallel"),
        name="in_proj0",
    )(x2, g.reshape(1, -1), wq.T, wiq.T, wikw.T, wckv, wikw, wz, wxbc, wdt, kv_norm.reshape(1, -1))


def _f2k(x):
    b = pltpu.bitcast(x, jnp.int32)
    return jnp.where(b < 0, b ^ jnp.int32(0x7FFFFFFF), b)


def _k2f(k):
    b = jnp.where(k < 0, k ^ jnp.int32(0x7FFFFFFF), k)
    return pltpu.bitcast(b, F32)


FOLD_ROWS = 8 * SUBLANES
VALUE_PROBES = 10
PROBES_PER_CHECK = 3
UNCHECKED_ROUNDS = 5


def _fold(x, op):
    return op(x.reshape(x.shape[0] // FOLD_ROWS, FOLD_ROWS, LANES), axis=0)


def _any_lane(flag):
    return jnp.max(jnp.where(flag, 1.0, 0.0))


def _dsa_kernel(qt_ref, iqt_ref, wt_ref, iqtn_ref, wtn_ref, ik_ref, ckv_ref, ckvt_ref, wukt_ref, wuvt_ref, o_ref,
                score_scr, qlat_scr, r_scr, acc_scr, m_scr, thr_scr, part_scr, stat_scr,
                *, topk, nblocks):
    i = pl.program_id(1)
    kc = KEY_CHUNK
    qb = Q_BLOCK
    nh = N_HEADS_A
    cur = i % 2
    nxt = 1 - cur
    inext = jnp.minimum(i + 1, nblocks - 1)
    nch = (i * qb) // kc + 1
    nch_next = (inext * qb) // kc + 1
    t0 = i * qb
    key_i = lax.broadcasted_iota(jnp.int32, (kc, LANES), 0)
    rel_i = key_i - lax.broadcasted_iota(jnp.int32, (kc, LANES), 1)

    wscale = (IDX_HEADS ** -0.5) * (IDX_DIM ** -0.5)

    def load_indexer_queries(iqt):
        r_scr[...] = jnp.zeros(r_scr.shape, BF16)
        for h in range(IDX_HEADS):
            r_scr[0:IDX_DIM, h * qb:(h + 1) * qb] = iqt[h * IDX_DIM:(h + 1) * IDX_DIM, :]

    def reset_partials():
        part_scr[0] = jnp.full((FOLD_ROWS, LANES), jnp.inf, F32)
        part_scr[1] = jnp.full((FOLD_ROWS, LANES), -jnp.inf, F32)
        part_scr[2] = jnp.zeros((FOLD_ROWS, LANES), F32)
        part_scr[3] = jnp.zeros((FOLD_ROWS, LANES), F32)

    def score_chunk(j, buf, wts, tq0):
        base = pl.multiple_of(j * kc, kc)
        lt = _dot(ik_ref[0, pl.ds(base, kc), :], r_scr[...])
        acc = None
        for h in range(IDX_HEADS):
            term = jnp.maximum(lt[:, h * qb:(h + 1) * qb], 0.0) * wts[h:h + 1, :]
            acc = term if acc is None else acc + term
        sc = jnp.where(rel_i <= (tq0 - base), acc, -jnp.inf)
        score_scr[buf, pl.ds(base, kc), :] = sc
        part_scr[0] = jnp.minimum(part_scr[0], _fold(acc, jnp.min))
        part_scr[1] = jnp.maximum(part_scr[1], _fold(acc, jnp.max))
        part_scr[2] = part_scr[2] + _fold(jnp.where(sc > 0.0, 1.0, 0.0), jnp.sum)
        part_scr[3] = part_scr[3] + _fold(jnp.where(sc >= 0.0, 1.0, 0.0), jnp.sum)

    def publish_stats():
        stat_scr[0:1, :] = jnp.min(part_scr[0], axis=0, keepdims=True)
        stat_scr[1:2, :] = jnp.max(part_scr[1], axis=0, keepdims=True)
        stat_scr[2:3, :] = jnp.sum(part_scr[2], axis=0, keepdims=True)
        stat_scr[3:4, :] = jnp.sum(part_scr[3], axis=0, keepdims=True)

    @pl.when(i == 0)
    def _():
        load_indexer_queries(iqt_ref[...])
        reset_partials()
        score_chunk(0, cur, wt_ref[...] * wscale, t0)
        publish_stats()

    colmin = stat_scr[0:1, :]
    colmax = stat_scr[1:2, :]
    npos = stat_scr[2:3, :].astype(jnp.int32)
    nnonneg = stat_scr[3:4, :].astype(jnp.int32)

    def count(pred):
        def body(j, acc):
            base = pl.multiple_of(j * kc, kc)
            blk = score_scr[cur, pl.ds(base, kc), :]
            return acc + _fold(jnp.where(pred(blk, base), 1.0, 0.0), jnp.sum)
        acc = lax.fori_loop(0, nch, body, jnp.zeros((FOLD_ROWS, LANES), F32))
        return jnp.sum(acc, axis=0, keepdims=True).astype(jnp.int32)

    t_pos = t0 + lax.broadcasted_iota(jnp.int32, (1, qb), 1)
    nvalid = t_pos + 1
    allsel = nvalid <= topk
    key_zero, key_tiny, key_negzero = 0, 1, -1
    at_zero = jnp.logical_and(npos < topk, nnonneg >= topk)
    above = npos >= topk
    below = nnonneg < topk
    lo0 = jnp.where(at_zero, key_zero, jnp.where(above, key_tiny, _f2k(colmin)))
    cnt0 = jnp.where(at_zero, nnonneg, jnp.where(above, npos, nvalid))
    hi0 = jnp.where(at_zero, key_zero + 1, jnp.where(below, key_negzero, _f2k(colmax) + 1))
    lo0 = jnp.where(allsel, jnp.int32(KEY_LOWEST), lo0)
    hi0 = jnp.where(allsel, jnp.int32(KEY_LOWEST + 1), hi0)
    cnt0 = jnp.where(allsel, jnp.int32(topk), cnt0)
    go0 = _any_lane(lo0 + 1 < hi0)
    cnt_hi0 = jnp.where(below, nnonneg, 0)

    def search_step(lo, hi, cnt_lo, cnt_hi, by_value):
        active = lo + 1 < hi
        mid_k = (lo & hi) + ((lo ^ hi) >> 1)
        mid_v = _f2k(0.5 * _k2f(lo) + 0.5 * _k2f(hi - 1))
        mid_v = jnp.minimum(jnp.maximum(mid_v, lo + 1), hi - 1)
        mid = jnp.where(by_value, mid_v, mid_k)
        cand = _k2f(mid)
        cnt = count(lambda blk, base: blk >= cand)
        ge = cnt >= topk
        up = jnp.logical_and(active, ge)
        dn = jnp.logical_and(active, jnp.logical_not(ge))
        hit = jnp.logical_and(active, cnt == topk)
        lo = jnp.where(up, mid, lo)
        hi = jnp.where(hit, mid + 1, jnp.where(dn, mid, hi))
        cnt_lo = jnp.where(up, cnt, cnt_lo)
        cnt_hi = jnp.where(dn, cnt, cnt_hi)
        return lo, hi, cnt_lo, cnt_hi

    def resolve_small(lo, hi, cnt_lo, cnt_hi):
        active = lo + 1 < hi
        v_below, v_hi = _k2f(lo - 1), _k2f(hi)

        def body(j, c):
            mn, mx = c
            base = pl.multiple_of(j * kc, kc)
            blk = score_scr[cur, pl.ds(base, kc), :]
            mn = jnp.minimum(mn, _fold(jnp.where(blk > v_below, blk, jnp.inf), jnp.min))
            mx = jnp.maximum(mx, _fold(jnp.where(blk < v_hi, blk, -jnp.inf), jnp.max))
            return mn, mx
        mn, mx = lax.fori_loop(0, nch, body, (jnp.full((FOLD_ROWS, LANES), jnp.inf, F32),
                                              jnp.full((FOLD_ROWS, LANES), -jnp.inf, F32)))
        mn = jnp.min(mn, axis=0, keepdims=True)
        mx = jnp.max(mx, axis=0, keepdims=True)
        inside = cnt_lo - cnt_hi
        small = jnp.logical_and(active, inside <= 2)
        want_top = (topk - cnt_hi) == 1
        twins = jnp.logical_and(mx == mn, inside == 2)
        key_t = _f2k(jnp.where(want_top, mx, mn))
        cnt_t = jnp.where(want_top, cnt_hi + 1 + jnp.where(twins, 1, 0), cnt_lo)
        return (jnp.where(small, key_t, lo), jnp.where(small, key_t + 1, hi), jnp.where(small, cnt_t, cnt_lo), cnt_hi)

    def bis_cond(c):
        return c[5] > 0.0

    def probe_round(r, c):
        for u in range(PROBES_PER_CHECK):
            c = search_step(*c, r * PROBES_PER_CHECK + u < VALUE_PROBES)
        return c

    def bis_body(c):
        r = c[4]
        st = resolve_small(*probe_round(r, c[:4]))
        return (*st, r + 1, _any_lane(st[0] + 1 < st[1]))

    first = jnp.where(go0 > 0.0, UNCHECKED_ROUNDS, 0)
    st = resolve_small(*lax.fori_loop(0, first, probe_round, (lo0, hi0, cnt0, cnt_hi0)))
    lo, _, cnt_lo, _, _, _ = lax.while_loop(bis_cond, bis_body, (*st, first, _any_lane(st[0] + 1 < st[1])))
    thr_scr[...] = _k2f(lo)
    tied = cnt_lo > topk

    @pl.when(_any_lane(tied) > 0.0)
    def _():
        thr = thr_scr[...]
        need = topk - count(lambda blk, base: blk > thr)

        def idx_body(_, c):
            plo, phi = c
            pm = (plo + phi) >> 1
            cnt = count(lambda blk, base: jnp.logical_and(blk == thr, (key_i + base) <= pm))
            ok = cnt >= need
            return jnp.where(ok, plo, pm), jnp.where(ok, pm, phi)

        nbits = int(np.ceil(np.log2(score_scr.shape[1]))) + 1
        _, cut = lax.fori_loop(0, nbits, idx_body,
                               (jnp.full((1, qb), -1, jnp.int32), jnp.broadcast_to(nch * kc - 1, (1, qb))))
        cut = jnp.where(tied, cut, jnp.int32(2 ** 30))

        def fix(j, _):
            base = pl.multiple_of(j * kc, kc)
            blk = score_scr[cur, pl.ds(base, kc), :]
            drop = jnp.logical_and(blk == thr, (key_i + base) > cut)
            score_scr[cur, pl.ds(base, kc), :] = jnp.where(drop, -jnp.inf, blk)
            return 0
        lax.fori_loop(0, nch, fix, 0)

    qt = qt_ref[...]
    qscale = (HEAD_DIM_A ** -0.5) * float(np.log2(np.e))
    for p in range(nh // 2):
        r = _dot(wukt_ref[p], qt[p * LANES:(p + 1) * LANES, :])
        for v in range(2):
            h = 2 * p + v
            qlat_scr[:, h * qb:(h + 1) * qb] = (r[v * KV_LATENT:(v + 1) * KV_LATENT, :] * qscale).astype(BF16)
    load_indexer_queries(iqtn_ref[...])
    wts_next = wtn_ref[...] * wscale
    tn0 = inext * qb
    reset_partials()

    m_scr[...] = jnp.full(m_scr.shape, NEG_BIG, F32)
    acc_scr[...] = jnp.zeros(acc_scr.shape, F32)

    def sweep(j, _):
        base = pl.multiple_of(j * kc, kc)
        kv = ckv_ref[0, pl.ds(base, kc), :]
        kvt = ckvt_ref[0, j]
        bias = jnp.where(score_scr[cur, pl.ds(base, kc), :] >= thr_scr[...], 0.0, NEG_BIG)
        s = _dot(kv, qlat_scr[...])
        score_chunk(j, nxt, wts_next, tn0)
        ps, alphas = [], []
        for h in range(nh):
            sh = s[:, h * qb:(h + 1) * qb] + bias
            m_old = m_scr[h:h + 1, :]
            m_new = jnp.maximum(m_old, jnp.max(sh, axis=0, keepdims=True))
            ps.append(jnp.exp2(sh - m_new).astype(BF16))
            alphas.append(jnp.exp2(m_old - m_new))
            m_scr[h:h + 1, :] = m_new
        pv = _dot(kvt, jnp.concatenate(ps, axis=1))
        acc_scr[...] = acc_scr[...] * jnp.concatenate(alphas, axis=1) + pv
        return 0

    lax.fori_loop(0, nch, sweep, 0)

    @pl.when(nch_next > nch)
    def _():
        score_chunk(nch, nxt, wts_next, tn0)

    publish_stats()

    inv_l = 1.0 / acc_scr[KV_LATENT:KV_LATENT + 1, :]
    outs = []
    for h in range(nh):
        cols = slice(h * qb, (h + 1) * qb)
        olat = (acc_scr[0:KV_LATENT, cols] * inv_l[:, cols]).astype(BF16)
        outs.append(_dot(wuvt_ref[h], olat))
    o_ref[...] = jnp.concatenate(outs, axis=0).T.astype(BF16)


def _dsa(qt, iqt, ikwt, ik, ckv, ckvt, w_uk, w_uv, b, s):
    kc = KEY_CHUNK
    nb = s // Q_BLOCK
    nchunks = s // kc
    topk = min(TOPK_MAX, s // 4)
    hd = N_HEADS_A * HEAD_DIM_A
    wukt = jnp.swapaxes(w_uk, 1, 2).reshape(N_HEADS_A // 2, 2, KV_LATENT, HEAD_DIM_A)
    eye2 = jnp.eye(2, dtype=F32)
    wukt = jnp.einsum('pvcd,vu->pvcud', wukt, eye2).reshape(N_HEADS_A // 2, 2 * KV_LATENT, 2 * HEAD_DIM_A)
    wuvt = jnp.swapaxes(w_uv, 1, 2)
    this_blk = lambda bi, i: bi * nb + i
    next_blk = lambda bi, i: bi * nb + jnp.minimum(i + 1, nb - 1)
    qcol = lambda r, blk: pl.BlockSpec((r, Q_BLOCK), lambda bi, i: (0, blk(bi, i)))
    wrow = lambda blk: pl.BlockSpec((IDX_HEADS, Q_BLOCK), lambda bi, i: (IDX_DIM // IDX_HEADS, blk(bi, i)))
    kern = functools.partial(_dsa_kernel, topk=topk, nblocks=nb)
    return pl.pallas_call(
        kern,
        grid=(b, nb),
        in_specs=[qcol(hd, this_blk), qcol(IDX_HEADS * IDX_DIM, this_blk), wrow(this_blk),
                  qcol(IDX_HEADS * IDX_DIM, next_blk), wrow(next_blk),
                  pl.BlockSpec((1, s, LANES), lambda bi, i: (bi, 0, 0)),
                  pl.BlockSpec((1, s, KV_LATENT), lambda bi, i: (bi, 0, 0)),
                  pl.BlockSpec((1, nchunks, KV_LATENT + KVT_PAD, kc), lambda bi, i: (bi, 0, 0, 0)),
                  _vmem_full(), _vmem_full()],
        out_specs=pl.BlockSpec((Q_BLOCK, hd), lambda bi, i: (bi * nb + i, 0)),
        out_shape=jax.ShapeDtypeStruct((b * s, hd), BF16),
        scratch_shapes=[
            pltpu.VMEM((2, s, Q_BLOCK), F32),
            pltpu.VMEM((KV_LATENT, N_HEADS_A * Q_BLOCK), BF16),
            pltpu.VMEM((LANES, IDX_HEADS * Q_BLOCK), BF16),
            pltpu.VMEM((KV_LATENT + KVT_PAD, N_HEADS_A * Q_BLOCK), F32),
            pltpu.VMEM((N_HEADS_A, Q_BLOCK), F32),
            pltpu.VMEM((1, Q_BLOCK), F32),
            pltpu.VMEM((4, FOLD_ROWS, Q_BLOCK), F32),
            pltpu.VMEM((SUBLANES, Q_BLOCK), F32),
        ],
        compiler_params=_params("parallel", "arbitrary"),
        name="dsa_attention",
    )(qt, iqt, ikwt, iqt, ikwt, ik.reshape(b, s, LANES), ckv.reshape(b, s, KV_LATENT),
      ckvt.reshape(b, nchunks, KV_LATENT + KVT_PAD, kc), wukt.astype(BF16), wuvt.astype(BF16))


def _causal_conv(x, w, halo):
    taps = w.shape[0]
    top_row = lax.broadcasted_iota(jnp.int32, halo.shape, 0)
    y = x * w[taps - 1:taps, :]
    for k in range(1, taps):
        xk = pltpu.roll(x, k, 0)
        top = jnp.where(top_row < k, pltpu.roll(halo, k, 0), xk[:SUBLANES, :])
        xk = jnp.concatenate([top, xk[SUBLANES:, :]], axis=0)
        y = y + xk * w[taps - 1 - k:taps - k, :]
    return y


def _split3(x):
    p1 = x.astype(BF16)
    r = x - p1.astype(F32)
    p2 = r.astype(BF16)
    return p1, p2, (r - p2.astype(F32)).astype(BF16)


def _ssd_kernel(xbc_ref, z_ref, dt_ref, cw_ref, cb_ref, dtb_ref, alog_ref, dexp_ref, nw_ref, e_ref,
                o_ref, halo_scr, state_scr):
    c = pl.program_id(1)
    L = SSM_CHUNK
    n = SSM_STATE
    gw = SSM_D_INNER // SSM_GROUPS

    @pl.when(c == 0)
    def _():
        halo_scr[...] = jnp.zeros(halo_scr.shape, F32)
        state_scr[...] = jnp.zeros(state_scr.shape, F32)

    xbc = xbc_ref[0]
    conv = _causal_conv(xbc, cw_ref[...], halo_scr[...]) + cb_ref[...]
    halo_scr[...] = xbc[L - SUBLANES:, :]
    act = conv * jax.nn.sigmoid(conv)
    xs = act[:, :SSM_D_INNER]
    bm = act[:, SSM_D_INNER:SSM_D_INNER + SSM_GROUPS * n]
    cm = act[:, SSM_D_INNER + SSM_GROUPS * n:]

    dt = jax.nn.softplus(dt_ref[0] + dtb_ref[...])
    a = dt * (-jnp.exp(alog_ref[...]))
    ri = lax.broadcasted_iota(jnp.int32, (L, L), 0)
    ci = lax.broadcasted_iota(jnp.int32, (L, L), 1)
    tri = ri >= ci
    tri_b = jnp.where(tri, 1.0, 0.0).astype(BF16)
    cs3 = _dot(tri_b, jnp.concatenate(_split3(a), axis=1))
    a_cs = cs3[:, :LANES] + cs3[:, LANES:2 * LANES] + cs3[:, 2 * LANES:]
    a_cs_t = a_cs.T
    w_end = dt * jnp.exp(a_cs[L - 1:L, :] - a_cs)
    grow_c = jnp.exp(a_cs)
    pieces = [p for arr in (dt, w_end, grow_c) for p in _split3(arr)]
    ex = _dot(jnp.concatenate(pieces, axis=0), e_ref[...])
    dt_e, wend_e, grow = (ex[3 * i * L:(3 * i + 1) * L] + ex[(3 * i + 1) * L:(3 * i + 2) * L]
                          + ex[(3 * i + 2) * L:(3 * i + 3) * L] for i in range(3))
    xdt_b = (xs * dt_e).astype(BF16)
    xend_b = (xs * wend_e).astype(BF16)
    lane = lax.broadcasted_iota(jnp.int32, (L, LANES), 1)

    y_parts = []
    for g in range(SSM_GROUPS):
        bg = bm[:, g * n:(g + 1) * n]
        cg = cm[:, g * n:(g + 1) * n].astype(BF16)
        cbm = _dot_nt(cg, bg.astype(BF16))
        for pr in range(gw // LANES):
            col = g * gw + pr * LANES
            xpair = xdt_b[:, col:col + LANES]
            outs = []
            for v in range(2):
                h = (col // SSM_HEAD_DIM) + v
                seg = a_cs[:, h:h + 1] - a_cs_t[h:h + 1, :]
                dec = jnp.exp(jnp.where(tri, seg, -jnp.inf))
                outs.append(_dot((cbm * dec).astype(BF16), xpair))
            y_parts.append(jnp.where(lane < SSM_HEAD_DIM, outs[0], outs[1]))
    y = jnp.concatenate(y_parts, axis=1)

    offs = []
    for g in range(SSM_GROUPS):
        sl = slice(g * gw, (g + 1) * gw)
        bg_t = bm[:, g * n:(g + 1) * n].T.astype(BF16)
        cg = cm[:, g * n:(g + 1) * n].astype(BF16)
        st = state_scr[g]
        offs.append(_dot(cg, st.astype(BF16)))
        state_scr[g] = st * grow[L - 1:L, sl] + _dot(bg_t, xend_b[:, sl])
    y = y + jnp.concatenate(offs, axis=1) * grow + xs * dexp_ref[...]

    z = z_ref[0]
    y = y * (z * jax.nn.sigmoid(z))
    nw = nw_ref[...]
    outs = []
    for g in range(SSM_GROUPS):
        sl = slice(g * gw, (g + 1) * gw)
        outs.append(_rms(y[:, sl], nw[:, sl]))
    o_ref[0] = jnp.concatenate(outs, axis=1).astype(BF16)


def _ssd(xbc, z, dtp, conv_w, conv_b, dt_bias, a_log, d, norm_w, b, s):
    L = SSM_CHUNK
    cx = SSM_D_INNER + 2 * SSM_GROUPS * SSM_STATE
    pad = LANES - SSM_HEADS
    expand = jnp.repeat(jnp.eye(SSM_HEADS, dtype=F32), SSM_HEAD_DIM, axis=1)
    expand = jnp.pad(expand, ((0, pad), (0, 0)))
    blk = lambda c: pl.BlockSpec((1, L, c), lambda bi, i: (bi, i, 0))
    return pl.pallas_call(
        _ssd_kernel,
        grid=(b, s // L),
        in_specs=[blk(cx), blk(SSM_D_INNER), blk(LANES)] + [_vmem_full()] * 7,
        out_specs=blk(SSM_D_INNER),
        out_shape=jax.ShapeDtypeStruct((b, s, SSM_D_INNER), BF16),
        scratch_shapes=[pltpu.VMEM((SUBLANES, cx), F32),
                        pltpu.VMEM((SSM_GROUPS, SSM_STATE, SSM_D_INNER // SSM_GROUPS), F32)],
        compiler_params=_params("parallel", "arbitrary"),
        name="ssd_mixer",
    )(xbc.reshape(b, s, cx), z.reshape(b, s, SSM_D_INNER), dtp.reshape(b, s, LANES),
      conv_w, conv_b.reshape(1, -1), jnp.pad(dt_bias, (0, pad)).reshape(1, -1),
      jnp.pad(a_log, (0, pad)).reshape(1, -1), jnp.repeat(d, SSM_HEAD_DIM).reshape(1, -1),
      norm_w.reshape(1, -1), expand.astype(BF16))


def _swiglu(x, g, wg, wu, wd):
    hn = _rms(x, g).astype(BF16)
    a = _dot(hn, wg[...])
    u = _dot(hn, wu[...])
    act = (a * jax.nn.sigmoid(a) * u).astype(BF16)
    return x + _dot(act, wd[...])


def _mix_ffn_kernel(x_ref, ya_ref, yb_ref, wa, wb, g_ref, wg, wu, wd, o_ref):
    x = x_ref[...] + _dot(ya_ref[...], wa[...]) + _dot(yb_ref[...], wb[...])
    o_ref[...] = _swiglu(x, g_ref[...], wg, wu, wd)


def _mix_ffn(x2, ya, yb, out_w, g, w_gate, w_up, w_down):
    n = x2.shape[0]
    tm = TOKEN_TILE
    ca = N_HEADS_A * HEAD_DIM_A
    w = out_w.astype(BF16)
    row = lambda c: pl.BlockSpec((tm, c), lambda i: (i, 0))
    return pl.pallas_call(
        _mix_ffn_kernel,
        grid=(n // tm,),
        in_specs=[row(D_MODEL), row(ca), row(SSM_D_INNER)] + [_vmem_full()] * 6,
        out_specs=row(D_MODEL),
        out_shape=jax.ShapeDtypeStruct((n, D_MODEL), F32),
        compiler_params=_params("parallel"),
        name="mix_ffn",
    )(x2, ya, yb, w[:ca], w[ca:], g.reshape(1, -1), w_gate.astype(BF16), w_up.astype(BF16), w_down.astype(BF16))


def _conv_ffn_kernel(x_ref, xh_ref, gc_ref, wb, wc, wv, cw_ref, wo, g_ref, wg, wu, wd, fn_ref, o_ref,
                     *, tiles_per_seq):
    i = pl.program_id(0)
    x = x_ref[...]
    xn = _rms(x, gc_ref[...]).astype(BF16)
    u = _dot(xn, wc[...]) * _dot(xn, wv[...])
    xhn = _rms(xh_ref[...], gc_ref[...]).astype(BF16)
    halo = jnp.where(i % tiles_per_seq == 0, 0.0, _dot(xhn, wc[...]) * _dot(xhn, wv[...]))
    y = (_dot(xn, wb[...]) * _causal_conv(u, cw_ref[...], halo)).astype(BF16)
    x = x + _dot(y, wo[...])
    o_ref[...] = _rms(_swiglu(x, g_ref[...], wg, wu, wd), fn_ref[...])


def _conv_ffn(x2, norm_w, in_w, conv_w, out_w, g, w_gate, w_up, w_down, final_norm, s):
    n = x2.shape[0]
    tm = CONV_TOKEN_TILE
    w = in_w.astype(BF16)
    row = pl.BlockSpec((tm, D_MODEL), lambda i: (i, 0))
    per = tm // SUBLANES
    halo = pl.BlockSpec((SUBLANES, D_MODEL), lambda i: (jnp.maximum(i * per - 1, 0), 0))
    return pl.pallas_call(
        functools.partial(_conv_ffn_kernel, tiles_per_seq=s // tm),
        grid=(n // tm,),
        in_specs=[row, halo] + [_vmem_full()] * 11,
        out_specs=row,
        out_shape=jax.ShapeDtypeStruct((n, D_MODEL), F32),
        compiler_params=_params("parallel"),
        name="conv_ffn",
    )(x2, x2, norm_w.reshape(1, -1), w[:, :SC_WIDTH], w[:, SC_WIDTH:2 * SC_WIDTH], w[:, 2 * SC_WIDTH:],
      conv_w, out_w.astype(BF16), g.reshape(1, -1), w_gate.astype(BF16), w_up.astype(BF16), w_down.astype(BF16),
      final_norm.reshape(1, -1))


def kernel(x, l0_attn_norm, l0_in_w, l0_kv_norm, l0_w_uk, l0_w_uv, l0_conv_w, l0_conv_b, l0_dt_bias,
           l0_A_log, l0_D, l0_ssm_norm, l0_out_w, l0_ffn_norm, l0_w_gate, l0_w_up, l0_w_down,
           l1_conv_norm, l1_in_w, l1_conv_w, l1_out_w, l1_ffn_norm, l1_w_gate, l1_w_up, l1_w_down,
           final_norm):
    b, s, d = x.shape
    assert d == D_MODEL and s % TOKEN_TILE == 0 and s % CONV_TOKEN_TILE == 0 and s % KEY_CHUNK == 0 and s % SSM_CHUNK == 0
    x2 = x.reshape(b * s, d)
    qt, iqt, ikwt, ckv, ckvt, ik, z, xbc, dtp = _in_proj0(x2, l0_attn_norm, l0_in_w, l0_kv_norm)
    ya = _dsa(qt, iqt, ikwt, ik, ckv, ckvt, l0_w_uk, l0_w_uv, b, s)
    yb = _ssd(xbc, z, dtp, l0_conv_w, l0_conv_b, l0_dt_bias, l0_A_log, l0_D, l0_ssm_norm, b, s)
    x2 = _mix_ffn(x2, ya, yb.reshape(b * s, -1), l0_out_w, l0_ffn_norm, l0_w_gate, l0_w_up, l0_w_down)
    x2 = _conv_ffn(x2, l1_conv_norm, l1_in_w, l1_conv_w, l1_out_w, l1_ffn_norm, l1_w_gate, l1_w_up, l1_w_down,
                   final_norm, s)
    return x2.reshape(b, s, d)
```

```python
import functools

import jax
import jax.numpy as jnp
import numpy as np
from jax import lax
from jax.experimental import pallas as pl
from jax.experimental.pallas import tpu as pltpu

D_MODEL = 1024
N_HEADS_A = 8
HEAD_DIM_A = 64
KV_LATENT = 256
IDX_HEADS = 8
IDX_DIM = 64
TOPK_MAX = 256
Q_BLOCK = 128
SSM_D_INNER = 1024
SSM_HEADS = 16
SSM_HEAD_DIM = SSM_D_INNER // SSM_HEADS
SSM_GROUPS = 2
SSM_STATE = 128
SSM_CONV = 4
SSM_CHUNK = 128
SC_WIDTH = D_MODEL
SC_CONV = 3
D_FF = -(-8 * D_MODEL // (3 * 256)) * 256
EPS = 1e-6

LANES = 128
SUBLANES = 8
TOKEN_TILE = 512
CONV_TOKEN_TILE = 512
KEY_CHUNK = 512
KVT_PAD = 16
VMEM_LIMIT = 56 * 1024 * 1024

F32 = jnp.float32
BF16 = jnp.bfloat16
NEG_BIG = -1e30
F32_LOWEST = float(np.finfo(np.float32).min)
KEY_LOWEST = int(np.array(F32_LOWEST, np.float32).view(np.int32)) ^ 0x7FFFFFFF


def _vmem_full():
    return pl.BlockSpec(memory_space=pltpu.VMEM)


def _params(*sem):
    return pltpu.CompilerParams(dimension_semantics=sem, vmem_limit_bytes=VMEM_LIMIT)


def _rms(x, w):
    return x * lax.rsqrt(jnp.mean(x * x, axis=-1, keepdims=True) + EPS) * w


def _dot(a, b):
    return jnp.dot(a, b, preferred_element_type=F32)


def _dot_nt(a, b):
    return lax.dot_general(a, b, (((1,), (1,)), ((), ())), preferred_element_type=F32)


def _in0_kernel(x_ref, g_ref, wqt, wiqt, wikwt, wckv, wikw, wz, wxbc, wdt, kvn_ref,
                qt_o, iqt_o, ikwt_o, ckv_o, ckvt_o, ik_o, z_o, xbc_o, dt_o):
    xn = _rms(x_ref[...], g_ref[...]).astype(BF16)
    qt_o[...] = _dot_nt(wqt[...], xn).astype(BF16)
    iqt_o[...] = _dot_nt(wiqt[...], xn).astype(BF16)
    ikwt_o[...] = _dot_nt(wikwt[...], xn)
    c = _rms(_dot(xn, wckv[...]), kvn_ref[...])
    ckv_o[...] = c.astype(BF16)
    ones_rows = (lax.broadcasted_iota(jnp.int32, (KVT_PAD, c.shape[0]), 0) == 0).astype(F32)
    ckvt_o[0] = jnp.concatenate([c.T, ones_rows], axis=0).astype(BF16)
    ik_o[...] = _dot(xn, wikw[...]).astype(BF16)
    z_o[...] = _dot(xn, wz[...])
    xbc_o[...] = _dot(xn, wxbc[...])
    dt_o[...] = _dot(xn, wdt[...])


def _in_proj0(x2, g, in_w, kv_norm):
    n = x2.shape[0]
    cuts = np.cumsum((N_HEADS_A * HEAD_DIM_A, KV_LATENT, IDX_HEADS * IDX_DIM, IDX_DIM, IDX_HEADS,
                      SSM_D_INNER, SSM_D_INNER + 2 * SSM_GROUPS * SSM_STATE, SSM_HEADS))
    w = in_w.astype(BF16)
    wq, wckv, wiq = w[:, :cuts[0]], w[:, cuts[0]:cuts[1]], w[:, cuts[1]:cuts[2]]
    wikw = jnp.pad(w[:, cuts[2]:cuts[4]], ((0, 0), (0, LANES - IDX_DIM - IDX_HEADS)))
    wz, wxbc = w[:, cuts[4]:cuts[5]], w[:, cuts[5]:cuts[6]]
    wdt = jnp.pad(w[:, cuts[6]:cuts[7]], ((0, 0), (0, LANES - SSM_HEADS)))
    tm = KEY_CHUNK
    row = lambda c: pl.BlockSpec((tm, c), lambda i: (i, 0))
    col = lambda r: pl.BlockSpec((r, tm), lambda i: (0, i))
    hd = N_HEADS_A * HEAD_DIM_A
    out_specs = [col(hd), col(hd), col(LANES), row(KV_LATENT),
                 pl.BlockSpec((1, KV_LATENT + KVT_PAD, tm), lambda i: (i, 0, 0)),
                 row(LANES), row(SSM_D_INNER), row(wxbc.shape[1]), row(LANES)]
    sds = jax.ShapeDtypeStruct
    out_shape = [sds((hd, n), BF16), sds((hd, n), BF16), sds((LANES, n), F32), sds((n, KV_LATENT), BF16),
                 sds((n // tm, KV_LATENT + KVT_PAD, tm), BF16), sds((n, LANES), BF16), sds((n, SSM_D_INNER), F32),
                 sds((n, wxbc.shape[1]), F32), sds((n, LANES), F32)]
    return pl.pallas_call(
        _in0_kernel,
        grid=(n // tm,),
        in_specs=[row(D_MODEL)] + [_vmem_full()] * 10,
        out_specs=out_specs,
        out_shape=out_shape,
        compiler_params=_params("parallel"),
        name="in_proj0",
    )(x2, g.reshape(1, -1), wq.T, wiq.T, wikw.T, wckv, wikw, wz, wxbc, wdt, kv_norm.reshape(1, -1))


def _f2k(x):
    b = pltpu.bitcast(x, jnp.int32)
    return jnp.where(b < 0, b ^ jnp.int32(0x7FFFFFFF), b)


def _k2f(k):
    b = jnp.where(k < 0, k ^ jnp.int32(0x7FFFFFFF), k)
    return pltpu.bitcast(b, F32)


FOLD_ROWS = 4 * SUBLANES
VALUE_PROBES = 10
PROBES_PER_CHECK = 3
UNCHECKED_ROUNDS = 5


def _fold(x, op):
    return op(x.reshape(x.shape[0] // FOLD_ROWS, FOLD_ROWS, LANES), axis=0)


def _any_lane(flag):
    return jnp.max(jnp.where(flag, 1.0, 0.0))


def _dsa_kernel(qt_ref, iqt_ref, wt_ref, iqtn_ref, wtn_ref, ik_ref, ckv_ref, ckvt_ref, wukt_ref, wuvt_ref, o_ref,
                score_scr, qlat_scr, r_scr, acc_scr, m_scr, thr_scr, part_scr, stat_scr,
                *, topk, nblocks):
    i = pl.program_id(1)
    kc = KEY_CHUNK
    qb = Q_BLOCK
    nh = N_HEADS_A
    cur = i % 2
    nxt = 1 - cur
    inext = jnp.minimum(i + 1, nblocks - 1)
    nch = (i * qb) // kc + 1
    nch_next = (inext * qb) // kc + 1
    t0 = i * qb
    key_i = lax.broadcasted_iota(jnp.int32, (kc, LANES), 0)
    rel_i = key_i - lax.broadcasted_iota(jnp.int32, (kc, LANES), 1)

    wscale = (IDX_HEADS ** -0.5) * (IDX_DIM ** -0.5)

    def load_indexer_queries(iqt):
        r_scr[...] = jnp.zeros(r_scr.shape, BF16)
        for h in range(IDX_HEADS):
            r_scr[0:IDX_DIM, h * qb:(h + 1) * qb] = iqt[h * IDX_DIM:(h + 1) * IDX_DIM, :]

    def reset_partials():
        part_scr[0] = jnp.full((FOLD_ROWS, LANES), jnp.inf, F32)
        part_scr[1] = jnp.full((FOLD_ROWS, LANES), -jnp.inf, F32)
        part_scr[2] = jnp.zeros((FOLD_ROWS, LANES), F32)
        part_scr[3] = jnp.zeros((FOLD_ROWS, LANES), F32)

    def score_chunk(j, buf, wts, tq0):
        base = pl.multiple_of(j * kc, kc)
        lt = _dot(ik_ref[0, pl.ds(base, kc), :], r_scr[...])
        acc = None
        for h in range(IDX_HEADS):
            term = jnp.maximum(lt[:, h * qb:(h + 1) * qb], 0.0) * wts[h:h + 1, :]
            acc = term if acc is None else acc + term
        sc = jnp.where(rel_i <= (tq0 - base), acc, -jnp.inf)
        score_scr[buf, pl.ds(base, kc), :] = sc
        part_scr[0] = jnp.minimum(part_scr[0], _fold(acc, jnp.min))
        part_scr[1] = jnp.maximum(part_scr[1], _fold(acc, jnp.max))
        part_scr[2] = part_scr[2] + _fold(jnp.where(sc > 0.0, 1.0, 0.0), jnp.sum)
        part_scr[3] = part_scr[3] + _fold(jnp.where(sc >= 0.0, 1.0, 0.0), jnp.sum)

    def publish_stats():
        stat_scr[0:1, :] = jnp.min(part_scr[0], axis=0, keepdims=True)
        stat_scr[1:2, :] = jnp.max(part_scr[1], axis=0, keepdims=True)
        stat_scr[2:3, :] = jnp.sum(part_scr[2], axis=0, keepdims=True)
        stat_scr[3:4, :] = jnp.sum(part_scr[3], axis=0, keepdims=True)

    @pl.when(i == 0)
    def _():
        load_indexer_queries(iqt_ref[...])
        reset_partials()
        score_chunk(0, cur, wt_ref[...] * wscale, t0)
        publish_stats()

    colmin = stat_scr[0:1, :]
    colmax = stat_scr[1:2, :]
    npos = stat_scr[2:3, :].astype(jnp.int32)
    nnonneg = stat_scr[3:4, :].astype(jnp.int32)

    def count(pred):
        def body(j, acc):
            base = pl.multiple_of(j * kc, kc)
            blk = score_scr[cur, pl.ds(base, kc), :]
            return acc + _fold(jnp.where(pred(blk, base), 1.0, 0.0), jnp.sum)
        acc = lax.fori_loop(0, nch, body, jnp.zeros((FOLD_ROWS, LANES), F32))
        return jnp.sum(acc, axis=0, keepdims=True).astype(jnp.int32)

    t_pos = t0 + lax.broadcasted_iota(jnp.int32, (1, qb), 1)
    nvalid = t_pos + 1
    allsel = nvalid <= topk
    key_zero, key_tiny, key_negzero = 0, 1, -1
    at_zero = jnp.logical_and(npos < topk, nnonneg >= topk)
    above = npos >= topk
    below = nnonneg < topk
    lo0 = jnp.where(at_zero, key_zero, jnp.where(above, key_tiny, _f2k(colmin)))
    cnt0 = jnp.where(at_zero, nnonneg, jnp.where(above, npos, nvalid))
    hi0 = jnp.where(at_zero, key_zero + 1, jnp.where(below, key_negzero, _f2k(colmax) + 1))
    lo0 = jnp.where(allsel, jnp.int32(KEY_LOWEST), lo0)
    hi0 = jnp.where(allsel, jnp.int32(KEY_LOWEST + 1), hi0)
    cnt0 = jnp.where(allsel, jnp.int32(topk), cnt0)
    go0 = _any_lane(lo0 + 1 < hi0)
    cnt_hi0 = jnp.where(below, nnonneg, 0)

    def search_step(lo, hi, cnt_lo, cnt_hi, by_value):
        active = lo + 1 < hi
        mid_k = (lo & hi) + ((lo ^ hi) >> 1)
        mid_v = _f2k(0.5 * _k2f(lo) + 0.5 * _k2f(hi - 1))
        mid_v = jnp.minimum(jnp.maximum(mid_v, lo + 1), hi - 1)
        mid = jnp.where(by_value, mid_v, mid_k)
        cand = _k2f(mid)
        cnt = count(lambda blk, base: blk >= cand)
        ge = cnt >= topk
        up = jnp.logical_and(active, ge)
        dn = jnp.logical_and(active, jnp.logical_not(ge))
        hit = jnp.logical_and(active, cnt == topk)
        lo = jnp.where(up, mid, lo)
        hi = jnp.where(hit, mid + 1, jnp.where(dn, mid, hi))
        cnt_lo = jnp.where(up, cnt, cnt_lo)
        cnt_hi = jnp.where(dn, cnt, cnt_hi)
        return lo, hi, cnt_lo, cnt_hi

    def resolve_small(lo, hi, cnt_lo, cnt_hi):
        active = lo + 1 < hi
        v_below, v_hi = _k2f(lo - 1), _k2f(hi)

        def body(j, c):
            mn, mx = c
            base = pl.multiple_of(j * kc, kc)
            blk = score_scr[cur, pl.ds(base, kc), :]
            mn = jnp.minimum(mn, _fold(jnp.where(blk > v_below, blk, jnp.inf), jnp.min))
            mx = jnp.maximum(mx, _fold(jnp.where(blk < v_hi, blk, -jnp.inf), jnp.max))
            return mn, mx
        mn, mx = lax.fori_loop(0, nch, body, (jnp.full((FOLD_ROWS, LANES), jnp.inf, F32),
                                              jnp.full((FOLD_ROWS, LANES), -jnp.inf, F32)))
        mn = jnp.min(mn, axis=0, keepdims=True)
        mx = jnp.max(mx, axis=0, keepdims=True)
        inside = cnt_lo - cnt_hi
        small = jnp.logical_and(active, inside <= 2)
        want_top = (topk - cnt_hi) == 1
        twins = jnp.logical_and(mx == mn, inside == 2)
        key_t = _f2k(jnp.where(want_top, mx, mn))
        cnt_t = jnp.where(want_top, cnt_hi + 1 + jnp.where(twins, 1, 0), cnt_lo)
        return (jnp.where(small, key_t, lo), jnp.where(small, key_t + 1, hi), jnp.where(small, cnt_t, cnt_lo), cnt_hi)

    def bis_cond(c):
        return c[5] > 0.0

    def probe_round(r, c):
        for u in range(PROBES_PER_CHECK):
            c = search_step(*c, r * PROBES_PER_CHECK + u < VALUE_PROBES)
        return c

    def bis_body(c):
        r = c[4]
        st = resolve_small(*probe_round(r, c[:4]))
        return (*st, r + 1, _any_lane(st[0] + 1 < st[1]))

    first = jnp.where(go0 > 0.0, UNCHECKED_ROUNDS, 0)
    st = resolve_small(*lax.fori_loop(0, first, probe_round, (lo0, hi0, cnt0, cnt_hi0)))
    lo, _, cnt_lo, _, _, _ = lax.while_loop(bis_cond, bis_body, (*st, first, _any_lane(st[0] + 1 < st[1])))
    thr_scr[...] = _k2f(lo)
    tied = cnt_lo > topk

    @pl.when(_any_lane(tied) > 0.0)
    def _():
        thr = thr_scr[...]
        need = topk - count(lambda blk, base: blk > thr)

        def idx_body(_, c):
            plo, phi = c
            pm = (plo + phi) >> 1
            cnt = count(lambda blk, base: jnp.logical_and(blk == thr, (key_i + base) <= pm))
            ok = cnt >= need
            return jnp.where(ok, plo, pm), jnp.where(ok, pm, phi)

        nbits = int(np.ceil(np.log2(score_scr.shape[1]))) + 1
        _, cut = lax.fori_loop(0, nbits, idx_body,
                               (jnp.full((1, qb), -1, jnp.int32), jnp.broadcast_to(nch * kc - 1, (1, qb))))
        cut = jnp.where(tied, cut, jnp.int32(2 ** 30))

        def fix(j, _):
            base = pl.multiple_of(j * kc, kc)
            blk = score_scr[cur, pl.ds(base, kc), :]
            drop = jnp.logical_and(blk == thr, (key_i + base) > cut)
            score_scr[cur, pl.ds(base, kc), :] = jnp.where(drop, -jnp.inf, blk)
            return 0
        lax.fori_loop(0, nch, fix, 0)

    qt = qt_ref[...]
    qscale = (HEAD_DIM_A ** -0.5) * float(np.log2(np.e))
    for p in range(nh // 2):
        r = _dot(wukt_ref[p], qt[p * LANES:(p + 1) * LANES, :])
        for v in range(2):
            h = 2 * p + v
            qlat_scr[:, h * qb:(h + 1) * qb] = (r[v * KV_LATENT:(v + 1) * KV_LATENT, :] * qscale).astype(BF16)
    load_indexer_queries(iqtn_ref[...])
    wts_next = wtn_ref[...] * wscale
    tn0 = inext * qb
    reset_partials()

    m_scr[...] = jnp.full(m_scr.shape, NEG_BIG, F32)
    acc_scr[...] = jnp.zeros(acc_scr.shape, F32)

    def sweep(j, _):
        base = pl.multiple_of(j * kc, kc)
        kv = ckv_ref[0, pl.ds(base, kc), :]
        kvt = ckvt_ref[0, j]
        bias = jnp.where(score_scr[cur, pl.ds(base, kc), :] >= thr_scr[...], 0.0, NEG_BIG)
        s = _dot(kv, qlat_scr[...])
        score_chunk(j, nxt, wts_next, tn0)
        ps, alphas = [], []
        for h in range(nh):
            sh = s[:, h * qb:(h + 1) * qb] + bias
            m_old = m_scr[h:h + 1, :]
            m_new = jnp.maximum(m_old, jnp.max(sh, axis=0, keepdims=True))
            ps.append(jnp.exp2(sh - m_new).astype(BF16))
            alphas.append(jnp.exp2(m_old - m_new))
            m_scr[h:h + 1, :] = m_new
        pv = _dot(kvt, jnp.concatenate(ps, axis=1))
        acc_scr[...] = acc_scr[...] * jnp.concatenate(alphas, axis=1) + pv
        return 0

    lax.fori_loop(0, nch, sweep, 0)

    @pl.when(nch_next > nch)
    def _():
        score_chunk(nch, nxt, wts_next, tn0)

    publish_stats()

    inv_l = 1.0 / acc_scr[KV_LATENT:KV_LATENT + 1, :]
    outs = []
    for h in range(nh):
        cols = slice(h * qb, (h + 1) * qb)
        olat = (acc_scr[0:KV_LATENT, cols] * inv_l[:, cols]).astype(BF16)
        outs.append(_dot(wuvt_ref[h], olat))
    o_ref[...] = jnp.concatenate(outs, axis=0).T.astype(BF16)


def _dsa(qt, iqt, ikwt, ik, ckv, ckvt, w_uk, w_uv, b, s):
    kc = KEY_CHUNK
    nb = s // Q_BLOCK
    nchunks = s // kc
    topk = min(TOPK_MAX, s // 4)
    hd = N_HEADS_A * HEAD_DIM_A
    wukt = jnp.swapaxes(w_uk, 1, 2).reshape(N_HEADS_A // 2, 2, KV_LATENT, HEAD_DIM_A)
    eye2 = jnp.eye(2, dtype=F32)
    wukt = jnp.einsum('pvcd,vu->pvcud', wukt, eye2).reshape(N_HEADS_A // 2, 2 * KV_LATENT, 2 * HEAD_DIM_A)
    wuvt = jnp.swapaxes(w_uv, 1, 2)
    this_blk = lambda bi, i: bi * nb + i
    next_blk = lambda bi, i: bi * nb + jnp.minimum(i + 1, nb - 1)
    qcol = lambda r, blk: pl.BlockSpec((r, Q_BLOCK), lambda bi, i: (0, blk(bi, i)))
    wrow = lambda blk: pl.BlockSpec((IDX_HEADS, Q_BLOCK), lambda bi, i: (IDX_DIM // IDX_HEADS, blk(bi, i)))
    kern = functools.partial(_dsa_kernel, topk=topk, nblocks=nb)
    return pl.pallas_call(
        kern,
        grid=(b, nb),
        in_specs=[qcol(hd, this_blk), qcol(IDX_HEADS * IDX_DIM, this_blk), wrow(this_blk),
                  qcol(IDX_HEADS * IDX_DIM, next_blk), wrow(next_blk),
                  pl.BlockSpec((1, s, LANES), lambda bi, i: (bi, 0, 0)),
                  pl.BlockSpec((1, s, KV_LATENT), lambda bi, i: (bi, 0, 0)),
                  pl.BlockSpec((1, nchunks, KV_LATENT + KVT_PAD, kc), lambda bi, i: (bi, 0, 0, 0)),
                  _vmem_full(), _vmem_full()],
        out_specs=pl.BlockSpec((Q_BLOCK, hd), lambda bi, i: (bi * nb + i, 0)),
        out_shape=jax.ShapeDtypeStruct((b * s, hd), BF16),
        scratch_shapes=[
            pltpu.VMEM((2, s, Q_BLOCK), F32),
            pltpu.VMEM((KV_LATENT, N_HEADS_A * Q_BLOCK), BF16),
            pltpu.VMEM((LANES, IDX_HEADS * Q_BLOCK), BF16),
            pltpu.VMEM((KV_LATENT + KVT_PAD, N_HEADS_A * Q_BLOCK), F32),
            pltpu.VMEM((N_HEADS_A, Q_BLOCK), F32),
            pltpu.VMEM((1, Q_BLOCK), F32),
            pltpu.VMEM((4, FOLD_ROWS, Q_BLOCK), F32),
            pltpu.VMEM((SUBLANES, Q_BLOCK), F32),
        ],
        compiler_params=_params("parallel", "arbitrary"),
        name="dsa_attention",
    )(qt, iqt, ikwt, iqt, ikwt, ik.reshape(b, s, LANES), ckv.reshape(b, s, KV_LATENT),
      ckvt.reshape(b, nchunks, KV_LATENT + KVT_PAD, kc), wukt.astype(BF16), wuvt.astype(BF16))


def _causal_conv(x, w, halo):
    taps = w.shape[0]
    top_row = lax.broadcasted_iota(jnp.int32, halo.shape, 0)
    y = x * w[taps - 1:taps, :]
    for k in range(1, taps):
        xk = pltpu.roll(x, k, 0)
        top = jnp.where(top_row < k, pltpu.roll(halo, k, 0), xk[:SUBLANES, :])
        xk = jnp.concatenate([top, xk[SUBLANES:, :]], axis=0)
        y = y + xk * w[taps - 1 - k:taps - k, :]
    return y


def _split3(x):
    p1 = x.astype(BF16)
    r = x - p1.astype(F32)
    p2 = r.astype(BF16)
    return p1, p2, (r - p2.astype(F32)).astype(BF16)


def _ssd_kernel(xbc_ref, z_ref, dt_ref, cw_ref, cb_ref, dtb_ref, alog_ref, dexp_ref, nw_ref, e_ref,
                o_ref, halo_scr, state_scr):
    c = pl.program_id(1)
    L = SSM_CHUNK
    n = SSM_STATE
    gw = SSM_D_INNER // SSM_GROUPS

    @pl.when(c == 0)
    def _():
        halo_scr[...] = jnp.zeros(halo_scr.shape, F32)
        state_scr[...] = jnp.zeros(state_scr.shape, F32)

    xbc = xbc_ref[0]
    conv = _causal_conv(xbc, cw_ref[...], halo_scr[...]) + cb_ref[...]
    halo_scr[...] = xbc[L - SUBLANES:, :]
    act = conv * jax.nn.sigmoid(conv)
    xs = act[:, :SSM_D_INNER]
    bm = act[:, SSM_D_INNER:SSM_D_INNER + SSM_GROUPS * n]
    cm = act[:, SSM_D_INNER + SSM_GROUPS * n:]

    dt = jax.nn.softplus(dt_ref[0] + dtb_ref[...])
    a = dt * (-jnp.exp(alog_ref[...]))
    ri = lax.broadcasted_iota(jnp.int32, (L, L), 0)
    ci = lax.broadcasted_iota(jnp.int32, (L, L), 1)
    tri = ri >= ci
    tri_b = jnp.where(tri, 1.0, 0.0).astype(BF16)
    cs3 = _dot(tri_b, jnp.concatenate(_split3(a), axis=1))
    a_cs = cs3[:, :LANES] + cs3[:, LANES:2 * LANES] + cs3[:, 2 * LANES:]
    a_cs_t = a_cs.T
    w_end = dt * jnp.exp(a_cs[L - 1:L, :] - a_cs)
    grow_c = jnp.exp(a_cs)
    pieces = [p for arr in (dt, w_end, grow_c) for p in _split3(arr)]
    ex = _dot(jnp.concatenate(pieces, axis=0), e_ref[...])
    dt_e, wend_e, grow = (ex[3 * i * L:(3 * i + 1) * L] + ex[(3 * i + 1) * L:(3 * i + 2) * L]
                          + ex[(3 * i + 2) * L:(3 * i + 3) * L] for i in range(3))
    xdt_b = (xs * dt_e).astype(BF16)
    xend_b = (xs * wend_e).astype(BF16)
    lane = lax.broadcasted_iota(jnp.int32, (L, LANES), 1)

    y_parts = []
    for g in range(SSM_GROUPS):
        bg = bm[:, g * n:(g + 1) * n]
        cg = cm[:, g * n:(g + 1) * n].astype(BF16)
        cbm = _dot_nt(cg, bg.astype(BF16))
        for pr in range(gw // LANES):
            col = g * gw + pr * LANES
            xpair = xdt_b[:, col:col + LANES]
            outs = []
            for v in range(2):
                h = (col // SSM_HEAD_DIM) + v
                seg = a_cs[:, h:h + 1] - a_cs_t[h:h + 1, :]
                dec = jnp.exp(jnp.where(tri, seg, -jnp.inf))
                outs.append(_dot((cbm * dec).astype(BF16), xpair))
            y_parts.append(jnp.where(lane < SSM_HEAD_DIM, outs[0], outs[1]))
    y = jnp.concatenate(y_parts, axis=1)

    offs = []
    for g in range(SSM_GROUPS):
        sl = slice(g * gw, (g + 1) * gw)
        bg_t = bm[:, g * n:(g + 1) * n].T.astype(BF16)
        cg = cm[:, g * n:(g + 1) * n].astype(BF16)
        st = state_scr[g]
        offs.append(_dot(cg, st.astype(BF16)))
        state_scr[g] = st * grow[L - 1:L, sl] + _dot(bg_t, xend_b[:, sl])
    y = y + jnp.concatenate(offs, axis=1) * grow + xs * dexp_ref[...]

    z = z_ref[0]
    y = y * (z * jax.nn.sigmoid(z))
    nw = nw_ref[...]
    outs = []
    for g in range(SSM_GROUPS):
        sl = slice(g * gw, (g + 1) * gw)
        outs.append(_rms(y[:, sl], nw[:, sl]))
    o_ref[0] = jnp.concatenate(outs, axis=1).astype(BF16)


def _ssd(xbc, z, dtp, conv_w, conv_b, dt_bias, a_log, d, norm_w, b, s):
    L = SSM_CHUNK
    cx = SSM_D_INNER + 2 * SSM_GROUPS * SSM_STATE
    pad = LANES - SSM_HEADS
    expand = jnp.repeat(jnp.eye(SSM_HEADS, dtype=F32), SSM_HEAD_DIM, axis=1)
    expand = jnp.pad(expand, ((0, pad), (0, 0)))
    blk = lambda c: pl.BlockSpec((1, L, c), lambda bi, i: (bi, i, 0))
    return pl.pallas_call(
        _ssd_kernel,
        grid=(b, s // L),
        in_specs=[blk(cx), blk(SSM_D_INNER), blk(LANES)] + [_vmem_full()] * 7,
        out_specs=blk(SSM_D_INNER),
        out_shape=jax.ShapeDtypeStruct((b, s, SSM_D_INNER), BF16),
        scratch_shapes=[pltpu.VMEM((SUBLANES, cx), F32),
                        pltpu.VMEM((SSM_GROUPS, SSM_STATE, SSM_D_INNER // SSM_GROUPS), F32)],
        compiler_params=_params("parallel", "arbitrary"),
        name="ssd_mixer",
    )(xbc.reshape(b, s, cx), z.reshape(b, s, SSM_D_INNER), dtp.reshape(b, s, LANES),
      conv_w, conv_b.reshape(1, -1), jnp.pad(dt_bias, (0, pad)).reshape(1, -1),
      jnp.pad(a_log, (0, pad)).reshape(1, -1), jnp.repeat(d, SSM_HEAD_DIM).reshape(1, -1),
      norm_w.reshape(1, -1), expand.astype(BF16))


def _swiglu(x, g, wg, wu, wd):
    hn = _rms(x, g).astype(BF16)
    a = _dot(hn, wg[...])
    u = _dot(hn, wu[...])
    act = (a * jax.nn.sigmoid(a) * u).astype(BF16)
    return x + _dot(act, wd[...])


def _mix_ffn_kernel(x_ref, ya_ref, yb_ref, wa, wb, g_ref, wg, wu, wd, o_ref):
    x = x_ref[...] + _dot(ya_ref[...], wa[...]) + _dot(yb_ref[...], wb[...])
    o_ref[...] = _swiglu(x, g_ref[...], wg, wu, wd)


def _mix_ffn(x2, ya, yb, out_w, g, w_gate, w_up, w_down):
    n = x2.shape[0]
    tm = TOKEN_TILE
    ca = N_HEADS_A * HEAD_DIM_A
    w = out_w.astype(BF16)
    row = lambda c: pl.BlockSpec((tm, c), lambda i: (i, 0))
    return pl.pallas_call(
        _mix_ffn_kernel,
        grid=(n // tm,),
        in_specs=[row(D_MODEL), row(ca), row(SSM_D_INNER)] + [_vmem_full()] * 6,
        out_specs=row(D_MODEL),
        out_shape=jax.ShapeDtypeStruct((n, D_MODEL), F32),
        compiler_params=_params("parallel"),
        name="mix_ffn",
    )(x2, ya, yb, w[:ca], w[ca:], g.reshape(1, -1), w_gate.astype(BF16), w_up.astype(BF16), w_down.astype(BF16))


def _conv_ffn_kernel(x_ref, xh_ref, gc_ref, wb, wc, wv, cw_ref, wo, g_ref, wg, wu, wd, fn_ref, o_ref,
                     *, tiles_per_seq):
    i = pl.program_id(0)
    x = x_ref[...]
    xn = _rms(x, gc_ref[...]).astype(BF16)
    u = _dot(xn, wc[...]) * _dot(xn, wv[...])
    xhn = _rms(xh_ref[...], gc_ref[...]).astype(BF16)
    halo = jnp.where(i % tiles_per_seq == 0, 0.0, _dot(xhn, wc[...]) * _dot(xhn, wv[...]))
    y = (_dot(xn, wb[...]) * _causal_conv(u, cw_ref[...], halo)).astype(BF16)
    x = x + _dot(y, wo[...])
    o_ref[...] = _rms(_swiglu(x, g_ref[...], wg, wu, wd), fn_ref[...])


def _conv_ffn(x2, norm_w, in_w, conv_w, out_w, g, w_gate, w_up, w_down, final_norm, s):
    n = x2.shape[0]
    tm = CONV_TOKEN_TILE
    w = in_w.astype(BF16)
    row = pl.BlockSpec((tm, D_MODEL), lambda i: (i, 0))
    per = tm // SUBLANES
    halo = pl.BlockSpec((SUBLANES, D_MODEL), lambda i: (jnp.maximum(i * per - 1, 0), 0))
    return pl.pallas_call(
        functools.partial(_conv_ffn_kernel, tiles_per_seq=s // tm),
        grid=(n // tm,),
        in_specs=[row, halo] + [_vmem_full()] * 11,
        out_specs=row,
        out_shape=jax.ShapeDtypeStruct((n, D_MODEL), F32),
        compiler_params=_params("parallel"),
        name="conv_ffn",
    )(x2, x2, norm_w.reshape(1, -1), w[:, :SC_WIDTH], w[:, SC_WIDTH:2 * SC_WIDTH], w[:, 2 * SC_WIDTH:],
      conv_w, out_w.astype(BF16), g.reshape(1, -1), w_gate.astype(BF16), w_up.astype(BF16), w_down.astype(BF16),
      final_norm.reshape(1, -1))


def kernel(x, l0_attn_norm, l0_in_w, l0_kv_norm, l0_w_uk, l0_w_uv, l0_conv_w, l0_conv_b, l0_dt_bias,
           l0_A_log, l0_D, l0_ssm_norm, l0_out_w, l0_ffn_norm, l0_w_gate, l0_w_up, l0_w_down,
           l1_conv_norm, l1_in_w, l1_conv_w, l1_out_w, l1_ffn_norm, l1_w_gate, l1_w_up, l1_w_down,
           final_norm):
    b, s, d = x.shape
    assert d == D_MODEL and s % TOKEN_TILE == 0 and s % CONV_TOKEN_TILE == 0 and s % KEY_CHUNK == 0 and s % SSM_CHUNK == 0
    x2 = x.reshape(b * s, d)
    qt, iqt, ikwt, ckv, ckvt, ik, z, xbc, dtp = _in_proj0(x2, l0_attn_norm, l0_in_w, l0_kv_norm)
    ya = _dsa(qt, iqt, ikwt, ik, ckv, ckvt, l0_w_uk, l0_w_uv, b, s)
    yb = _ssd(xbc, z, dtp, l0_conv_w, l0_conv_b, l0_dt_bias, l0_A_log, l0_D, l0_ssm_norm, b, s)
    x2 = _mix_ffn(x2, ya, yb.reshape(b * s, -1), l0_out_w, l0_ffn_norm, l0_w_gate, l0_w_up, l0_w_down)
    x2 = _conv_ffn(x2, l1_conv_norm, l1_in_w, l1_conv_w, l1_out_w, l1_ffn_norm, l1_w_gate, l1_w_up, l1_w_down,
                   final_norm, s)
    return x2.reshape(b, s, d)
```

```python
import functools

import jax
import jax.numpy as jnp
import numpy as np
from jax import lax
from jax.experimental import pallas as pl
from jax.experimental.pallas import tpu as pltpu

D_MODEL = 1024
N_HEADS_A = 8
HEAD_DIM_A = 64
KV_LATENT = 256
IDX_HEADS = 8
IDX_DIM = 64
TOPK_MAX = 256
Q_BLOCK = 128
SSM_D_INNER = 1024
SSM_HEADS = 16
SSM_HEAD_DIM = SSM_D_INNER // SSM_HEADS
SSM_GROUPS = 2
SSM_STATE = 128
SSM_CONV = 4
SSM_CHUNK = 128
SC_WIDTH = D_MODEL
SC_CONV = 3
D_FF = -(-8 * D_MODEL // (3 * 256)) * 256
EPS = 1e-6

LANES = 128
SUBLANES = 8
TOKEN_TILE = 512
CONV_TOKEN_TILE = 512
KEY_CHUNK = 512
KVT_PAD = 16
VMEM_LIMIT = 56 * 1024 * 1024

F32 = jnp.float32
BF16 = jnp.bfloat16
NEG_BIG = -1e30
F32_LOWEST = float(np.finfo(np.float32).min)
KEY_LOWEST = int(np.array(F32_LOWEST, np.float32).view(np.int32)) ^ 0x7FFFFFFF


def _vmem_full():
    return pl.BlockSpec(memory_space=pltpu.VMEM)


def _params(*sem):
    return pltpu.CompilerParams(dimension_semantics=sem, vmem_limit_bytes=VMEM_LIMIT)


def _rms(x, w):
    return x * lax.rsqrt(jnp.mean(x * x, axis=-1, keepdims=True) + EPS) * w


def _dot(a, b):
    return jnp.dot(a, b, preferred_element_type=F32)


def _dot_nt(a, b):
    return lax.dot_general(a, b, (((1,), (1,)), ((), ())), preferred_element_type=F32)


def _in0_kernel(x_ref, g_ref, wqt, wiqt, wikwt, wckv, wikw, wz, wxbc, wdt, kvn_ref,
                qt_o, iqt_o, ikwt_o, ckv_o, ckvt_o, ik_o, z_o, xbc_o, dt_o):
    xn = _rms(x_ref[...], g_ref[...]).astype(BF16)
    qt_o[...] = _dot_nt(wqt[...], xn).astype(BF16)
    iqt_o[...] = _dot_nt(wiqt[...], xn).astype(BF16)
    ikwt_o[...] = _dot_nt(wikwt[...], xn)
    c = _rms(_dot(xn, wckv[...]), kvn_ref[...])
    ckv_o[...] = c.astype(BF16)
    ones_rows = (lax.broadcasted_iota(jnp.int32, (KVT_PAD, c.shape[0]), 0) == 0).astype(F32)
    ckvt_o[0] = jnp.concatenate([c.T, ones_rows], axis=0).astype(BF16)
    ik_o[...] = _dot(xn, wikw[...]).astype(BF16)
    z_o[...] = _dot(xn, wz[...])
    xbc_o[...] = _dot(xn, wxbc[...])
    dt_o[...] = _dot(xn, wdt[...])


def _in_proj0(x2, g, in_w, kv_norm):
    n = x2.shape[0]
    cuts = np.cumsum((N_HEADS_A * HEAD_DIM_A, KV_LATENT, IDX_HEADS * IDX_DIM, IDX_DIM, IDX_HEADS,
                      SSM_D_INNER, SSM_D_INNER + 2 * SSM_GROUPS * SSM_STATE, SSM_HEADS))
    w = in_w.astype(BF16)
    wq, wckv, wiq = w[:, :cuts[0]], w[:, cuts[0]:cuts[1]], w[:, cuts[1]:cuts[2]]
    wikw = jnp.pad(w[:, cuts[2]:cuts[4]], ((0, 0), (0, LANES - IDX_DIM - IDX_HEADS)))
    wz, wxbc = w[:, cuts[4]:cuts[5]], w[:, cuts[5]:cuts[6]]
    wdt = jnp.pad(w[:, cuts[6]:cuts[7]], ((0, 0), (0, LANES - SSM_HEADS)))
    tm = KEY_CHUNK
    row = lambda c: pl.BlockSpec((tm, c), lambda i: (i, 0))
    col = lambda r: pl.BlockSpec((r, tm), lambda i: (0, i))
    hd = N_HEADS_A * HEAD_DIM_A
    out_specs = [col(hd), col(hd), col(LANES), row(KV_LATENT),
                 pl.BlockSpec((1, KV_LATENT + KVT_PAD, tm), lambda i: (i, 0, 0)),
                 row(LANES), row(SSM_D_INNER), row(wxbc.shape[1]), row(LANES)]
    sds = jax.ShapeDtypeStruct
    out_shape = [sds((hd, n), BF16), sds((hd, n), BF16), sds((LANES, n), F32), sds((n, KV_LATENT), BF16),
                 sds((n // tm, KV_LATENT + KVT_PAD, tm), BF16), sds((n, LANES), BF16), sds((n, SSM_D_INNER), F32),
                 sds((n, wxbc.shape[1]), F32), sds((n, LANES), F32)]
    return pl.pallas_call(
        _in0_kernel,
        grid=(n // tm,),
        in_specs=[row(D_MODEL)] + [_vmem_full()] * 10,
        out_specs=out_specs,
        out_shape=out_shape,
        compiler_params=_params("parallel"),
        name="in_proj0",
    )(x2, g.reshape(1, -1), wq.T, wiq.T, wikw.T, wckv, wikw, wz, wxbc, wdt, kv_norm.reshape(1, -1))


def _f2k(x):
    b = pltpu.bitcast(x, jnp.int32)
    return jnp.where(b < 0, b ^ jnp.int32(0x7FFFFFFF), b)


def _k2f(k):
    b = jnp.where(k < 0, k ^ jnp.int32(0x7FFFFFFF), k)
    return pltpu.bitcast(b, F32)


FOLD_ROWS = 4 * SUBLANES
VALUE_PROBES = 10
PROBES_PER_CHECK = 3
UNCHECKED_ROUNDS = 5


def _fold(x, op):
    return op(x.reshape(x.shape[0] // FOLD_ROWS, FOLD_ROWS, LANES), axis=0)


def _any_lane(flag):
    return jnp.max(jnp.where(flag, 1.0, 0.0))


def _dsa_kernel(qt_ref, iqt_ref, wt_ref, iqtn_ref, wtn_ref, ik_ref, ckv_ref, ckvt_ref, wukt_ref, wuvt_ref, tri_ref,
                o_ref,
                score_scr, qlat_scr, r_scr, acc_scr, m_scr, thr_scr, part_scr, stat_scr,
                *, topk, nblocks):
    i = pl.program_id(1)
    kc = KEY_CHUNK
    qb = Q_BLOCK
    nh = N_HEADS_A
    cur = i % 2
    nxt = 1 - cur
    inext = jnp.minimum(i + 1, nblocks - 1)
    nch = (i * qb) // kc + 1
    nch_next = (inext * qb) // kc + 1
    t0 = i * qb
    key_i = lax.broadcasted_iota(jnp.int32, (kc, LANES), 0)
    rel_i = key_i - lax.broadcasted_iota(jnp.int32, (kc, LANES), 1)

    wscale = (IDX_HEADS ** -0.5) * (IDX_DIM ** -0.5)

    def load_indexer_queries(iqt):
        r_scr[...] = jnp.zeros(r_scr.shape, BF16)
        for h in range(IDX_HEADS):
            r_scr[0:IDX_DIM, h * qb:(h + 1) * qb] = iqt[h * IDX_DIM:(h + 1) * IDX_DIM, :]

    def reset_partials():
        part_scr[0] = jnp.full((FOLD_ROWS, LANES), jnp.inf, F32)
        part_scr[1] = jnp.full((FOLD_ROWS, LANES), -jnp.inf, F32)
        part_scr[2] = jnp.zeros((FOLD_ROWS, LANES), F32)
        part_scr[3] = jnp.zeros((FOLD_ROWS, LANES), F32)

    def score_chunk(j, buf, wts, tq0):
        base = pl.multiple_of(j * kc, kc)
        lt = _dot(ik_ref[0, pl.ds(base, kc), :], r_scr[...])
        acc = None
        for h in range(IDX_HEADS):
            term = jnp.maximum(lt[:, h * qb:(h + 1) * qb], 0.0) * wts[h:h + 1, :]
            acc = term if acc is None else acc + term
        sc = jnp.where(rel_i <= (tq0 - base), acc, -jnp.inf)
        score_scr[buf, pl.ds(base, kc), :] = sc
        part_scr[0] = jnp.minimum(part_scr[0], _fold(acc, jnp.min))
        part_scr[1] = jnp.maximum(part_scr[1], _fold(acc, jnp.max))
        part_scr[2] = part_scr[2] + _fold(jnp.where(sc > 0.0, 1.0, 0.0), jnp.sum)
        part_scr[3] = part_scr[3] + _fold(jnp.where(sc >= 0.0, 1.0, 0.0), jnp.sum)

    def publish_stats():
        stat_scr[0:1, :] = jnp.min(part_scr[0], axis=0, keepdims=True)
        stat_scr[1:2, :] = jnp.max(part_scr[1], axis=0, keepdims=True)
        stat_scr[2:3, :] = jnp.sum(part_scr[2], axis=0, keepdims=True)
        stat_scr[3:4, :] = jnp.sum(part_scr[3], axis=0, keepdims=True)

    @pl.when(i == 0)
    def _():
        load_indexer_queries(iqt_ref[...])
        reset_partials()
        score_chunk(0, cur, wt_ref[...] * wscale, t0)
        publish_stats()

    colmin = stat_scr[0:1, :]
    colmax = stat_scr[1:2, :]
    npos = stat_scr[2:3, :].astype(jnp.int32)
    nnonneg = stat_scr[3:4, :].astype(jnp.int32)

    def count(pred):
        def body(j, acc):
            base = pl.multiple_of(j * kc, kc)
            blk = score_scr[cur, pl.ds(base, kc), :]
            return acc + _fold(jnp.where(pred(blk, base), 1.0, 0.0), jnp.sum)
        acc = lax.fori_loop(0, nch, body, jnp.zeros((FOLD_ROWS, LANES), F32))
        return jnp.sum(acc, axis=0, keepdims=True).astype(jnp.int32)

    t_pos = t0 + lax.broadcasted_iota(jnp.int32, (1, qb), 1)
    nvalid = t_pos + 1
    allsel = nvalid <= topk
    key_zero, key_tiny, key_negzero = 0, 1, -1
    at_zero = jnp.logical_and(npos < topk, nnonneg >= topk)
    above = npos >= topk
    below = nnonneg < topk
    lo0 = jnp.where(at_zero, key_zero, jnp.where(above, key_tiny, _f2k(colmin)))
    cnt0 = jnp.where(at_zero, nnonneg, jnp.where(above, npos, nvalid))
    hi0 = jnp.where(at_zero, key_zero + 1, jnp.where(below, key_negzero, _f2k(colmax) + 1))
    lo0 = jnp.where(allsel, jnp.int32(KEY_LOWEST), lo0)
    hi0 = jnp.where(allsel, jnp.int32(KEY_LOWEST + 1), hi0)
    cnt0 = jnp.where(allsel, jnp.int32(topk), cnt0)
    go0 = _any_lane(lo0 + 1 < hi0)
    cnt_hi0 = jnp.where(below, nnonneg, 0)

    def search_step(lo, hi, cnt_lo, cnt_hi, by_value):
        active = lo + 1 < hi
        mid_k = (lo & hi) + ((lo ^ hi) >> 1)
        mid_v = _f2k(0.5 * _k2f(lo) + 0.5 * _k2f(hi - 1))
        mid_v = jnp.minimum(jnp.maximum(mid_v, lo + 1), hi - 1)
        mid = jnp.where(by_value, mid_v, mid_k)
        cand = _k2f(mid)
        cnt = count(lambda blk, base: blk >= cand)
        ge = cnt >= topk
        up = jnp.logical_and(active, ge)
        dn = jnp.logical_and(active, jnp.logical_not(ge))
        hit = jnp.logical_and(active, cnt == topk)
        lo = jnp.where(up, mid, lo)
        hi = jnp.where(hit, mid + 1, jnp.where(dn, mid, hi))
        cnt_lo = jnp.where(up, cnt, cnt_lo)
        cnt_hi = jnp.where(dn, cnt, cnt_hi)
        return lo, hi, cnt_lo, cnt_hi

    def resolve_small(lo, hi, cnt_lo, cnt_hi):
        active = lo + 1 < hi
        v_below, v_hi = _k2f(lo - 1), _k2f(hi)

        def body(j, c):
            mn, mx = c
            base = pl.multiple_of(j * kc, kc)
            blk = score_scr[cur, pl.ds(base, kc), :]
            mn = jnp.minimum(mn, _fold(jnp.where(blk > v_below, blk, jnp.inf), jnp.min))
            mx = jnp.maximum(mx, _fold(jnp.where(blk < v_hi, blk, -jnp.inf), jnp.max))
            return mn, mx
        mn, mx = lax.fori_loop(0, nch, body, (jnp.full((FOLD_ROWS, LANES), jnp.inf, F32),
                                              jnp.full((FOLD_ROWS, LANES), -jnp.inf, F32)))
        mn = jnp.min(mn, axis=0, keepdims=True)
        mx = jnp.max(mx, axis=0, keepdims=True)
        inside = cnt_lo - cnt_hi
        small = jnp.logical_and(active, inside <= 2)
        want_top = (topk - cnt_hi) == 1
        twins = jnp.logical_and(mx == mn, inside == 2)
        key_t = _f2k(jnp.where(want_top, mx, mn))
        cnt_t = jnp.where(want_top, cnt_hi + 1 + jnp.where(twins, 1, 0), cnt_lo)
        return (jnp.where(small, key_t, lo), jnp.where(small, key_t + 1, hi), jnp.where(small, cnt_t, cnt_lo), cnt_hi)

    def bis_cond(c):
        return c[5] > 0.0

    def probe_round(r, c):
        for u in range(PROBES_PER_CHECK):
            c = search_step(*c, r * PROBES_PER_CHECK + u < VALUE_PROBES)
        return c

    def bis_body(c):
        r = c[4]
        st = resolve_small(*probe_round(r, c[:4]))
        return (*st, r + 1, _any_lane(st[0] + 1 < st[1]))

    first = jnp.where(go0 > 0.0, UNCHECKED_ROUNDS, 0)
    st = resolve_small(*lax.fori_loop(0, first, probe_round, (lo0, hi0, cnt0, cnt_hi0)))
    lo, _, cnt_lo, _, _, _ = lax.while_loop(bis_cond, bis_body, (*st, first, _any_lane(st[0] + 1 < st[1])))
    thr_scr[...] = _k2f(lo)
    tied = cnt_lo > topk

    @pl.when(_any_lane(tied) > 0.0)
    def _():
        thr = thr_scr[...]
        need = (topk - count(lambda blk, base: blk > thr)).astype(F32)

        def fix(j, seen):
            base = pl.multiple_of(j * kc, kc)
            blk = score_scr[cur, pl.ds(base, kc), :]
            eq = blk == thr
            rank = _dot(tri_ref[...], jnp.where(eq, 1.0, 0.0).astype(BF16)) + seen
            score_scr[cur, pl.ds(base, kc), :] = jnp.where(eq, jnp.where(rank > need, -jnp.inf, blk), blk)
            return rank[kc - 1:kc, :]
        lax.fori_loop(0, nch, fix, jnp.zeros((1, qb), F32))

    qt = qt_ref[...]
    qscale = (HEAD_DIM_A ** -0.5) * float(np.log2(np.e))
    for p in range(nh // 2):
        r = _dot(wukt_ref[p], qt[p * LANES:(p + 1) * LANES, :])
        for v in range(2):
            h = 2 * p + v
            qlat_scr[:, h * qb:(h + 1) * qb] = (r[v * KV_LATENT:(v + 1) * KV_LATENT, :] * qscale).astype(BF16)
    load_indexer_queries(iqtn_ref[...])
    wts_next = wtn_ref[...] * wscale
    tn0 = inext * qb
    reset_partials()

    m_scr[...] = jnp.full(m_scr.shape, NEG_BIG, F32)
    acc_scr[...] = jnp.zeros(acc_scr.shape, F32)

    def sweep(j, _):
        base = pl.multiple_of(j * kc, kc)
        kv = ckv_ref[0, pl.ds(base, kc), :]
        kvt = ckvt_ref[0, j]
        bias = jnp.where(score_scr[cur, pl.ds(base, kc), :] >= thr_scr[...], 0.0, NEG_BIG)
        s = _dot(kv, qlat_scr[...])
        score_chunk(j, nxt, wts_next, tn0)
        ps, alphas = [], []
        for h in range(nh):
            sh = s[:, h * qb:(h + 1) * qb] + bias
            m_old = m_scr[h:h + 1, :]
            m_new = jnp.maximum(m_old, jnp.max(sh, axis=0, keepdims=True))
            ps.append(jnp.exp2(sh - m_new).astype(BF16))
            alphas.append(jnp.exp2(m_old - m_new))
            m_scr[h:h + 1, :] = m_new
        pv = _dot(kvt, jnp.concatenate(ps, axis=1))
        acc_scr[...] = acc_scr[...] * jnp.concatenate(alphas, axis=1) + pv
        return 0

    lax.fori_loop(0, nch, sweep, 0)

    @pl.when(nch_next > nch)
    def _():
        score_chunk(nch, nxt, wts_next, tn0)

    publish_stats()

    inv_l = 1.0 / acc_scr[KV_LATENT:KV_LATENT + 1, :]
    outs = []
    for h in range(nh):
        cols = slice(h * qb, (h + 1) * qb)
        olat = (acc_scr[0:KV_LATENT, cols] * inv_l[:, cols]).astype(BF16)
        outs.append(_dot(wuvt_ref[h], olat))
    o_ref[...] = jnp.concatenate(outs, axis=0).T.astype(BF16)


def _dsa(qt, iqt, ikwt, ik, ckv, ckvt, w_uk, w_uv, b, s):
    kc = KEY_CHUNK
    nb = s // Q_BLOCK
    nchunks = s // kc
    topk = min(TOPK_MAX, s // 4)
    hd = N_HEADS_A * HEAD_DIM_A
    wukt = jnp.swapaxes(w_uk, 1, 2).reshape(N_HEADS_A // 2, 2, KV_LATENT, HEAD_DIM_A)
    eye2 = jnp.eye(2, dtype=F32)
    wukt = jnp.einsum('pvcd,vu->pvcud', wukt, eye2).reshape(N_HEADS_A // 2, 2 * KV_LATENT, 2 * HEAD_DIM_A)
    wuvt = jnp.swapaxes(w_uv, 1, 2)
    this_blk = lambda bi, i: bi * nb + i
    next_blk = lambda bi, i: bi * nb + jnp.minimum(i + 1, nb - 1)
    qcol = lambda r, blk: pl.BlockSpec((r, Q_BLOCK), lambda bi, i: (0, blk(bi, i)))
    wrow = lambda blk: pl.BlockSpec((IDX_HEADS, Q_BLOCK), lambda bi, i: (IDX_DIM // IDX_HEADS, blk(bi, i)))
    kern = functools.partial(_dsa_kernel, topk=topk, nblocks=nb)
    return pl.pallas_call(
        kern,
        grid=(b, nb),
        in_specs=[qcol(hd, this_blk), qcol(IDX_HEADS * IDX_DIM, this_blk), wrow(this_blk),
                  qcol(IDX_HEADS * IDX_DIM, next_blk), wrow(next_blk),
                  pl.BlockSpec((1, s, LANES), lambda bi, i: (bi, 0, 0)),
                  pl.BlockSpec((1, s, KV_LATENT), lambda bi, i: (bi, 0, 0)),
                  pl.BlockSpec((1, nchunks, KV_LATENT + KVT_PAD, kc), lambda bi, i: (bi, 0, 0, 0)),
                  _vmem_full(), _vmem_full(), _vmem_full()],
        out_specs=pl.BlockSpec((Q_BLOCK, hd), lambda bi, i: (bi * nb + i, 0)),
        out_shape=jax.ShapeDtypeStruct((b * s, hd), BF16),
        scratch_shapes=[
            pltpu.VMEM((2, s, Q_BLOCK), F32),
            pltpu.VMEM((KV_LATENT, N_HEADS_A * Q_BLOCK), BF16),
            pltpu.VMEM((LANES, IDX_HEADS * Q_BLOCK), BF16),
            pltpu.VMEM((KV_LATENT + KVT_PAD, N_HEADS_A * Q_BLOCK), F32),
            pltpu.VMEM((N_HEADS_A, Q_BLOCK), F32),
            pltpu.VMEM((1, Q_BLOCK), F32),
            pltpu.VMEM((4, FOLD_ROWS, Q_BLOCK), F32),
            pltpu.VMEM((SUBLANES, Q_BLOCK), F32),
        ],
        compiler_params=_params("parallel", "arbitrary"),
        name="dsa_attention",
    )(qt, iqt, ikwt, iqt, ikwt, ik.reshape(b, s, LANES), ckv.reshape(b, s, KV_LATENT),
      ckvt.reshape(b, nchunks, KV_LATENT + KVT_PAD, kc), wukt.astype(BF16), wuvt.astype(BF16),
      jnp.tril(jnp.ones((kc, kc), BF16)))


def _causal_conv(x, w, halo):
    taps = w.shape[0]
    top_row = lax.broadcasted_iota(jnp.int32, halo.shape, 0)
    y = x * w[taps - 1:taps, :]
    for k in range(1, taps):
        xk = pltpu.roll(x, k, 0)
        top = jnp.where(top_row < k, pltpu.roll(halo, k, 0), xk[:SUBLANES, :])
        xk = jnp.concatenate([top, xk[SUBLANES:, :]], axis=0)
        y = y + xk * w[taps - 1 - k:taps - k, :]
    return y


def _split3(x):
    p1 = x.astype(BF16)
    r = x - p1.astype(F32)
    p2 = r.astype(BF16)
    return p1, p2, (r - p2.astype(F32)).astype(BF16)


def _ssd_kernel(xbc_ref, z_ref, dt_ref, cw_ref, cb_ref, dtb_ref, alog_ref, dexp_ref, nw_ref, e_ref,
                o_ref, halo_scr, state_scr):
    c = pl.program_id(1)
    L = SSM_CHUNK
    n = SSM_STATE
    gw = SSM_D_INNER // SSM_GROUPS

    @pl.when(c == 0)
    def _():
        halo_scr[...] = jnp.zeros(halo_scr.shape, F32)
        state_scr[...] = jnp.zeros(state_scr.shape, F32)

    xbc = xbc_ref[0]
    conv = _causal_conv(xbc, cw_ref[...], halo_scr[...]) + cb_ref[...]
    halo_scr[...] = xbc[L - SUBLANES:, :]
    act = conv * jax.nn.sigmoid(conv)
    xs = act[:, :SSM_D_INNER]
    bm = act[:, SSM_D_INNER:SSM_D_INNER + SSM_GROUPS * n]
    cm = act[:, SSM_D_INNER + SSM_GROUPS * n:]

    dt = jax.nn.softplus(dt_ref[0] + dtb_ref[...])
    a = dt * (-jnp.exp(alog_ref[...]))
    ri = lax.broadcasted_iota(jnp.int32, (L, L), 0)
    ci = lax.broadcasted_iota(jnp.int32, (L, L), 1)
    tri = ri >= ci
    tri_b = jnp.where(tri, 1.0, 0.0).astype(BF16)
    cs3 = _dot(tri_b, jnp.concatenate(_split3(a), axis=1))
    a_cs = cs3[:, :LANES] + cs3[:, LANES:2 * LANES] + cs3[:, 2 * LANES:]
    a_cs_t = a_cs.T
    w_end = dt * jnp.exp(a_cs[L - 1:L, :] - a_cs)
    grow_c = jnp.exp(a_cs)
    pieces = [p for arr in (dt, w_end, grow_c) for p in _split3(arr)]
    ex = _dot(jnp.concatenate(pieces, axis=0), e_ref[...])
    dt_e, wend_e, grow = (ex[3 * i * L:(3 * i + 1) * L] + ex[(3 * i + 1) * L:(3 * i + 2) * L]
                          + ex[(3 * i + 2) * L:(3 * i + 3) * L] for i in range(3))
    xdt_b = (xs * dt_e).astype(BF16)
    xend_b = (xs * wend_e).astype(BF16)
    lane = lax.broadcasted_iota(jnp.int32, (L, LANES), 1)

    y_parts = []
    for g in range(SSM_GROUPS):
        bg = bm[:, g * n:(g + 1) * n]
        cg = cm[:, g * n:(g + 1) * n].astype(BF16)
        cbm = _dot_nt(cg, bg.astype(BF16))
        for pr in range(gw // LANES):
            col = g * gw + pr * LANES
            xpair = xdt_b[:, col:col + LANES]
            outs = []
            for v in range(2):
                h = (col // SSM_HEAD_DIM) + v
                seg = a_cs[:, h:h + 1] - a_cs_t[h:h + 1, :]
                dec = jnp.exp(jnp.where(tri, seg, -jnp.inf))
                outs.append(_dot((cbm * dec).astype(BF16), xpair))
            y_parts.append(jnp.where(lane < SSM_HEAD_DIM, outs[0], outs[1]))
    y = jnp.concatenate(y_parts, axis=1)

    offs = []
    for g in range(SSM_GROUPS):
        sl = slice(g * gw, (g + 1) * gw)
        bg_t = bm[:, g * n:(g + 1) * n].T.astype(BF16)
        cg = cm[:, g * n:(g + 1) * n].astype(BF16)
        st = state_scr[g]
        offs.append(_dot(cg, st.astype(BF16)))
        state_scr[g] = st * grow[L - 1:L, sl] + _dot(bg_t, xend_b[:, sl])
    y = y + jnp.concatenate(offs, axis=1) * grow + xs * dexp_ref[...]

    z = z_ref[0]
    y = y * (z * jax.nn.sigmoid(z))
    nw = nw_ref[...]
    outs = []
    for g in range(SSM_GROUPS):
        sl = slice(g * gw, (g + 1) * gw)
        outs.append(_rms(y[:, sl], nw[:, sl]))
    o_ref[0] = jnp.concatenate(outs, axis=1).astype(BF16)


def _ssd(xbc, z, dtp, conv_w, conv_b, dt_bias, a_log, d, norm_w, b, s):
    L = SSM_CHUNK
    cx = SSM_D_INNER + 2 * SSM_GROUPS * SSM_STATE
    pad = LANES - SSM_HEADS
    expand = jnp.repeat(jnp.eye(SSM_HEADS, dtype=F32), SSM_HEAD_DIM, axis=1)
    expand = jnp.pad(expand, ((0, pad), (0, 0)))
    blk = lambda c: pl.BlockSpec((1, L, c), lambda bi, i: (bi, i, 0))
    return pl.pallas_call(
        _ssd_kernel,
        grid=(b, s // L),
        in_specs=[blk(cx), blk(SSM_D_INNER), blk(LANES)] + [_vmem_full()] * 7,
        out_specs=blk(SSM_D_INNER),
        out_shape=jax.ShapeDtypeStruct((b, s, SSM_D_INNER), BF16),
        scratch_shapes=[pltpu.VMEM((SUBLANES, cx), F32),
                        pltpu.VMEM((SSM_GROUPS, SSM_STATE, SSM_D_INNER // SSM_GROUPS), F32)],
        compiler_params=_params("parallel", "arbitrary"),
        name="ssd_mixer",
    )(xbc.reshape(b, s, cx), z.reshape(b, s, SSM_D_INNER), dtp.reshape(b, s, LANES),
      conv_w, conv_b.reshape(1, -1), jnp.pad(dt_bias, (0, pad)).reshape(1, -1),
      jnp.pad(a_log, (0, pad)).reshape(1, -1), jnp.repeat(d, SSM_HEAD_DIM).reshape(1, -1),
      norm_w.reshape(1, -1), expand.astype(BF16))


def _swiglu(x, g, wg, wu, wd):
    hn = _rms(x, g).astype(BF16)
    a = _dot(hn, wg[...])
    u = _dot(hn, wu[...])
    act = (a * jax.nn.sigmoid(a) * u).astype(BF16)
    return x + _dot(act, wd[...])


def _mix_ffn_kernel(x_ref, ya_ref, yb_ref, wa, wb, g_ref, wg, wu, wd, o_ref):
    x = x_ref[...] + _dot(ya_ref[...], wa[...]) + _dot(yb_ref[...], wb[...])
    o_ref[...] = _swiglu(x, g_ref[...], wg, wu, wd)


def _mix_ffn(x2, ya, yb, out_w, g, w_gate, w_up, w_down):
    n = x2.shape[0]
    tm = TOKEN_TILE
    ca = N_HEADS_A * HEAD_DIM_A
    w = out_w.astype(BF16)
    row = lambda c: pl.BlockSpec((tm, c), lambda i: (i, 0))
    return pl.pallas_call(
        _mix_ffn_kernel,
        grid=(n // tm,),
        in_specs=[row(D_MODEL), row(ca), row(SSM_D_INNER)] + [_vmem_full()] * 6,
        out_specs=row(D_MODEL),
        out_shape=jax.ShapeDtypeStruct((n, D_MODEL), F32),
        compiler_params=_params("parallel"),
        name="mix_ffn",
    )(x2, ya, yb, w[:ca], w[ca:], g.reshape(1, -1), w_gate.astype(BF16), w_up.astype(BF16), w_down.astype(BF16))


def _conv_ffn_kernel(x_ref, xh_ref, gc_ref, wb, wc, wv, cw_ref, wo, g_ref, wg, wu, wd, fn_ref, o_ref,
                     *, tiles_per_seq):
    i = pl.program_id(0)
    x = x_ref[...]
    xn = _rms(x, gc_ref[...]).astype(BF16)
    u = _dot(xn, wc[...]) * _dot(xn, wv[...])
    xhn = _rms(xh_ref[...], gc_ref[...]).astype(BF16)
    halo = jnp.where(i % tiles_per_seq == 0, 0.0, _dot(xhn, wc[...]) * _dot(xhn, wv[...]))
    y = (_dot(xn, wb[...]) * _causal_conv(u, cw_ref[...], halo)).astype(BF16)
    x = x + _dot(y, wo[...])
    o_ref[...] = _rms(_swiglu(x, g_ref[...], wg, wu, wd), fn_ref[...])


def _conv_ffn(x2, norm_w, in_w, conv_w, out_w, g, w_gate, w_up, w_down, final_norm, s):
    n = x2.shape[0]
    tm = CONV_TOKEN_TILE
    w = in_w.astype(BF16)
    row = pl.BlockSpec((tm, D_MODEL), lambda i: (i, 0))
    per = tm // SUBLANES
    halo = pl.BlockSpec((SUBLANES, D_MODEL), lambda i: (jnp.maximum(i * per - 1, 0), 0))
    return pl.pallas_call(
        functools.partial(_conv_ffn_kernel, tiles_per_seq=s // tm),
        grid=(n // tm,),
        in_specs=[row, halo] + [_vmem_full()] * 11,
        out_specs=row,
        out_shape=jax.ShapeDtypeStruct((n, D_MODEL), F32),
        compiler_params=_params("parallel"),
        name="conv_ffn",
    )(x2, x2, norm_w.reshape(1, -1), w[:, :SC_WIDTH], w[:, SC_WIDTH:2 * SC_WIDTH], w[:, 2 * SC_WIDTH:],
      conv_w, out_w.astype(BF16), g.reshape(1, -1), w_gate.astype(BF16), w_up.astype(BF16), w_down.astype(BF16),
      final_norm.reshape(1, -1))


def kernel(x, l0_attn_norm, l0_in_w, l0_kv_norm, l0_w_uk, l0_w_uv, l0_conv_w, l0_conv_b, l0_dt_bias,
           l0_A_log, l0_D, l0_ssm_norm, l0_out_w, l0_ffn_norm, l0_w_gate, l0_w_up, l0_w_down,
           l1_conv_norm, l1_in_w, l1_conv_w, l1_out_w, l1_ffn_norm, l1_w_gate, l1_w_up, l1_w_down,
           final_norm):
    b, s, d = x.shape
    assert d == D_MODEL and s % TOKEN_TILE == 0 and s % CONV_TOKEN_TILE == 0 and s % KEY_CHUNK == 0 and s % SSM_CHUNK == 0
    x2 = x.reshape(b * s, d)
    qt, iqt, ikwt, ckv, ckvt, ik, z, xbc, dtp = _in_proj0(x2, l0_attn_norm, l0_in_w, l0_kv_norm)
    ya = _dsa(qt, iqt, ikwt, ik, ckv, ckvt, l0_w_uk, l0_w_uv, b, s)
    yb = _ssd(xbc, z, dtp, l0_conv_w, l0_conv_b, l0_dt_bias, l0_A_log, l0_D, l0_ssm_norm, b, s)
    x2 = _mix_ffn(x2, ya, yb.reshape(b * s, -1), l0_out_w, l0_ffn_norm, l0_w_gate, l0_w_up, l0_w_down)
    x2 = _conv_ffn(x2, l1_conv_norm, l1_in_w, l1_conv_w, l1_out_w, l1_ffn_norm, l1_w_gate, l1_w_up, l1_w_down,
                   final_norm, s)
    return x2.reshape(b, s, d)
```

```python
import functools

import jax
import jax.numpy as jnp
import numpy as np
from jax import lax
from jax.experimental import pallas as pl
from jax.experimental.pallas import tpu as pltpu

D_MODEL = 1024
N_HEADS_A = 8
HEAD_DIM_A = 64
KV_LATENT = 256
IDX_HEADS = 8
IDX_DIM = 64
TOPK_MAX = 256
Q_BLOCK = 128
SSM_D_INNER = 1024
SSM_HEADS = 16
SSM_HEAD_DIM = SSM_D_INNER // SSM_HEADS
SSM_GROUPS = 2
SSM_STATE = 128
SSM_CONV = 4
SSM_CHUNK = 128
SC_WIDTH = D_MODEL
SC_CONV = 3
D_FF = -(-8 * D_MODEL // (3 * 256)) * 256
EPS = 1e-6

LANES = 128
SUBLANES = 8
TOKEN_TILE = 512
CONV_TOKEN_TILE = 512
KEY_CHUNK = 512
KVT_PAD = 16
VMEM_LIMIT = 56 * 1024 * 1024

F32 = jnp.float32
BF16 = jnp.bfloat16
NEG_BIG = -1e30
F32_LOWEST = float(np.finfo(np.float32).min)
KEY_LOWEST = int(np.array(F32_LOWEST, np.float32).view(np.int32)) ^ 0x7FFFFFFF


def _vmem_full():
    return pl.BlockSpec(memory_space=pltpu.VMEM)


def _params(*sem):
    return pltpu.CompilerParams(dimension_semantics=sem, vmem_limit_bytes=VMEM_LIMIT)


def _rms(x, w):
    return x * lax.rsqrt(jnp.mean(x * x, axis=-1, keepdims=True) + EPS) * w


def _dot(a, b):
    return jnp.dot(a, b, preferred_element_type=F32)


def _dot_nt(a, b):
    return lax.dot_general(a, b, (((1,), (1,)), ((), ())), preferred_element_type=F32)


def _in0_kernel(x_ref, g_ref, wqt, wiqt, wikwt, wckv, wikw, wz, wxbc, wdt, kvn_ref,
                qt_o, iqt_o, ikwt_o, ckv_o, ckvt_o, ik_o, z_o, xbc_o, dt_o):
    xn = _rms(x_ref[...], g_ref[...]).astype(BF16)
    qt_o[...] = _dot_nt(wqt[...], xn).astype(BF16)
    iqt_o[...] = _dot_nt(wiqt[...], xn).astype(BF16)
    ikwt_o[...] = _dot_nt(wikwt[...], xn)
    c = _rms(_dot(xn, wckv[...]), kvn_ref[...])
    ckv_o[...] = c.astype(BF16)
    ones_rows = (lax.broadcasted_iota(jnp.int32, (KVT_PAD, c.shape[0]), 0) == 0).astype(F32)
    ckvt_o[0] = jnp.concatenate([c.T, ones_rows], axis=0).astype(BF16)
    ik_o[...] = _dot(xn, wikw[...]).astype(BF16)
    z_o[...] = _dot(xn, wz[...])
    xbc_o[...] = _dot(xn, wxbc[...])
    dt_o[...] = _dot(xn, wdt[...])


def _in_proj0(x2, g, in_w, kv_norm):
    n = x2.shape[0]
    cuts = np.cumsum((N_HEADS_A * HEAD_DIM_A, KV_LATENT, IDX_HEADS * IDX_DIM, IDX_DIM, IDX_HEADS,
                      SSM_D_INNER, SSM_D_INNER + 2 * SSM_GROUPS * SSM_STATE, SSM_HEADS))
    w = in_w.astype(BF16)
    wq, wckv, wiq = w[:, :cuts[0]], w[:, cuts[0]:cuts[1]], w[:, cuts[1]:cuts[2]]
    wikw = jnp.pad(w[:, cuts[2]:cuts[4]], ((0, 0), (0, LANES - IDX_DIM - IDX_HEADS)))
    wz, wxbc = w[:, cuts[4]:cuts[5]], w[:, cuts[5]:cuts[6]]
    wdt = jnp.pad(w[:, cuts[6]:cuts[7]], ((0, 0), (0, LANES - SSM_HEADS)))
    tm = KEY_CHUNK
    row = lambda c: pl.BlockSpec((tm, c), lambda i: (i, 0))
    col = lambda r: pl.BlockSpec((r, tm), lambda i: (0, i))
    hd = N_HEADS_A * HEAD_DIM_A
    out_specs = [col(hd), col(hd), col(LANES), row(KV_LATENT),
                 pl.BlockSpec((1, KV_LATENT + KVT_PAD, tm), lambda i: (i, 0, 0)),
                 row(LANES), row(SSM_D_INNER), row(wxbc.shape[1]), row(LANES)]
    sds = jax.ShapeDtypeStruct
    out_shape = [sds((hd, n), BF16), sds((hd, n), BF16), sds((LANES, n), F32), sds((n, KV_LATENT), BF16),
                 sds((n // tm, KV_LATENT + KVT_PAD, tm), BF16), sds((n, LANES), BF16), sds((n, SSM_D_INNER), F32),
                 sds((n, wxbc.shape[1]), F32), sds((n, LANES), F32)]
    return pl.pallas_call(
        _in0_kernel,
        grid=(n // tm,),
        in_specs=[row(D_MODEL)] + [_vmem_full()] * 10,
        out_specs=out_specs,
        out_shape=out_shape,
        compiler_params=_params("parallel"),
        name="in_proj0",
    )(x2, g.reshape(1, -1), wq.T, wiq.T, wikw.T, wckv, wikw, wz, wxbc, wdt, kv_norm.reshape(1, -1))


def _f2k(x):
    b = pltpu.bitcast(x, jnp.int32)
    return jnp.where(b < 0, b ^ jnp.int32(0x7FFFFFFF), b)


def _k2f(k):
    b = jnp.where(k < 0, k ^ jnp.int32(0x7FFFFFFF), k)
    return pltpu.bitcast(b, F32)


FOLD_ROWS = 4 * SUBLANES
VALUE_PROBES = 12
PROBES_PER_CHECK = 3
UNCHECKED_ROUNDS = 5


def _fold(x, op):
    return op(x.reshape(x.shape[0] // FOLD_ROWS, FOLD_ROWS, LANES), axis=0)


def _any_lane(flag):
    return jnp.max(jnp.where(flag, 1.0, 0.0))


def _dsa_kernel(qt_ref, iqt_ref, wt_ref, iqtn_ref, wtn_ref, ik_ref, ckv_ref, ckvt_ref, wukt_ref, wuvt_ref, tri_ref,
                o_ref,
                score_scr, qlat_scr, r_scr, acc_scr, m_scr, thr_scr, part_scr, stat_scr,
                *, topk, nblocks):
    i = pl.program_id(1)
    kc = KEY_CHUNK
    qb = Q_BLOCK
    nh = N_HEADS_A
    cur = i % 2
    nxt = 1 - cur
    inext = jnp.minimum(i + 1, nblocks - 1)
    nch = (i * qb) // kc + 1
    nch_next = (inext * qb) // kc + 1
    t0 = i * qb
    key_i = lax.broadcasted_iota(jnp.int32, (kc, LANES), 0)
    rel_i = key_i - lax.broadcasted_iota(jnp.int32, (kc, LANES), 1)

    wscale = (IDX_HEADS ** -0.5) * (IDX_DIM ** -0.5)

    def load_indexer_queries(iqt):
        r_scr[...] = jnp.zeros(r_scr.shape, BF16)
        for h in range(IDX_HEADS):
            r_scr[0:IDX_DIM, h * qb:(h + 1) * qb] = iqt[h * IDX_DIM:(h + 1) * IDX_DIM, :]

    def reset_partials():
        part_scr[0] = jnp.full((FOLD_ROWS, LANES), jnp.inf, F32)
        part_scr[1] = jnp.full((FOLD_ROWS, LANES), -jnp.inf, F32)
        part_scr[2] = jnp.zeros((FOLD_ROWS, LANES), F32)
        part_scr[3] = jnp.zeros((FOLD_ROWS, LANES), F32)

    def score_chunk(j, buf, wts, tq0):
        base = pl.multiple_of(j * kc, kc)
        lt = _dot(ik_ref[0, pl.ds(base, kc), :], r_scr[...])
        acc = None
        for h in range(IDX_HEADS):
            term = jnp.maximum(lt[:, h * qb:(h + 1) * qb], 0.0) * wts[h:h + 1, :]
            acc = term if acc is None else acc + term
        sc = jnp.where(rel_i <= (tq0 - base), acc, -jnp.inf)
        score_scr[buf, pl.ds(base, kc), :] = sc
        part_scr[0] = jnp.minimum(part_scr[0], _fold(acc, jnp.min))
        part_scr[1] = jnp.maximum(part_scr[1], _fold(acc, jnp.max))
        part_scr[2] = part_scr[2] + _fold(jnp.where(sc > 0.0, 1.0, 0.0), jnp.sum)
        part_scr[3] = part_scr[3] + _fold(jnp.where(sc >= 0.0, 1.0, 0.0), jnp.sum)

    def publish_stats():
        stat_scr[0:1, :] = jnp.min(part_scr[0], axis=0, keepdims=True)
        stat_scr[1:2, :] = jnp.max(part_scr[1], axis=0, keepdims=True)
        stat_scr[2:3, :] = jnp.sum(part_scr[2], axis=0, keepdims=True)
        stat_scr[3:4, :] = jnp.sum(part_scr[3], axis=0, keepdims=True)

    @pl.when(i == 0)
    def _():
        load_indexer_queries(iqt_ref[...])
        reset_partials()
        score_chunk(0, cur, wt_ref[...] * wscale, t0)
        publish_stats()

    colmin = stat_scr[0:1, :]
    colmax = stat_scr[1:2, :]
    npos = stat_scr[2:3, :].astype(jnp.int32)
    nnonneg = stat_scr[3:4, :].astype(jnp.int32)

    def count(pred):
        def body(j, acc):
            base = pl.multiple_of(j * kc, kc)
            blk = score_scr[cur, pl.ds(base, kc), :]
            return acc + _fold(jnp.where(pred(blk, base), 1.0, 0.0), jnp.sum)
        acc = lax.fori_loop(0, nch, body, jnp.zeros((FOLD_ROWS, LANES), F32))
        return jnp.sum(acc, axis=0, keepdims=True).astype(jnp.int32)

    t_pos = t0 + lax.broadcasted_iota(jnp.int32, (1, qb), 1)
    nvalid = t_pos + 1
    allsel = nvalid <= topk
    key_zero, key_tiny, key_negzero = 0, 1, -1
    at_zero = jnp.logical_and(npos < topk, nnonneg >= topk)
    above = npos >= topk
    below = nnonneg < topk
    lo0 = jnp.where(at_zero, key_zero, jnp.where(above, key_tiny, _f2k(colmin)))
    cnt0 = jnp.where(at_zero, nnonneg, jnp.where(above, npos, nvalid))
    hi0 = jnp.where(at_zero, key_zero + 1, jnp.where(below, key_negzero, _f2k(colmax) + 1))
    lo0 = jnp.where(allsel, jnp.int32(KEY_LOWEST), lo0)
    hi0 = jnp.where(allsel, jnp.int32(KEY_LOWEST + 1), hi0)
    cnt0 = jnp.where(allsel, jnp.int32(topk), cnt0)
    go0 = _any_lane(lo0 + 1 < hi0)
    cnt_hi0 = jnp.where(below, nnonneg, 0)

    def search_step(lo, hi, cnt_lo, cnt_hi, by_value):
        active = lo + 1 < hi
        mid_k = (lo & hi) + ((lo ^ hi) >> 1)
        mid_v = _f2k(0.5 * _k2f(lo) + 0.5 * _k2f(hi - 1))
        mid_v = jnp.minimum(jnp.maximum(mid_v, lo + 1), hi - 1)
        mid = jnp.where(by_value, mid_v, mid_k)
        cand = _k2f(mid)
        cnt = count(lambda blk, base: blk >= cand)
        ge = cnt >= topk
        up = jnp.logical_and(active, ge)
        dn = jnp.logical_and(active, jnp.logical_not(ge))
        hit = jnp.logical_and(active, cnt == topk)
        lo = jnp.where(up, mid, lo)
        hi = jnp.where(hit, mid + 1, jnp.where(dn, mid, hi))
        cnt_lo = jnp.where(up, cnt, cnt_lo)
        cnt_hi = jnp.where(dn, cnt, cnt_hi)
        return lo, hi, cnt_lo, cnt_hi

    def resolve_small(lo, hi, cnt_lo, cnt_hi):
        active = lo + 1 < hi
        v_below, v_hi = _k2f(lo - 1), _k2f(hi)

        def body(j, c):
            mn, mx = c
            base = pl.multiple_of(j * kc, kc)
            blk = score_scr[cur, pl.ds(base, kc), :]
            mn = jnp.minimum(mn, _fold(jnp.where(blk > v_below, blk, jnp.inf), jnp.min))
            mx = jnp.maximum(mx, _fold(jnp.where(blk < v_hi, blk, -jnp.inf), jnp.max))
            return mn, mx
        mn, mx = lax.fori_loop(0, nch, body, (jnp.full((FOLD_ROWS, LANES), jnp.inf, F32),
                                              jnp.full((FOLD_ROWS, LANES), -jnp.inf, F32)))
        mn = jnp.min(mn, axis=0, keepdims=True)
        mx = jnp.max(mx, axis=0, keepdims=True)
        inside = cnt_lo - cnt_hi
        small = jnp.logical_and(active, inside <= 2)
        want_top = (topk - cnt_hi) == 1
        twins = jnp.logical_and(mx == mn, inside == 2)
        key_t = _f2k(jnp.where(want_top, mx, mn))
        cnt_t = jnp.where(want_top, cnt_hi + 1 + jnp.where(twins, 1, 0), cnt_lo)
        return (jnp.where(small, key_t, lo), jnp.where(small, key_t + 1, hi), jnp.where(small, cnt_t, cnt_lo), cnt_hi)

    def bis_cond(c):
        return c[5] > 0.0

    def probe_round(r, c):
        for u in range(PROBES_PER_CHECK):
            c = search_step(*c, r * PROBES_PER_CHECK + u < VALUE_PROBES)
        return c

    def bis_body(c):
        r = c[4]
        st = resolve_small(*probe_round(r, c[:4]))
        return (*st, r + 1, _any_lane(st[0] + 1 < st[1]))

    first = jnp.where(go0 > 0.0, UNCHECKED_ROUNDS, 0)
    st = resolve_small(*lax.fori_loop(0, first, probe_round, (lo0, hi0, cnt0, cnt_hi0)))
    lo, _, cnt_lo, _, _, _ = lax.while_loop(bis_cond, bis_body, (*st, first, _any_lane(st[0] + 1 < st[1])))
    thr_scr[...] = _k2f(lo)
    tied = cnt_lo > topk

    @pl.when(_any_lane(tied) > 0.0)
    def _():
        thr = thr_scr[...]
        need = (topk - count(lambda blk, base: blk > thr)).astype(F32)

        def fix(j, seen):
            base = pl.multiple_of(j * kc, kc)
            blk = score_scr[cur, pl.ds(base, kc), :]
            eq = blk == thr
            rank = _dot(tri_ref[...], jnp.where(eq, 1.0, 0.0).astype(BF16)) + seen
            score_scr[cur, pl.ds(base, kc), :] = jnp.where(eq, jnp.where(rank > need, -jnp.inf, blk), blk)
            return rank[kc - 1:kc, :]
        lax.fori_loop(0, nch, fix, jnp.zeros((1, qb), F32))

    qt = qt_ref[...]
    qscale = (HEAD_DIM_A ** -0.5) * float(np.log2(np.e))
    for p in range(nh // 2):
        r = _dot(wukt_ref[p], qt[p * LANES:(p + 1) * LANES, :])
        for v in range(2):
            h = 2 * p + v
            qlat_scr[:, h * qb:(h + 1) * qb] = (r[v * KV_LATENT:(v + 1) * KV_LATENT, :] * qscale).astype(BF16)
    load_indexer_queries(iqtn_ref[...])
    wts_next = wtn_ref[...] * wscale
    tn0 = inext * qb
    reset_partials()

    m_scr[...] = jnp.full(m_scr.shape, NEG_BIG, F32)
    acc_scr[...] = jnp.zeros(acc_scr.shape, F32)

    def sweep(j, _):
        base = pl.multiple_of(j * kc, kc)
        kv = ckv_ref[0, pl.ds(base, kc), :]
        kvt = ckvt_ref[0, j]
        bias = jnp.where(score_scr[cur, pl.ds(base, kc), :] >= thr_scr[...], 0.0, NEG_BIG)
        s = _dot(kv, qlat_scr[...])
        score_chunk(j, nxt, wts_next, tn0)
        ps, alphas = [], []
        for h in range(nh):
            sh = s[:, h * qb:(h + 1) * qb] + bias
            m_old = m_scr[h:h + 1, :]
            m_new = jnp.maximum(m_old, jnp.max(sh, axis=0, keepdims=True))
            ps.append(jnp.exp2(sh - m_new).astype(BF16))
            alphas.append(jnp.exp2(m_old - m_new))
            m_scr[h:h + 1, :] = m_new
        pv = _dot(kvt, jnp.concatenate(ps, axis=1))
        acc_scr[...] = acc_scr[...] * jnp.concatenate(alphas, axis=1) + pv
        return 0

    lax.fori_loop(0, nch, sweep, 0)

    @pl.when(nch_next > nch)
    def _():
        score_chunk(nch, nxt, wts_next, tn0)

    publish_stats()

    inv_l = 1.0 / acc_scr[KV_LATENT:KV_LATENT + 1, :]
    outs = []
    for h in range(nh):
        cols = slice(h * qb, (h + 1) * qb)
        olat = (acc_scr[0:KV_LATENT, cols] * inv_l[:, cols]).astype(BF16)
        outs.append(_dot(wuvt_ref[h], olat))
    o_ref[...] = jnp.concatenate(outs, axis=0).T.astype(BF16)


def _dsa(qt, iqt, ikwt, ik, ckv, ckvt, w_uk, w_uv, b, s):
    kc = KEY_CHUNK
    nb = s // Q_BLOCK
    nchunks = s // kc
    topk = min(TOPK_MAX, s // 4)
    hd = N_HEADS_A * HEAD_DIM_A
    wukt = jnp.swapaxes(w_uk, 1, 2).reshape(N_HEADS_A // 2, 2, KV_LATENT, HEAD_DIM_A)
    eye2 = jnp.eye(2, dtype=F32)
    wukt = jnp.einsum('pvcd,vu->pvcud', wukt, eye2).reshape(N_HEADS_A // 2, 2 * KV_LATENT, 2 * HEAD_DIM_A)
    wuvt = jnp.swapaxes(w_uv, 1, 2)
    this_blk = lambda bi, i: bi * nb + i
    next_blk = lambda bi, i: bi * nb + jnp.minimum(i + 1, nb - 1)
    qcol = lambda r, blk: pl.BlockSpec((r, Q_BLOCK), lambda bi, i: (0, blk(bi, i)))
    wrow = lambda blk: pl.BlockSpec((IDX_HEADS, Q_BLOCK), lambda bi, i: (IDX_DIM // IDX_HEADS, blk(bi, i)))
    kern = functools.partial(_dsa_kernel, topk=topk, nblocks=nb)
    return pl.pallas_call(
        kern,
        grid=(b, nb),
        in_specs=[qcol(hd, this_blk), qcol(IDX_HEADS * IDX_DIM, this_blk), wrow(this_blk),
                  qcol(IDX_HEADS * IDX_DIM, next_blk), wrow(next_blk),
                  pl.BlockSpec((1, s, LANES), lambda bi, i: (bi, 0, 0)),
                  pl.BlockSpec((1, s, KV_LATENT), lambda bi, i: (bi, 0, 0)),
                  pl.BlockSpec((1, nchunks, KV_LATENT + KVT_PAD, kc), lambda bi, i: (bi, 0, 0, 0)),
                  _vmem_full(), _vmem_full(), _vmem_full()],
        out_specs=pl.BlockSpec((Q_BLOCK, hd), lambda bi, i: (bi * nb + i, 0)),
        out_shape=jax.ShapeDtypeStruct((b * s, hd), BF16),
        scratch_shapes=[
            pltpu.VMEM((2, s, Q_BLOCK), F32),
            pltpu.VMEM((KV_LATENT, N_HEADS_A * Q_BLOCK), BF16),
            pltpu.VMEM((LANES, IDX_HEADS * Q_BLOCK), BF16),
            pltpu.VMEM((KV_LATENT + KVT_PAD, N_HEADS_A * Q_BLOCK), F32),
            pltpu.VMEM((N_HEADS_A, Q_BLOCK), F32),
            pltpu.VMEM((1, Q_BLOCK), F32),
            pltpu.VMEM((4, FOLD_ROWS, Q_BLOCK), F32),
            pltpu.VMEM((SUBLANES, Q_BLOCK), F32),
        ],
        compiler_params=_params("parallel", "arbitrary"),
        name="dsa_attention",
    )(qt, iqt, ikwt, iqt, ikwt, ik.reshape(b, s, LANES), ckv.reshape(b, s, KV_LATENT),
      ckvt.reshape(b, nchunks, KV_LATENT + KVT_PAD, kc), wukt.astype(BF16), wuvt.astype(BF16),
      jnp.tril(jnp.ones((kc, kc), BF16)))


def _causal_conv(x, w, halo):
    taps = w.shape[0]
    top_row = lax.broadcasted_iota(jnp.int32, halo.shape, 0)
    y = x * w[taps - 1:taps, :]
    for k in range(1, taps):
        xk = pltpu.roll(x, k, 0)
        top = jnp.where(top_row < k, pltpu.roll(halo, k, 0), xk[:SUBLANES, :])
        xk = jnp.concatenate([top, xk[SUBLANES:, :]], axis=0)
        y = y + xk * w[taps - 1 - k:taps - k, :]
    return y


def _split3(x):
    p1 = x.astype(BF16)
    r = x - p1.astype(F32)
    p2 = r.astype(BF16)
    return p1, p2, (r - p2.astype(F32)).astype(BF16)


def _ssd_kernel(xbc_ref, z_ref, dt_ref, cw_ref, cb_ref, dtb_ref, alog_ref, dexp_ref, nw_ref, e_ref,
                o_ref, halo_scr, state_scr):
    c = pl.program_id(1)
    L = SSM_CHUNK
    n = SSM_STATE
    gw = SSM_D_INNER // SSM_GROUPS

    @pl.when(c == 0)
    def _():
        halo_scr[...] = jnp.zeros(halo_scr.shape, F32)
        state_scr[...] = jnp.zeros(state_scr.shape, F32)

    xbc = xbc_ref[0]
    conv = _causal_conv(xbc, cw_ref[...], halo_scr[...]) + cb_ref[...]
    halo_scr[...] = xbc[L - SUBLANES:, :]
    act = conv * jax.nn.sigmoid(conv)
    xs = act[:, :SSM_D_INNER]
    bm = act[:, SSM_D_INNER:SSM_D_INNER + SSM_GROUPS * n]
    cm = act[:, SSM_D_INNER + SSM_GROUPS * n:]

    dt = jax.nn.softplus(dt_ref[0] + dtb_ref[...])
    a = dt * (-jnp.exp(alog_ref[...]))
    ri = lax.broadcasted_iota(jnp.int32, (L, L), 0)
    ci = lax.broadcasted_iota(jnp.int32, (L, L), 1)
    tri = ri >= ci
    tri_b = jnp.where(tri, 1.0, 0.0).astype(BF16)
    cs3 = _dot(tri_b, jnp.concatenate(_split3(a), axis=1))
    a_cs = cs3[:, :LANES] + cs3[:, LANES:2 * LANES] + cs3[:, 2 * LANES:]
    a_cs_t = a_cs.T
    w_end = dt * jnp.exp(a_cs[L - 1:L, :] - a_cs)
    grow_c = jnp.exp(a_cs)
    pieces = [p for arr in (dt, w_end, grow_c) for p in _split3(arr)]
    ex = _dot(jnp.concatenate(pieces, axis=0), e_ref[...])
    dt_e, wend_e, grow = (ex[3 * i * L:(3 * i + 1) * L] + ex[(3 * i + 1) * L:(3 * i + 2) * L]
                          + ex[(3 * i + 2) * L:(3 * i + 3) * L] for i in range(3))
    xdt_b = (xs * dt_e).astype(BF16)
    xend_b = (xs * wend_e).astype(BF16)
    lane = lax.broadcasted_iota(jnp.int32, (L, LANES), 1)

    y_parts = []
    for g in range(SSM_GROUPS):
        bg = bm[:, g * n:(g + 1) * n]
        cg = cm[:, g * n:(g + 1) * n].astype(BF16)
        cbm = _dot_nt(cg, bg.astype(BF16))
        for pr in range(gw // LANES):
            col = g * gw + pr * LANES
            xpair = xdt_b[:, col:col + LANES]
            outs = []
            for v in range(2):
                h = (col // SSM_HEAD_DIM) + v
                seg = a_cs[:, h:h + 1] - a_cs_t[h:h + 1, :]
                dec = jnp.exp(jnp.where(tri, seg, -jnp.inf))
                outs.append(_dot((cbm * dec).astype(BF16), xpair))
            y_parts.append(jnp.where(lane < SSM_HEAD_DIM, outs[0], outs[1]))
    y = jnp.concatenate(y_parts, axis=1)

    offs = []
    for g in range(SSM_GROUPS):
        sl = slice(g * gw, (g + 1) * gw)
        bg_t = bm[:, g * n:(g + 1) * n].T.astype(BF16)
        cg = cm[:, g * n:(g + 1) * n].astype(BF16)
        st = state_scr[g]
        offs.append(_dot(cg, st.astype(BF16)))
        state_scr[g] = st * grow[L - 1:L, sl] + _dot(bg_t, xend_b[:, sl])
    y = y + jnp.concatenate(offs, axis=1) * grow + xs * dexp_ref[...]

    z = z_ref[0]
    y = y * (z * jax.nn.sigmoid(z))
    nw = nw_ref[...]
    outs = []
    for g in range(SSM_GROUPS):
        sl = slice(g * gw, (g + 1) * gw)
        outs.append(_rms(y[:, sl], nw[:, sl]))
    o_ref[0] = jnp.concatenate(outs, axis=1).astype(BF16)


def _ssd(xbc, z, dtp, conv_w, conv_b, dt_bias, a_log, d, norm_w, b, s):
    L = SSM_CHUNK
    cx = SSM_D_INNER + 2 * SSM_GROUPS * SSM_STATE
    pad = LANES - SSM_HEADS
    expand = jnp.repeat(jnp.eye(SSM_HEADS, dtype=F32), SSM_HEAD_DIM, axis=1)
    expand = jnp.pad(expand, ((0, pad), (0, 0)))
    blk = lambda c: pl.BlockSpec((1, L, c), lambda bi, i: (bi, i, 0))
    return pl.pallas_call(
        _ssd_kernel,
        grid=(b, s // L),
        in_specs=[blk(cx), blk(SSM_D_INNER), blk(LANES)] + [_vmem_full()] * 7,
        out_specs=blk(SSM_D_INNER),
        out_shape=jax.ShapeDtypeStruct((b, s, SSM_D_INNER), BF16),
        scratch_shapes=[pltpu.VMEM((SUBLANES, cx), F32),
                        pltpu.VMEM((SSM_GROUPS, SSM_STATE, SSM_D_INNER // SSM_GROUPS), F32)],
        compiler_params=_params("parallel", "arbitrary"),
        name="ssd_mixer",
    )(xbc.reshape(b, s, cx), z.reshape(b, s, SSM_D_INNER), dtp.reshape(b, s, LANES),
      conv_w, conv_b.reshape(1, -1), jnp.pad(dt_bias, (0, pad)).reshape(1, -1),
      jnp.pad(a_log, (0, pad)).reshape(1, -1), jnp.repeat(d, SSM_HEAD_DIM).reshape(1, -1),
      norm_w.reshape(1, -1), expand.astype(BF16))


def _swiglu(x, g, wg, wu, wd):
    hn = _rms(x, g).astype(BF16)
    a = _dot(hn, wg[...])
    u = _dot(hn, wu[...])
    act = (a * jax.nn.sigmoid(a) * u).astype(BF16)
    return x + _dot(act, wd[...])


def _mix_ffn_kernel(x_ref, ya_ref, yb_ref, wa, wb, g_ref, wg, wu, wd, o_ref):
    x = x_ref[...] + _dot(ya_ref[...], wa[...]) + _dot(yb_ref[...], wb[...])
    o_ref[...] = _swiglu(x, g_ref[...], wg, wu, wd)


def _mix_ffn(x2, ya, yb, out_w, g, w_gate, w_up, w_down):
    n = x2.shape[0]
    tm = TOKEN_TILE
    ca = N_HEADS_A * HEAD_DIM_A
    w = out_w.astype(BF16)
    row = lambda c: pl.BlockSpec((tm, c), lambda i: (i, 0))
    return pl.pallas_call(
        _mix_ffn_kernel,
        grid=(n // tm,),
        in_specs=[row(D_MODEL), row(ca), row(SSM_D_INNER)] + [_vmem_full()] * 6,
        out_specs=row(D_MODEL),
        out_shape=jax.ShapeDtypeStruct((n, D_MODEL), F32),
        compiler_params=_params("parallel"),
        name="mix_ffn",
    )(x2, ya, yb, w[:ca], w[ca:], g.reshape(1, -1), w_gate.astype(BF16), w_up.astype(BF16), w_down.astype(BF16))


def _conv_ffn_kernel(x_ref, xh_ref, gc_ref, wb, wc, wv, cw_ref, wo, g_ref, wg, wu, wd, fn_ref, o_ref,
                     *, tiles_per_seq):
    i = pl.program_id(0)
    x = x_ref[...]
    xn = _rms(x, gc_ref[...]).astype(BF16)
    u = _dot(xn, wc[...]) * _dot(xn, wv[...])
    xhn = _rms(xh_ref[...], gc_ref[...]).astype(BF16)
    halo = jnp.where(i % tiles_per_seq == 0, 0.0, _dot(xhn, wc[...]) * _dot(xhn, wv[...]))
    y = (_dot(xn, wb[...]) * _causal_conv(u, cw_ref[...], halo)).astype(BF16)
    x = x + _dot(y, wo[...])
    o_ref[...] = _rms(_swiglu(x, g_ref[...], wg, wu, wd), fn_ref[...])


def _conv_ffn(x2, norm_w, in_w, conv_w, out_w, g, w_gate, w_up, w_down, final_norm, s):
    n = x2.shape[0]
    tm = CONV_TOKEN_TILE
    w = in_w.astype(BF16)
    row = pl.BlockSpec((tm, D_MODEL), lambda i: (i, 0))
    per = tm // SUBLANES
    halo = pl.BlockSpec((SUBLANES, D_MODEL), lambda i: (jnp.maximum(i * per - 1, 0), 0))
    return pl.pallas_call(
        functools.partial(_conv_ffn_kernel, tiles_per_seq=s // tm),
        grid=(n // tm,),
        in_specs=[row, halo] + [_vmem_full()] * 11,
        out_specs=row,
        out_shape=jax.ShapeDtypeStruct((n, D_MODEL), F32),
        compiler_params=_params("parallel"),
        name="conv_ffn",
    )(x2, x2, norm_w.reshape(1, -1), w[:, :SC_WIDTH], w[:, SC_WIDTH:2 * SC_WIDTH], w[:, 2 * SC_WIDTH:],
      conv_w, out_w.astype(BF16), g.reshape(1, -1), w_gate.astype(BF16), w_up.astype(BF16), w_down.astype(BF16),
      final_norm.reshape(1, -1))


def kernel(x, l0_attn_norm, l0_in_w, l0_kv_norm, l0_w_uk, l0_w_uv, l0_conv_w, l0_conv_b, l0_dt_bias,
           l0_A_log, l0_D, l0_ssm_norm, l0_out_w, l0_ffn_norm, l0_w_gate, l0_w_up, l0_w_down,
           l1_conv_norm, l1_in_w, l1_conv_w, l1_out_w, l1_ffn_norm, l1_w_gate, l1_w_up, l1_w_down,
           final_norm):
    b, s, d = x.shape
    assert d == D_MODEL and s % TOKEN_TILE == 0 and s % CONV_TOKEN_TILE == 0 and s % KEY_CHUNK == 0 and s % SSM_CHUNK == 0
    x2 = x.reshape(b * s, d)
    qt, iqt, ikwt, ckv, ckvt, ik, z, xbc, dtp = _in_proj0(x2, l0_attn_norm, l0_in_w, l0_kv_norm)
    ya = _dsa(qt, iqt, ikwt, ik, ckv, ckvt, l0_w_uk, l0_w_uv, b, s)
    yb = _ssd(xbc, z, dtp, l0_conv_w, l0_conv_b, l0_dt_bias, l0_A_log, l0_D, l0_ssm_norm, b, s)
    x2 = _mix_ffn(x2, ya, yb.reshape(b * s, -1), l0_out_w, l0_ffn_norm, l0_w_gate, l0_w_up, l0_w_down)
    x2 = _conv_ffn(x2, l1_conv_norm, l1_in_w, l1_conv_w, l1_out_w, l1_ffn_norm, l1_w_gate, l1_w_up, l1_w_down,
                   final_norm, s)
    return x2.reshape(b, s, d)
```

```python
import functools

import jax
import jax.numpy as jnp
import numpy as np
from jax import lax
from jax.experimental import pallas as pl
from jax.experimental.pallas import tpu as pltpu

D_MODEL = 1024
N_HEADS_A = 8
HEAD_DIM_A = 64
KV_LATENT = 256
IDX_HEADS = 8
IDX_DIM = 64
TOPK_MAX = 256
Q_BLOCK = 128
SSM_D_INNER = 1024
SSM_HEADS = 16
SSM_HEAD_DIM = SSM_D_INNER // SSM_HEADS
SSM_GROUPS = 2
SSM_STATE = 128
SSM_CONV = 4
SSM_CHUNK = 128
SC_WIDTH = D_MODEL
SC_CONV = 3
D_FF = -(-8 * D_MODEL // (3 * 256)) * 256
EPS = 1e-6

LANES = 128
SUBLANES = 8
TOKEN_TILE = 512
CONV_TOKEN_TILE = 512
KEY_CHUNK = 512
KVT_PAD = 16
VMEM_LIMIT = 56 * 1024 * 1024

F32 = jnp.float32
BF16 = jnp.bfloat16
NEG_BIG = -1e30
F32_LOWEST = float(np.finfo(np.float32).min)
KEY_LOWEST = int(np.array(F32_LOWEST, np.float32).view(np.int32)) ^ 0x7FFFFFFF


def _vmem_full():
    return pl.BlockSpec(memory_space=pltpu.VMEM)


def _params(*sem):
    return pltpu.CompilerParams(dimension_semantics=sem, vmem_limit_bytes=VMEM_LIMIT)


def _rms(x, w):
    return x * lax.rsqrt(jnp.mean(x * x, axis=-1, keepdims=True) + EPS) * w


def _dot(a, b):
    return jnp.dot(a, b, preferred_element_type=F32)


def _dot_nt(a, b):
    return lax.dot_general(a, b, (((1,), (1,)), ((), ())), preferred_element_type=F32)


def _in0_kernel(x_ref, g_ref, wqt, wiqt, wikwt, wckv, wikw, wz, wxbc, wdt, kvn_ref,
                qt_o, iqt_o, ikwt_o, ckv_o, ckvt_o, ik_o, z_o, xbc_o, dt_o):
    xn = _rms(x_ref[...], g_ref[...]).astype(BF16)
    qt_o[...] = _dot_nt(wqt[...], xn).astype(BF16)
    iqt_o[...] = _dot_nt(wiqt[...], xn).astype(BF16)
    ikwt_o[...] = _dot_nt(wikwt[...], xn)
    c = _rms(_dot(xn, wckv[...]), kvn_ref[...])
    ckv_o[...] = c.astype(BF16)
    ones_rows = (lax.broadcasted_iota(jnp.int32, (KVT_PAD, c.shape[0]), 0) == 0).astype(F32)
    ckvt_o[0] = jnp.concatenate([c.T, ones_rows], axis=0).astype(BF16)
    ik_o[...] = _dot(xn, wikw[...]).astype(BF16)
    z_o[...] = _dot(xn, wz[...])
    xbc_o[...] = _dot(xn, wxbc[...])
    dt_o[...] = _dot(xn, wdt[...])


def _in_proj0(x2, g, in_w, kv_norm):
    n = x2.shape[0]
    cuts = np.cumsum((N_HEADS_A * HEAD_DIM_A, KV_LATENT, IDX_HEADS * IDX_DIM, IDX_DIM, IDX_HEADS,
                      SSM_D_INNER, SSM_D_INNER + 2 * SSM_GROUPS * SSM_STATE, SSM_HEADS))
    w = in_w.astype(BF16)
    wq, wckv, wiq = w[:, :cuts[0]], w[:, cuts[0]:cuts[1]], w[:, cuts[1]:cuts[2]]
    wikw = jnp.pad(w[:, cuts[2]:cuts[4]], ((0, 0), (0, LANES - IDX_DIM - IDX_HEADS)))
    wz, wxbc = w[:, cuts[4]:cuts[5]], w[:, cuts[5]:cuts[6]]
    wdt = jnp.pad(w[:, cuts[6]:cuts[7]], ((0, 0), (0, LANES - SSM_HEADS)))
    tm = KEY_CHUNK
    row = lambda c: pl.BlockSpec((tm, c), lambda i: (i, 0))
    col = lambda r: pl.BlockSpec((r, tm), lambda i: (0, i))
    hd = N_HEADS_A * HEAD_DIM_A
    out_specs = [col(hd), col(hd), col(LANES), row(KV_LATENT),
                 pl.BlockSpec((1, KV_LATENT + KVT_PAD, tm), lambda i: (i, 0, 0)),
                 row(LANES), row(SSM_D_INNER), row(wxbc.shape[1]), row(LANES)]
    sds = jax.ShapeDtypeStruct
    out_shape = [sds((hd, n), BF16), sds((hd, n), BF16), sds((LANES, n), F32), sds((n, KV_LATENT), BF16),
                 sds((n // tm, KV_LATENT + KVT_PAD, tm), BF16), sds((n, LANES), BF16), sds((n, SSM_D_INNER), F32),
                 sds((n, wxbc.shape[1]), F32), sds((n, LANES), F32)]
    return pl.pallas_call(
        _in0_kernel,
        grid=(n // tm,),
        in_specs=[row(D_MODEL)] + [_vmem_full()] * 10,
        out_specs=out_specs,
        out_shape=out_shape,
        compiler_params=_params("parallel"),
        name="in_proj0",
    )(x2, g.reshape(1, -1), wq.T, wiq.T, wikw.T, wckv, wikw, wz, wxbc, wdt, kv_norm.reshape(1, -1))


def _f2k(x):
    b = pltpu.bitcast(x, jnp.int32)
    return jnp.where(b < 0, b ^ jnp.int32(0x7FFFFFFF), b)


def _k2f(k):
    b = jnp.where(k < 0, k ^ jnp.int32(0x7FFFFFFF), k)
    return pltpu.bitcast(b, F32)


FOLD_ROWS = 4 * SUBLANES
VALUE_PROBES = 12
PROBES_PER_CHECK = 3
UNCHECKED_ROUNDS = 5


def _fold(x, op):
    return op(x.reshape(x.shape[0] // FOLD_ROWS, FOLD_ROWS, LANES), axis=0)


def _any_lane(flag):
    return jnp.max(jnp.where(flag, 1.0, 0.0))


def _dsa_kernel(qt_ref, iqt_ref, wt_ref, iqtn_ref, wtn_ref, ik_ref, ckv_ref, ckvt_ref, wukt_ref, wuvt_ref, tri_ref,
                o_ref,
                score_scr, qlat_scr, r_scr, acc_scr, m_scr, thr_scr, part_scr, stat_scr,
                *, topk, nblocks):
    i = pl.program_id(1)
    kc = KEY_CHUNK
    qb = Q_BLOCK
    nh = N_HEADS_A
    cur = i % 2
    nxt = 1 - cur
    inext = jnp.minimum(i + 1, nblocks - 1)
    nch = (i * qb) // kc + 1
    nch_next = (inext * qb) // kc + 1
    t0 = i * qb
    key_i = lax.broadcasted_iota(jnp.int32, (kc, LANES), 0)
    rel_i = key_i - lax.broadcasted_iota(jnp.int32, (kc, LANES), 1)

    wscale = (IDX_HEADS ** -0.5) * (IDX_DIM ** -0.5)

    def load_indexer_queries(iqt):
        r_scr[...] = jnp.zeros(r_scr.shape, BF16)
        for h in range(IDX_HEADS):
            r_scr[0:IDX_DIM, h * qb:(h + 1) * qb] = iqt[h * IDX_DIM:(h + 1) * IDX_DIM, :]

    def reset_partials():
        part_scr[0] = jnp.full((FOLD_ROWS, LANES), jnp.inf, F32)
        part_scr[1] = jnp.full((FOLD_ROWS, LANES), -jnp.inf, F32)
        part_scr[2] = jnp.zeros((FOLD_ROWS, LANES), F32)
        part_scr[3] = jnp.zeros((FOLD_ROWS, LANES), F32)

    def score_chunk(j, buf, wts, tq0):
        base = pl.multiple_of(j * kc, kc)
        lt = _dot(ik_ref[0, pl.ds(base, kc), :], r_scr[...])
        acc = None
        for h in range(IDX_HEADS):
            term = jnp.maximum(lt[:, h * qb:(h + 1) * qb], 0.0) * wts[h:h + 1, :]
            acc = term if acc is None else acc + term
        sc = jnp.where(rel_i <= (tq0 - base), acc, -jnp.inf)
        score_scr[buf, pl.ds(base, kc), :] = sc
        part_scr[0] = jnp.minimum(part_scr[0], _fold(acc, jnp.min))
        part_scr[1] = jnp.maximum(part_scr[1], _fold(acc, jnp.max))
        part_scr[2] = part_scr[2] + _fold(jnp.where(sc > 0.0, 1.0, 0.0), jnp.sum)
        part_scr[3] = part_scr[3] + _fold(jnp.where(sc >= 0.0, 1.0, 0.0), jnp.sum)

    def publish_stats():
        stat_scr[0:1, :] = jnp.min(part_scr[0], axis=0, keepdims=True)
        stat_scr[1:2, :] = jnp.max(part_scr[1], axis=0, keepdims=True)
        stat_scr[2:3, :] = jnp.sum(part_scr[2], axis=0, keepdims=True)
        stat_scr[3:4, :] = jnp.sum(part_scr[3], axis=0, keepdims=True)

    @pl.when(i == 0)
    def _():
        load_indexer_queries(iqt_ref[...])
        reset_partials()
        score_chunk(0, cur, wt_ref[...] * wscale, t0)
        publish_stats()

    colmin = stat_scr[0:1, :]
    colmax = stat_scr[1:2, :]
    npos = stat_scr[2:3, :].astype(jnp.int32)
    nnonneg = stat_scr[3:4, :].astype(jnp.int32)

    def count(pred):
        def body(j, acc):
            base = pl.multiple_of(j * kc, kc)
            blk = score_scr[cur, pl.ds(base, kc), :]
            return acc + _fold(jnp.where(pred(blk, base), 1.0, 0.0), jnp.sum)
        acc = lax.fori_loop(0, nch, body, jnp.zeros((FOLD_ROWS, LANES), F32))
        return jnp.sum(acc, axis=0, keepdims=True).astype(jnp.int32)

    t_pos = t0 + lax.broadcasted_iota(jnp.int32, (1, qb), 1)
    nvalid = t_pos + 1
    allsel = nvalid <= topk
    key_zero, key_tiny, key_negzero = 0, 1, -1
    at_zero = jnp.logical_and(npos < topk, nnonneg >= topk)
    above = npos >= topk
    below = nnonneg < topk
    lo0 = jnp.where(at_zero, key_zero, jnp.where(above, key_tiny, _f2k(colmin)))
    cnt0 = jnp.where(at_zero, nnonneg, jnp.where(above, npos, nvalid))
    hi0 = jnp.where(at_zero, key_zero + 1, jnp.where(below, key_negzero, _f2k(colmax) + 1))
    lo0 = jnp.where(allsel, jnp.int32(KEY_LOWEST), lo0)
    hi0 = jnp.where(allsel, jnp.int32(KEY_LOWEST + 1), hi0)
    cnt0 = jnp.where(allsel, jnp.int32(topk), cnt0)
    go0 = _any_lane(lo0 + 1 < hi0)
    cnt_hi0 = jnp.where(below, nnonneg, 0)

    def search_step(lo, hi, cnt_lo, cnt_hi, by_value):
        active = lo + 1 < hi
        mid_k = (lo & hi) + ((lo ^ hi) >> 1)
        mid_v = _f2k(0.5 * _k2f(lo) + 0.5 * _k2f(hi - 1))
        mid_v = jnp.minimum(jnp.maximum(mid_v, lo + 1), hi - 1)
        mid = jnp.where(by_value, mid_v, mid_k)
        cand = _k2f(mid)
        cnt = count(lambda blk, base: blk >= cand)
        ge = cnt >= topk
        up = jnp.logical_and(active, ge)
        dn = jnp.logical_and(active, jnp.logical_not(ge))
        hit = jnp.logical_and(active, cnt == topk)
        lo = jnp.where(up, mid, lo)
        hi = jnp.where(hit, mid + 1, jnp.where(dn, mid, hi))
        cnt_lo = jnp.where(up, cnt, cnt_lo)
        cnt_hi = jnp.where(dn, cnt, cnt_hi)
        return lo, hi, cnt_lo, cnt_hi

    def resolve_small(lo, hi, cnt_lo, cnt_hi):
        active = lo + 1 < hi
        v_below, v_hi = _k2f(lo - 1), _k2f(hi)

        def body(j, c):
            mn, mx = c
            base = pl.multiple_of(j * kc, kc)
            blk = score_scr[cur, pl.ds(base, kc), :]
            mn = jnp.minimum(mn, _fold(jnp.where(blk > v_below, blk, jnp.inf), jnp.min))
            mx = jnp.maximum(mx, _fold(jnp.where(blk < v_hi, blk, -jnp.inf), jnp.max))
            return mn, mx
        mn, mx = lax.fori_loop(0, nch, body, (jnp.full((FOLD_ROWS, LANES), jnp.inf, F32),
                                              jnp.full((FOLD_ROWS, LANES), -jnp.inf, F32)))
        mn = jnp.min(mn, axis=0, keepdims=True)
        mx = jnp.max(mx, axis=0, keepdims=True)
        inside = cnt_lo - cnt_hi
        small = jnp.logical_and(active, inside <= 2)
        want_top = (topk - cnt_hi) == 1
        twins = jnp.logical_and(mx == mn, inside == 2)
        key_t = _f2k(jnp.where(want_top, mx, mn))
        cnt_t = jnp.where(want_top, cnt_hi + 1 + jnp.where(twins, 1, 0), cnt_lo)
        return (jnp.where(small, key_t, lo), jnp.where(small, key_t + 1, hi), jnp.where(small, cnt_t, cnt_lo), cnt_hi)

    def bis_cond(c):
        return c[5] > 0.0

    def probe_round(r, c):
        for u in range(PROBES_PER_CHECK):
            c = search_step(*c, r * PROBES_PER_CHECK + u < VALUE_PROBES)
        return c

    def bis_body(c):
        r = c[4]
        st = resolve_small(*probe_round(r, c[:4]))
        return (*st, r + 1, _any_lane(st[0] + 1 < st[1]))

    first = jnp.where(go0 > 0.0, UNCHECKED_ROUNDS, 0)
    st = resolve_small(*lax.fori_loop(0, first, probe_round, (lo0, hi0, cnt0, cnt_hi0)))
    lo, _, cnt_lo, _, _, _ = lax.while_loop(bis_cond, bis_body, (*st, first, _any_lane(st[0] + 1 < st[1])))
    thr_scr[...] = _k2f(lo)
    tied = cnt_lo > topk

    @pl.when(_any_lane(tied) > 0.0)
    def _():
        thr = thr_scr[...]
        need = (topk - count(lambda blk, base: blk > thr)).astype(F32)

        def fix(j, seen):
            base = pl.multiple_of(j * kc, kc)
            blk = score_scr[cur, pl.ds(base, kc), :]
            eq = blk == thr
            eqb = jnp.where(eq, 1.0, 0.0).astype(BF16)
            ranks = []
            for r in range(kc // LANES):
                pre = _dot(tri_ref[...], eqb[r * LANES:(r + 1) * LANES, :]) + seen
                ranks.append(pre)
                seen = pre[LANES - 1:LANES, :]
            rank = jnp.concatenate(ranks, axis=0)
            score_scr[cur, pl.ds(base, kc), :] = jnp.where(eq, jnp.where(rank > need, -jnp.inf, blk), blk)
            return seen
        lax.fori_loop(0, nch, fix, jnp.zeros((1, qb), F32))

    qt = qt_ref[...]
    qscale = (HEAD_DIM_A ** -0.5) * float(np.log2(np.e))
    for p in range(nh // 2):
        r = _dot(wukt_ref[p], qt[p * LANES:(p + 1) * LANES, :])
        for v in range(2):
            h = 2 * p + v
            qlat_scr[:, h * qb:(h + 1) * qb] = (r[v * KV_LATENT:(v + 1) * KV_LATENT, :] * qscale).astype(BF16)
    load_indexer_queries(iqtn_ref[...])
    wts_next = wtn_ref[...] * wscale
    tn0 = inext * qb
    reset_partials()

    m_scr[...] = jnp.full(m_scr.shape, NEG_BIG, F32)
    acc_scr[...] = jnp.zeros(acc_scr.shape, F32)

    def sweep(j, _):
        base = pl.multiple_of(j * kc, kc)
        kv = ckv_ref[0, pl.ds(base, kc), :]
        kvt = ckvt_ref[0, j]
        bias = jnp.where(score_scr[cur, pl.ds(base, kc), :] >= thr_scr[...], 0.0, NEG_BIG)
        s = _dot(kv, qlat_scr[...])
        score_chunk(j, nxt, wts_next, tn0)
        ps, alphas = [], []
        for h in range(nh):
            sh = s[:, h * qb:(h + 1) * qb] + bias
            m_old = m_scr[h:h + 1, :]
            m_new = jnp.maximum(m_old, jnp.max(sh, axis=0, keepdims=True))
            ps.append(jnp.exp2(sh - m_new).astype(BF16))
            alphas.append(jnp.exp2(m_old - m_new))
            m_scr[h:h + 1, :] = m_new
        pv = _dot(kvt, jnp.concatenate(ps, axis=1))
        acc_scr[...] = acc_scr[...] * jnp.concatenate(alphas, axis=1) + pv
        return 0

    lax.fori_loop(0, nch, sweep, 0)

    @pl.when(nch_next > nch)
    def _():
        score_chunk(nch, nxt, wts_next, tn0)

    publish_stats()

    inv_l = 1.0 / acc_scr[KV_LATENT:KV_LATENT + 1, :]
    outs = []
    for h in range(nh):
        cols = slice(h * qb, (h + 1) * qb)
        olat = (acc_scr[0:KV_LATENT, cols] * inv_l[:, cols]).astype(BF16)
        outs.append(_dot(wuvt_ref[h], olat))
    o_ref[...] = jnp.concatenate(outs, axis=0).T.astype(BF16)


def _dsa(qt, iqt, ikwt, ik, ckv, ckvt, w_uk, w_uv, b, s):
    kc = KEY_CHUNK
    nb = s // Q_BLOCK
    nchunks = s // kc
    topk = min(TOPK_MAX, s // 4)
    hd = N_HEADS_A * HEAD_DIM_A
    wukt = jnp.swapaxes(w_uk, 1, 2).reshape(N_HEADS_A // 2, 2, KV_LATENT, HEAD_DIM_A)
    eye2 = jnp.eye(2, dtype=F32)
    wukt = jnp.einsum('pvcd,vu->pvcud', wukt, eye2).reshape(N_HEADS_A // 2, 2 * KV_LATENT, 2 * HEAD_DIM_A)
    wuvt = jnp.swapaxes(w_uv, 1, 2)
    this_blk = lambda bi, i: bi * nb + i
    next_blk = lambda bi, i: bi * nb + jnp.minimum(i + 1, nb - 1)
    qcol = lambda r, blk: pl.BlockSpec((r, Q_BLOCK), lambda bi, i: (0, blk(bi, i)))
    wrow = lambda blk: pl.BlockSpec((IDX_HEADS, Q_BLOCK), lambda bi, i: (IDX_DIM // IDX_HEADS, blk(bi, i)))
    kern = functools.partial(_dsa_kernel, topk=topk, nblocks=nb)
    return pl.pallas_call(
        kern,
        grid=(b, nb),
        in_specs=[qcol(hd, this_blk), qcol(IDX_HEADS * IDX_DIM, this_blk), wrow(this_blk),
                  qcol(IDX_HEADS * IDX_DIM, next_blk), wrow(next_blk),
                  pl.BlockSpec((1, s, LANES), lambda bi, i: (bi, 0, 0)),
                  pl.BlockSpec((1, s, KV_LATENT), lambda bi, i: (bi, 0, 0)),
                  pl.BlockSpec((1, nchunks, KV_LATENT + KVT_PAD, kc), lambda bi, i: (bi, 0, 0, 0)),
                  _vmem_full(), _vmem_full(), _vmem_full()],
        out_specs=pl.BlockSpec((Q_BLOCK, hd), lambda bi, i: (bi * nb + i, 0)),
        out_shape=jax.ShapeDtypeStruct((b * s, hd), BF16),
        scratch_shapes=[
            pltpu.VMEM((2, s, Q_BLOCK), F32),
            pltpu.VMEM((KV_LATENT, N_HEADS_A * Q_BLOCK), BF16),
            pltpu.VMEM((LANES, IDX_HEADS * Q_BLOCK), BF16),
            pltpu.VMEM((KV_LATENT + KVT_PAD, N_HEADS_A * Q_BLOCK), F32),
            pltpu.VMEM((N_HEADS_A, Q_BLOCK), F32),
            pltpu.VMEM((1, Q_BLOCK), F32),
            pltpu.VMEM((4, FOLD_ROWS, Q_BLOCK), F32),
            pltpu.VMEM((SUBLANES, Q_BLOCK), F32),
        ],
        compiler_params=_params("parallel", "arbitrary"),
        name="dsa_attention",
    )(qt, iqt, ikwt, iqt, ikwt, ik.reshape(b, s, LANES), ckv.reshape(b, s, KV_LATENT),
      ckvt.reshape(b, nchunks, KV_LATENT + KVT_PAD, kc), wukt.astype(BF16), wuvt.astype(BF16),
      jnp.tril(jnp.ones((LANES, LANES), BF16)))


def _causal_conv(x, w, halo):
    taps = w.shape[0]
    top_row = lax.broadcasted_iota(jnp.int32, halo.shape, 0)
    y = x * w[taps - 1:taps, :]
    for k in range(1, taps):
        xk = pltpu.roll(x, k, 0)
        top = jnp.where(top_row < k, pltpu.roll(halo, k, 0), xk[:SUBLANES, :])
        xk = jnp.concatenate([top, xk[SUBLANES:, :]], axis=0)
        y = y + xk * w[taps - 1 - k:taps - k, :]
    return y


def _split3(x):
    p1 = x.astype(BF16)
    r = x - p1.astype(F32)
    p2 = r.astype(BF16)
    return p1, p2, (r - p2.astype(F32)).astype(BF16)


def _ssd_kernel(xbc_ref, z_ref, dt_ref, cw_ref, cb_ref, dtb_ref, alog_ref, dexp_ref, nw_ref, e_ref,
                o_ref, halo_scr, state_scr):
    c = pl.program_id(1)
    L = SSM_CHUNK
    n = SSM_STATE
    gw = SSM_D_INNER // SSM_GROUPS

    @pl.when(c == 0)
    def _():
        halo_scr[...] = jnp.zeros(halo_scr.shape, F32)
        state_scr[...] = jnp.zeros(state_scr.shape, F32)

    xbc = xbc_ref[0]
    conv = _causal_conv(xbc, cw_ref[...], halo_scr[...]) + cb_ref[...]
    halo_scr[...] = xbc[L - SUBLANES:, :]
    act = conv * jax.nn.sigmoid(conv)
    xs = act[:, :SSM_D_INNER]
    bm = act[:, SSM_D_INNER:SSM_D_INNER + SSM_GROUPS * n]
    cm = act[:, SSM_D_INNER + SSM_GROUPS * n:]

    dt = jax.nn.softplus(dt_ref[0] + dtb_ref[...])
    a = dt * (-jnp.exp(alog_ref[...]))
    ri = lax.broadcasted_iota(jnp.int32, (L, L), 0)
    ci = lax.broadcasted_iota(jnp.int32, (L, L), 1)
    tri = ri >= ci
    tri_b = jnp.where(tri, 1.0, 0.0).astype(BF16)
    cs3 = _dot(tri_b, jnp.concatenate(_split3(a), axis=1))
    a_cs = cs3[:, :LANES] + cs3[:, LANES:2 * LANES] + cs3[:, 2 * LANES:]
    a_cs_t = a_cs.T
    w_end = dt * jnp.exp(a_cs[L - 1:L, :] - a_cs)
    grow_c = jnp.exp(a_cs)
    pieces = [p for arr in (dt, w_end, grow_c) for p in _split3(arr)]
    ex = _dot(jnp.concatenate(pieces, axis=0), e_ref[...])
    dt_e, wend_e, grow = (ex[3 * i * L:(3 * i + 1) * L] + ex[(3 * i + 1) * L:(3 * i + 2) * L]
                          + ex[(3 * i + 2) * L:(3 * i + 3) * L] for i in range(3))
    xdt_b = (xs * dt_e).astype(BF16)
    xend_b = (xs * wend_e).astype(BF16)
    lane = lax.broadcasted_iota(jnp.int32, (L, LANES), 1)

    y_parts = []
    for g in range(SSM_GROUPS):
        bg = bm[:, g * n:(g + 1) * n]
        cg = cm[:, g * n:(g + 1) * n].astype(BF16)
        cbm = _dot_nt(cg, bg.astype(BF16))
        for pr in range(gw // LANES):
            col = g * gw + pr * LANES
            xpair = xdt_b[:, col:col + LANES]
            outs = []
            for v in range(2):
                h = (col // SSM_HEAD_DIM) + v
                seg = a_cs[:, h:h + 1] - a_cs_t[h:h + 1, :]
                dec = jnp.exp(jnp.where(tri, seg, -jnp.inf))
                outs.append(_dot((cbm * dec).astype(BF16), xpair))
            y_parts.append(jnp.where(lane < SSM_HEAD_DIM, outs[0], outs[1]))
    y = jnp.concatenate(y_parts, axis=1)

    offs = []
    for g in range(SSM_GROUPS):
        sl = slice(g * gw, (g + 1) * gw)
        bg_t = bm[:, g * n:(g + 1) * n].T.astype(BF16)
        cg = cm[:, g * n:(g + 1) * n].astype(BF16)
        st = state_scr[g]
        offs.append(_dot(cg, st.astype(BF16)))
        state_scr[g] = st * grow[L - 1:L, sl] + _dot(bg_t, xend_b[:, sl])
    y = y + jnp.concatenate(offs, axis=1) * grow + xs * dexp_ref[...]

    z = z_ref[0]
    y = y * (z * jax.nn.sigmoid(z))
    nw = nw_ref[...]
    outs = []
    for g in range(SSM_GROUPS):
        sl = slice(g * gw, (g + 1) * gw)
        outs.append(_rms(y[:, sl], nw[:, sl]))
    o_ref[0] = jnp.concatenate(outs, axis=1).astype(BF16)


def _ssd(xbc, z, dtp, conv_w, conv_b, dt_bias, a_log, d, norm_w, b, s):
    L = SSM_CHUNK
    cx = SSM_D_INNER + 2 * SSM_GROUPS * SSM_STATE
    pad = LANES - SSM_HEADS
    expand = jnp.repeat(jnp.eye(SSM_HEADS, dtype=F32), SSM_HEAD_DIM, axis=1)
    expand = jnp.pad(expand, ((0, pad), (0, 0)))
    blk = lambda c: pl.BlockSpec((1, L, c), lambda bi, i: (bi, i, 0))
    return pl.pallas_call(
        _ssd_kernel,
        grid=(b, s // L),
        in_specs=[blk(cx), blk(SSM_D_INNER), blk(LANES)] + [_vmem_full()] * 7,
        out_specs=blk(SSM_D_INNER),
        out_shape=jax.ShapeDtypeStruct((b, s, SSM_D_INNER), BF16),
        scratch_shapes=[pltpu.VMEM((SUBLANES, cx), F32),
                        pltpu.VMEM((SSM_GROUPS, SSM_STATE, SSM_D_INNER // SSM_GROUPS), F32)],
        compiler_params=_params("parallel", "arbitrary"),
        name="ssd_mixer",
    )(xbc.reshape(b, s, cx), z.reshape(b, s, SSM_D_INNER), dtp.reshape(b, s, LANES),
      conv_w, conv_b.reshape(1, -1), jnp.pad(dt_bias, (0, pad)).reshape(1, -1),
      jnp.pad(a_log, (0, pad)).reshape(1, -1), jnp.repeat(d, SSM_HEAD_DIM).reshape(1, -1),
      norm_w.reshape(1, -1), expand.astype(BF16))


def _swiglu(x, g, wg, wu, wd):
    hn = _rms(x, g).astype(BF16)
    a = _dot(hn, wg[...])
    u = _dot(hn, wu[...])
    act = (a * jax.nn.sigmoid(a) * u).astype(BF16)
    return x + _dot(act, wd[...])


def _mix_ffn_kernel(x_ref, ya_ref, yb_ref, wa, wb, g_ref, wg, wu, wd, o_ref):
    x = x_ref[...] + _dot(ya_ref[...], wa[...]) + _dot(yb_ref[...], wb[...])
    o_ref[...] = _swiglu(x, g_ref[...], wg, wu, wd)


def _mix_ffn(x2, ya, yb, out_w, g, w_gate, w_up, w_down):
    n = x2.shape[0]
    tm = TOKEN_TILE
    ca = N_HEADS_A * HEAD_DIM_A
    w = out_w.astype(BF16)
    row = lambda c: pl.BlockSpec((tm, c), lambda i: (i, 0))
    return pl.pallas_call(
        _mix_ffn_kernel,
        grid=(n // tm,),
        in_specs=[row(D_MODEL), row(ca), row(SSM_D_INNER)] + [_vmem_full()] * 6,
        out_specs=row(D_MODEL),
        out_shape=jax.ShapeDtypeStruct((n, D_MODEL), F32),
        compiler_params=_params("parallel"),
        name="mix_ffn",
    )(x2, ya, yb, w[:ca], w[ca:], g.reshape(1, -1), w_gate.astype(BF16), w_up.astype(BF16), w_down.astype(BF16))


def _conv_ffn_kernel(x_ref, xh_ref, gc_ref, wb, wc, wv, cw_ref, wo, g_ref, wg, wu, wd, fn_ref, o_ref,
                     *, tiles_per_seq):
    i = pl.program_id(0)
    x = x_ref[...]
    xn = _rms(x, gc_ref[...]).astype(BF16)
    u = _dot(xn, wc[...]) * _dot(xn, wv[...])
    xhn = _rms(xh_ref[...], gc_ref[...]).astype(BF16)
    halo = jnp.where(i % tiles_per_seq == 0, 0.0, _dot(xhn, wc[...]) * _dot(xhn, wv[...]))
    y = (_dot(xn, wb[...]) * _causal_conv(u, cw_ref[...], halo)).astype(BF16)
    x = x + _dot(y, wo[...])
    o_ref[...] = _rms(_swiglu(x, g_ref[...], wg, wu, wd), fn_ref[...])


def _conv_ffn(x2, norm_w, in_w, conv_w, out_w, g, w_gate, w_up, w_down, final_norm, s):
    n = x2.shape[0]
    tm = CONV_TOKEN_TILE
    w = in_w.astype(BF16)
    row = pl.BlockSpec((tm, D_MODEL), lambda i: (i, 0))
    per = tm // SUBLANES
    halo = pl.BlockSpec((SUBLANES, D_MODEL), lambda i: (jnp.maximum(i * per - 1, 0), 0))
    return pl.pallas_call(
        functools.partial(_conv_ffn_kernel, tiles_per_seq=s // tm),
        grid=(n // tm,),
        in_specs=[row, halo] + [_vmem_full()] * 11,
        out_specs=row,
        out_shape=jax.ShapeDtypeStruct((n, D_MODEL), F32),
        compiler_params=_params("parallel"),
        name="conv_ffn",
    )(x2, x2, norm_w.reshape(1, -1), w[:, :SC_WIDTH], w[:, SC_WIDTH:2 * SC_WIDTH], w[:, 2 * SC_WIDTH:],
      conv_w, out_w.astype(BF16), g.reshape(1, -1), w_gate.astype(BF16), w_up.astype(BF16), w_down.astype(BF16),
      final_norm.reshape(1, -1))


def kernel(x, l0_attn_norm, l0_in_w, l0_kv_norm, l0_w_uk, l0_w_uv, l0_conv_w, l0_conv_b, l0_dt_bias,
           l0_A_log, l0_D, l0_ssm_norm, l0_out_w, l0_ffn_norm, l0_w_gate, l0_w_up, l0_w_down,
           l1_conv_norm, l1_in_w, l1_conv_w, l1_out_w, l1_ffn_norm, l1_w_gate, l1_w_up, l1_w_down,
           final_norm):
    b, s, d = x.shape
    assert d == D_MODEL and s % TOKEN_TILE == 0 and s % CONV_TOKEN_TILE == 0 and s % KEY_CHUNK == 0 and s % SSM_CHUNK == 0
    x2 = x.reshape(b * s, d)
    qt, iqt, ikwt, ckv, ckvt, ik, z, xbc, dtp = _in_proj0(x2, l0_attn_norm, l0_in_w, l0_kv_norm)
    ya = _dsa(qt, iqt, ikwt, ik, ckv, ckvt, l0_w_uk, l0_w_uv, b, s)
    yb = _ssd(xbc, z, dtp, l0_conv_w, l0_conv_b, l0_dt_bias, l0_A_log, l0_D, l0_ssm_norm, b, s)
    x2 = _mix_ffn(x2, ya, yb.reshape(b * s, -1), l0_out_w, l0_ffn_norm, l0_w_gate, l0_w_up, l0_w_down)
    x2 = _conv_ffn(x2, l1_conv_norm, l1_in_w, l1_conv_w, l1_out_w, l1_ffn_norm, l1_w_gate, l1_w_up, l1_w_down,
                   final_norm, s)
    return x2.reshape(b, s, d)
```

```python
import functools

import jax
import jax.numpy as jnp
import numpy as np
from jax import lax
from jax.experimental import pallas as pl
from jax.experimental.pallas import tpu as pltpu

D_MODEL = 1024
N_HEADS_A = 8
HEAD_DIM_A = 64
KV_LATENT = 256
IDX_HEADS = 8
IDX_DIM = 64
TOPK_MAX = 256
Q_BLOCK = 128
SSM_D_INNER = 1024
SSM_HEADS = 16
SSM_HEAD_DIM = SSM_D_INNER // SSM_HEADS
SSM_GROUPS = 2
SSM_STATE = 128
SSM_CONV = 4
SSM_CHUNK = 128
SC_WIDTH = D_MODEL
SC_CONV = 3
D_FF = -(-8 * D_MODEL // (3 * 256)) * 256
EPS = 1e-6

LANES = 128
SUBLANES = 8
TOKEN_TILE = 512
CONV_TOKEN_TILE = 512
KEY_CHUNK = 512
SSD_CHUNKS_PER_STEP = 2
KVT_PAD = 16
VMEM_LIMIT = 56 * 1024 * 1024

F32 = jnp.float32
BF16 = jnp.bfloat16
NEG_BIG = -1e30
F32_LOWEST = float(np.finfo(np.float32).min)
KEY_LOWEST = int(np.array(F32_LOWEST, np.float32).view(np.int32)) ^ 0x7FFFFFFF


def _vmem_full():
    return pl.BlockSpec(memory_space=pltpu.VMEM)


def _params(*sem):
    return pltpu.CompilerParams(dimension_semantics=sem, vmem_limit_bytes=VMEM_LIMIT)


def _rms(x, w):
    return x * lax.rsqrt(jnp.mean(x * x, axis=-1, keepdims=True) + EPS) * w


def _dot(a, b):
    return jnp.dot(a, b, preferred_element_type=F32)


def _dot_nt(a, b):
    return lax.dot_general(a, b, (((1,), (1,)), ((), ())), preferred_element_type=F32)


def _in0_kernel(x_ref, g_ref, wqt, wiqt, wikwt, wckv, wikw, wz, wxbc, wdt, kvn_ref,
                qt_o, iqt_o, ikwt_o, ckv_o, ckvt_o, ik_o, z_o, xbc_o, dt_o):
    xn = _rms(x_ref[...], g_ref[...]).astype(BF16)
    qt_o[...] = _dot_nt(wqt[...], xn).astype(BF16)
    iqt_o[...] = _dot_nt(wiqt[...], xn).astype(BF16)
    ikwt_o[...] = _dot_nt(wikwt[...], xn)
    c = _rms(_dot(xn, wckv[...]), kvn_ref[...])
    ckv_o[...] = c.astype(BF16)
    ones_rows = (lax.broadcasted_iota(jnp.int32, (KVT_PAD, c.shape[0]), 0) == 0).astype(F32)
    ckvt_o[0] = jnp.concatenate([c.T, ones_rows], axis=0).astype(BF16)
    ik_o[...] = _dot(xn, wikw[...]).astype(BF16)
    z_o[...] = _dot(xn, wz[...])
    xbc_o[...] = _dot(xn, wxbc[...])
    dt_o[...] = _dot(xn, wdt[...])


def _in_proj0(x2, g, in_w, kv_norm):
    n = x2.shape[0]
    cuts = np.cumsum((N_HEADS_A * HEAD_DIM_A, KV_LATENT, IDX_HEADS * IDX_DIM, IDX_DIM, IDX_HEADS,
                      SSM_D_INNER, SSM_D_INNER + 2 * SSM_GROUPS * SSM_STATE, SSM_HEADS))
    w = in_w.astype(BF16)
    wq, wckv, wiq = w[:, :cuts[0]], w[:, cuts[0]:cuts[1]], w[:, cuts[1]:cuts[2]]
    wikw = jnp.pad(w[:, cuts[2]:cuts[4]], ((0, 0), (0, LANES - IDX_DIM - IDX_HEADS)))
    wz, wxbc = w[:, cuts[4]:cuts[5]], w[:, cuts[5]:cuts[6]]
    wdt = jnp.pad(w[:, cuts[6]:cuts[7]], ((0, 0), (0, LANES - SSM_HEADS)))
    tm = KEY_CHUNK
    row = lambda c: pl.BlockSpec((tm, c), lambda i: (i, 0))
    col = lambda r: pl.BlockSpec((r, tm), lambda i: (0, i))
    hd = N_HEADS_A * HEAD_DIM_A
    out_specs = [col(hd), col(hd), col(LANES), row(KV_LATENT),
                 pl.BlockSpec((1, KV_LATENT + KVT_PAD, tm), lambda i: (i, 0, 0)),
                 row(LANES), row(SSM_D_INNER), row(wxbc.shape[1]), row(LANES)]
    sds = jax.ShapeDtypeStruct
    out_shape = [sds((hd, n), BF16), sds((hd, n), BF16), sds((LANES, n), F32), sds((n, KV_LATENT), BF16),
                 sds((n // tm, KV_LATENT + KVT_PAD, tm), BF16), sds((n, LANES), BF16), sds((n, SSM_D_INNER), F32),
                 sds((n, wxbc.shape[1]), F32), sds((n, LANES), F32)]
    return pl.pallas_call(
        _in0_kernel,
        grid=(n // tm,),
        in_specs=[row(D_MODEL)] + [_vmem_full()] * 10,
        out_specs=out_specs,
        out_shape=out_shape,
        compiler_params=_params("parallel"),
        name="in_proj0",
    )(x2, g.reshape(1, -1), wq.T, wiq.T, wikw.T, wckv, wikw, wz, wxbc, wdt, kv_norm.reshape(1, -1))


def _f2k(x):
    b = pltpu.bitcast(x, jnp.int32)
    return jnp.where(b < 0, b ^ jnp.int32(0x7FFFFFFF), b)


def _k2f(k):
    b = jnp.where(k < 0, k ^ jnp.int32(0x7FFFFFFF), k)
    return pltpu.bitcast(b, F32)


FOLD_ROWS = 4 * SUBLANES
VALUE_PROBES = 12
PROBES_PER_CHECK = 3
UNCHECKED_ROUNDS = 5


def _fold(x, op):
    return op(x.reshape(x.shape[0] // FOLD_ROWS, FOLD_ROWS, LANES), axis=0)


def _any_lane(flag):
    return jnp.max(jnp.where(flag, 1.0, 0.0))


def _dsa_kernel(qt_ref, iqt_ref, wt_ref, iqtn_ref, wtn_ref, ik_ref, ckv_ref, ckvt_ref, wukt_ref, wuvt_ref, tri_ref,
                o_ref,
                score_scr, qlat_scr, r_scr, acc_scr, m_scr, thr_scr, part_scr, stat_scr,
                *, topk, nblocks):
    i = pl.program_id(1)
    kc = KEY_CHUNK
    qb = Q_BLOCK
    nh = N_HEADS_A
    cur = i % 2
    nxt = 1 - cur
    inext = jnp.minimum(i + 1, nblocks - 1)
    nch = (i * qb) // kc + 1
    nch_next = (inext * qb) // kc + 1
    t0 = i * qb
    key_i = lax.broadcasted_iota(jnp.int32, (kc, LANES), 0)
    rel_i = key_i - lax.broadcasted_iota(jnp.int32, (kc, LANES), 1)

    wscale = (IDX_HEADS ** -0.5) * (IDX_DIM ** -0.5)

    def load_indexer_queries(iqt):
        r_scr[...] = jnp.zeros(r_scr.shape, BF16)
        for h in range(IDX_HEADS):
            r_scr[0:IDX_DIM, h * qb:(h + 1) * qb] = iqt[h * IDX_DIM:(h + 1) * IDX_DIM, :]

    def reset_partials():
        part_scr[0] = jnp.full((FOLD_ROWS, LANES), jnp.inf, F32)
        part_scr[1] = jnp.full((FOLD_ROWS, LANES), -jnp.inf, F32)
        part_scr[2] = jnp.zeros((FOLD_ROWS, LANES), F32)
        part_scr[3] = jnp.zeros((FOLD_ROWS, LANES), F32)

    def score_chunk(j, buf, wts, tq0):
        base = pl.multiple_of(j * kc, kc)
        lt = _dot(ik_ref[0, pl.ds(base, kc), :], r_scr[...])
        acc = None
        for h in range(IDX_HEADS):
            term = jnp.maximum(lt[:, h * qb:(h + 1) * qb], 0.0) * wts[h:h + 1, :]
            acc = term if acc is None else acc + term
        sc = jnp.where(rel_i <= (tq0 - base), acc, -jnp.inf)
        score_scr[buf, pl.ds(base, kc), :] = sc
        part_scr[0] = jnp.minimum(part_scr[0], _fold(acc, jnp.min))
        part_scr[1] = jnp.maximum(part_scr[1], _fold(acc, jnp.max))
        part_scr[2] = part_scr[2] + _fold(jnp.where(sc > 0.0, 1.0, 0.0), jnp.sum)
        part_scr[3] = part_scr[3] + _fold(jnp.where(sc >= 0.0, 1.0, 0.0), jnp.sum)

    def publish_stats():
        stat_scr[0:1, :] = jnp.min(part_scr[0], axis=0, keepdims=True)
        stat_scr[1:2, :] = jnp.max(part_scr[1], axis=0, keepdims=True)
        stat_scr[2:3, :] = jnp.sum(part_scr[2], axis=0, keepdims=True)
        stat_scr[3:4, :] = jnp.sum(part_scr[3], axis=0, keepdims=True)

    @pl.when(i == 0)
    def _():
        load_indexer_queries(iqt_ref[...])
        reset_partials()
        score_chunk(0, cur, wt_ref[...] * wscale, t0)
        publish_stats()

    colmin = stat_scr[0:1, :]
    colmax = stat_scr[1:2, :]
    npos = stat_scr[2:3, :].astype(jnp.int32)
    nnonneg = stat_scr[3:4, :].astype(jnp.int32)

    def count(pred):
        def body(j, acc):
            base = pl.multiple_of(j * kc, kc)
            blk = score_scr[cur, pl.ds(base, kc), :]
            return acc + _fold(jnp.where(pred(blk, base), 1.0, 0.0), jnp.sum)
        acc = lax.fori_loop(0, nch, body, jnp.zeros((FOLD_ROWS, LANES), F32))
        return jnp.sum(acc, axis=0, keepdims=True).astype(jnp.int32)

    t_pos = t0 + lax.broadcasted_iota(jnp.int32, (1, qb), 1)
    nvalid = t_pos + 1
    allsel = nvalid <= topk
    key_zero, key_tiny, key_negzero = 0, 1, -1
    at_zero = jnp.logical_and(npos < topk, nnonneg >= topk)
    above = npos >= topk
    below = nnonneg < topk
    lo0 = jnp.where(at_zero, key_zero, jnp.where(above, key_tiny, _f2k(colmin)))
    cnt0 = jnp.where(at_zero, nnonneg, jnp.where(above, npos, nvalid))
    hi0 = jnp.where(at_zero, key_zero + 1, jnp.where(below, key_negzero, _f2k(colmax) + 1))
    lo0 = jnp.where(allsel, jnp.int32(KEY_LOWEST), lo0)
    hi0 = jnp.where(allsel, jnp.int32(KEY_LOWEST + 1), hi0)
    cnt0 = jnp.where(allsel, jnp.int32(topk), cnt0)
    go0 = _any_lane(lo0 + 1 < hi0)
    cnt_hi0 = jnp.where(below, nnonneg, 0)

    def search_step(lo, hi, cnt_lo, cnt_hi, by_value):
        active = lo + 1 < hi
        mid_k = (lo & hi) + ((lo ^ hi) >> 1)
        mid_v = _f2k(0.5 * _k2f(lo) + 0.5 * _k2f(hi - 1))
        mid_v = jnp.minimum(jnp.maximum(mid_v, lo + 1), hi - 1)
        mid = jnp.where(by_value, mid_v, mid_k)
        cand = _k2f(mid)
        cnt = count(lambda blk, base: blk >= cand)
        ge = cnt >= topk
        up = jnp.logical_and(active, ge)
        dn = jnp.logical_and(active, jnp.logical_not(ge))
        hit = jnp.logical_and(active, cnt == topk)
        lo = jnp.where(up, mid, lo)
        hi = jnp.where(hit, mid + 1, jnp.where(dn, mid, hi))
        cnt_lo = jnp.where(up, cnt, cnt_lo)
        cnt_hi = jnp.where(dn, cnt, cnt_hi)
        return lo, hi, cnt_lo, cnt_hi

    def resolve_small(lo, hi, cnt_lo, cnt_hi):
        active = lo + 1 < hi
        v_below, v_hi = _k2f(lo - 1), _k2f(hi)

        def body(j, c):
            mn, mx = c
            base = pl.multiple_of(j * kc, kc)
            blk = score_scr[cur, pl.ds(base, kc), :]
            mn = jnp.minimum(mn, _fold(jnp.where(blk > v_below, blk, jnp.inf), jnp.min))
            mx = jnp.maximum(mx, _fold(jnp.where(blk < v_hi, blk, -jnp.inf), jnp.max))
            return mn, mx
        mn, mx = lax.fori_loop(0, nch, body, (jnp.full((FOLD_ROWS, LANES), jnp.inf, F32),
                                              jnp.full((FOLD_ROWS, LANES), -jnp.inf, F32)))
        mn = jnp.min(mn, axis=0, keepdims=True)
        mx = jnp.max(mx, axis=0, keepdims=True)
        inside = cnt_lo - cnt_hi
        small = jnp.logical_and(active, inside <= 2)
        want_top = (topk - cnt_hi) == 1
        twins = jnp.logical_and(mx == mn, inside == 2)
        key_t = _f2k(jnp.where(want_top, mx, mn))
        cnt_t = jnp.where(want_top, cnt_hi + 1 + jnp.where(twins, 1, 0), cnt_lo)
        return (jnp.where(small, key_t, lo), jnp.where(small, key_t + 1, hi), jnp.where(small, cnt_t, cnt_lo), cnt_hi)

    def bis_cond(c):
        return c[5] > 0.0

    def probe_round(r, c):
        for u in range(PROBES_PER_CHECK):
            c = search_step(*c, r * PROBES_PER_CHECK + u < VALUE_PROBES)
        return c

    def bis_body(c):
        r = c[4]
        st = resolve_small(*probe_round(r, c[:4]))
        return (*st, r + 1, _any_lane(st[0] + 1 < st[1]))

    first = jnp.where(go0 > 0.0, UNCHECKED_ROUNDS, 0)
    st = resolve_small(*lax.fori_loop(0, first, probe_round, (lo0, hi0, cnt0, cnt_hi0)))
    lo, _, cnt_lo, _, _, _ = lax.while_loop(bis_cond, bis_body, (*st, first, _any_lane(st[0] + 1 < st[1])))
    thr_scr[...] = _k2f(lo)
    tied = cnt_lo > topk

    @pl.when(_any_lane(tied) > 0.0)
    def _():
        thr = thr_scr[...]
        need = (topk - count(lambda blk, base: blk > thr)).astype(F32)

        def fix(j, seen):
            base = pl.multiple_of(j * kc, kc)
            blk = score_scr[cur, pl.ds(base, kc), :]
            eq = blk == thr
            eqb = jnp.where(eq, 1.0, 0.0).astype(BF16)
            ranks = []
            for r in range(kc // LANES):
                pre = _dot(tri_ref[...], eqb[r * LANES:(r + 1) * LANES, :]) + seen
                ranks.append(pre)
                seen = pre[LANES - 1:LANES, :]
            rank = jnp.concatenate(ranks, axis=0)
            score_scr[cur, pl.ds(base, kc), :] = jnp.where(eq, jnp.where(rank > need, -jnp.inf, blk), blk)
            return seen
        lax.fori_loop(0, nch, fix, jnp.zeros((1, qb), F32))

    qt = qt_ref[...]
    qscale = (HEAD_DIM_A ** -0.5) * float(np.log2(np.e))
    for p in range(nh // 2):
        r = _dot(wukt_ref[p], qt[p * LANES:(p + 1) * LANES, :])
        for v in range(2):
            h = 2 * p + v
            qlat_scr[:, h * qb:(h + 1) * qb] = (r[v * KV_LATENT:(v + 1) * KV_LATENT, :] * qscale).astype(BF16)
    load_indexer_queries(iqtn_ref[...])
    wts_next = wtn_ref[...] * wscale
    tn0 = inext * qb
    reset_partials()

    m_scr[...] = jnp.full(m_scr.shape, NEG_BIG, F32)
    acc_scr[...] = jnp.zeros(acc_scr.shape, F32)

    def sweep(j, _):
        base = pl.multiple_of(j * kc, kc)
        kv = ckv_ref[0, pl.ds(base, kc), :]
        kvt = ckvt_ref[0, j]
        bias = jnp.where(score_scr[cur, pl.ds(base, kc), :] >= thr_scr[...], 0.0, NEG_BIG)
        s = _dot(kv, qlat_scr[...])
        score_chunk(j, nxt, wts_next, tn0)
        ps, alphas = [], []
        for h in range(nh):
            sh = s[:, h * qb:(h + 1) * qb] + bias
            m_old = m_scr[h:h + 1, :]
            m_new = jnp.maximum(m_old, jnp.max(sh, axis=0, keepdims=True))
            ps.append(jnp.exp2(sh - m_new).astype(BF16))
            alphas.append(jnp.exp2(m_old - m_new))
            m_scr[h:h + 1, :] = m_new
        pv = _dot(kvt, jnp.concatenate(ps, axis=1))
        acc_scr[...] = acc_scr[...] * jnp.concatenate(alphas, axis=1) + pv
        return 0

    lax.fori_loop(0, nch, sweep, 0)

    @pl.when(nch_next > nch)
    def _():
        score_chunk(nch, nxt, wts_next, tn0)

    publish_stats()

    inv_l = 1.0 / acc_scr[KV_LATENT:KV_LATENT + 1, :]
    outs = []
    for h in range(nh):
        cols = slice(h * qb, (h + 1) * qb)
        olat = (acc_scr[0:KV_LATENT, cols] * inv_l[:, cols]).astype(BF16)
        outs.append(_dot(wuvt_ref[h], olat))
    o_ref[...] = jnp.concatenate(outs, axis=0).T.astype(BF16)


def _dsa(qt, iqt, ikwt, ik, ckv, ckvt, w_uk, w_uv, b, s):
    kc = KEY_CHUNK
    nb = s // Q_BLOCK
    nchunks = s // kc
    topk = min(TOPK_MAX, s // 4)
    hd = N_HEADS_A * HEAD_DIM_A
    wukt = jnp.swapaxes(w_uk, 1, 2).reshape(N_HEADS_A // 2, 2, KV_LATENT, HEAD_DIM_A)
    eye2 = jnp.eye(2, dtype=F32)
    wukt = jnp.einsum('pvcd,vu->pvcud', wukt, eye2).reshape(N_HEADS_A // 2, 2 * KV_LATENT, 2 * HEAD_DIM_A)
    wuvt = jnp.swapaxes(w_uv, 1, 2)
    this_blk = lambda bi, i: bi * nb + i
    next_blk = lambda bi, i: bi * nb + jnp.minimum(i + 1, nb - 1)
    qcol = lambda r, blk: pl.BlockSpec((r, Q_BLOCK), lambda bi, i: (0, blk(bi, i)))
    wrow = lambda blk: pl.BlockSpec((IDX_HEADS, Q_BLOCK), lambda bi, i: (IDX_DIM // IDX_HEADS, blk(bi, i)))
    kern = functools.partial(_dsa_kernel, topk=topk, nblocks=nb)
    return pl.pallas_call(
        kern,
        grid=(b, nb),
        in_specs=[qcol(hd, this_blk), qcol(IDX_HEADS * IDX_DIM, this_blk), wrow(this_blk),
                  qcol(IDX_HEADS * IDX_DIM, next_blk), wrow(next_blk),
                  pl.BlockSpec((1, s, LANES), lambda bi, i: (bi, 0, 0)),
                  pl.BlockSpec((1, s, KV_LATENT), lambda bi, i: (bi, 0, 0)),
                  pl.BlockSpec((1, nchunks, KV_LATENT + KVT_PAD, kc), lambda bi, i: (bi, 0, 0, 0)),
                  _vmem_full(), _vmem_full(), _vmem_full()],
        out_specs=pl.BlockSpec((Q_BLOCK, hd), lambda bi, i: (bi * nb + i, 0)),
        out_shape=jax.ShapeDtypeStruct((b * s, hd), BF16),
        scratch_shapes=[
            pltpu.VMEM((2, s, Q_BLOCK), F32),
            pltpu.VMEM((KV_LATENT, N_HEADS_A * Q_BLOCK), BF16),
            pltpu.VMEM((LANES, IDX_HEADS * Q_BLOCK), BF16),
            pltpu.VMEM((KV_LATENT + KVT_PAD, N_HEADS_A * Q_BLOCK), F32),
            pltpu.VMEM((N_HEADS_A, Q_BLOCK), F32),
            pltpu.VMEM((1, Q_BLOCK), F32),
            pltpu.VMEM((4, FOLD_ROWS, Q_BLOCK), F32),
            pltpu.VMEM((SUBLANES, Q_BLOCK), F32),
        ],
        compiler_params=_params("parallel", "arbitrary"),
        name="dsa_attention",
    )(qt, iqt, ikwt, iqt, ikwt, ik.reshape(b, s, LANES), ckv.reshape(b, s, KV_LATENT),
      ckvt.reshape(b, nchunks, KV_LATENT + KVT_PAD, kc), wukt.astype(BF16), wuvt.astype(BF16),
      jnp.tril(jnp.ones((LANES, LANES), BF16)))


def _causal_conv(x, w, halo):
    taps = w.shape[0]
    top_row = lax.broadcasted_iota(jnp.int32, halo.shape, 0)
    y = x * w[taps - 1:taps, :]
    for k in range(1, taps):
        xk = pltpu.roll(x, k, 0)
        top = jnp.where(top_row < k, pltpu.roll(halo, k, 0), xk[:SUBLANES, :])
        xk = jnp.concatenate([top, xk[SUBLANES:, :]], axis=0)
        y = y + xk * w[taps - 1 - k:taps - k, :]
    return y


def _split3(x):
    p1 = x.astype(BF16)
    r = x - p1.astype(F32)
    p2 = r.astype(BF16)
    return p1, p2, (r - p2.astype(F32)).astype(BF16)


def _ssd_kernel(xbc_ref, z_ref, dt_ref, cw_ref, cb_ref, dtb_ref, alog_ref, dexp_ref, nw_ref, e_ref,
                o_ref, halo_scr, state_scr):
    @pl.when(pl.program_id(1) == 0)
    def _():
        halo_scr[...] = jnp.zeros(halo_scr.shape, F32)
        state_scr[...] = jnp.zeros(state_scr.shape, F32)

    for sub in range(SSD_CHUNKS_PER_STEP):
        _ssd_chunk(slice(sub * SSM_CHUNK, (sub + 1) * SSM_CHUNK), xbc_ref, z_ref, dt_ref, cw_ref, cb_ref, dtb_ref,
                   alog_ref, dexp_ref, nw_ref, e_ref, o_ref, halo_scr, state_scr)


def _ssd_chunk(rows, xbc_ref, z_ref, dt_ref, cw_ref, cb_ref, dtb_ref, alog_ref, dexp_ref, nw_ref, e_ref,
               o_ref, halo_scr, state_scr):
    L = SSM_CHUNK
    n = SSM_STATE
    gw = SSM_D_INNER // SSM_GROUPS

    xbc = xbc_ref[0, rows, :]
    conv = _causal_conv(xbc, cw_ref[...], halo_scr[...]) + cb_ref[...]
    halo_scr[...] = xbc[L - SUBLANES:, :]
    act = conv * jax.nn.sigmoid(conv)
    xs = act[:, :SSM_D_INNER]
    bm = act[:, SSM_D_INNER:SSM_D_INNER + SSM_GROUPS * n]
    cm = act[:, SSM_D_INNER + SSM_GROUPS * n:]

    dt = jax.nn.softplus(dt_ref[0, rows, :] + dtb_ref[...])
    a = dt * (-jnp.exp(alog_ref[...]))
    ri = lax.broadcasted_iota(jnp.int32, (L, L), 0)
    ci = lax.broadcasted_iota(jnp.int32, (L, L), 1)
    tri = ri >= ci
    tri_b = jnp.where(tri, 1.0, 0.0).astype(BF16)
    cs3 = _dot(tri_b, jnp.concatenate(_split3(a), axis=1))
    a_cs = cs3[:, :LANES] + cs3[:, LANES:2 * LANES] + cs3[:, 2 * LANES:]
    a_cs_t = a_cs.T
    w_end = dt * jnp.exp(a_cs[L - 1:L, :] - a_cs)
    grow_c = jnp.exp(a_cs)
    pieces = [p for arr in (dt, w_end, grow_c) for p in _split3(arr)]
    ex = _dot(jnp.concatenate(pieces, axis=0), e_ref[...])
    dt_e, wend_e, grow = (ex[3 * i * L:(3 * i + 1) * L] + ex[(3 * i + 1) * L:(3 * i + 2) * L]
                          + ex[(3 * i + 2) * L:(3 * i + 3) * L] for i in range(3))
    xdt_b = (xs * dt_e).astype(BF16)
    xend_b = (xs * wend_e).astype(BF16)
    lane = lax.broadcasted_iota(jnp.int32, (L, LANES), 1)

    y_parts = []
    for g in range(SSM_GROUPS):
        bg = bm[:, g * n:(g + 1) * n]
        cg = cm[:, g * n:(g + 1) * n].astype(BF16)
        cbm = _dot_nt(cg, bg.astype(BF16))
        for pr in range(gw // LANES):
            col = g * gw + pr * LANES
            xpair = xdt_b[:, col:col + LANES]
            outs = []
            for v in range(2):
                h = (col // SSM_HEAD_DIM) + v
                seg = a_cs[:, h:h + 1] - a_cs_t[h:h + 1, :]
                dec = jnp.exp(jnp.where(tri, seg, -jnp.inf))
                outs.append(_dot((cbm * dec).astype(BF16), xpair))
            y_parts.append(jnp.where(lane < SSM_HEAD_DIM, outs[0], outs[1]))
    y = jnp.concatenate(y_parts, axis=1)

    offs = []
    for g in range(SSM_GROUPS):
        sl = slice(g * gw, (g + 1) * gw)
        bg_t = bm[:, g * n:(g + 1) * n].T.astype(BF16)
        cg = cm[:, g * n:(g + 1) * n].astype(BF16)
        st = state_scr[g]
        offs.append(_dot(cg, st.astype(BF16)))
        state_scr[g] = st * grow[L - 1:L, sl] + _dot(bg_t, xend_b[:, sl])
    y = y + jnp.concatenate(offs, axis=1) * grow + xs * dexp_ref[...]

    z = z_ref[0, rows, :]
    y = y * (z * jax.nn.sigmoid(z))
    nw = nw_ref[...]
    outs = []
    for g in range(SSM_GROUPS):
        sl = slice(g * gw, (g + 1) * gw)
        outs.append(_rms(y[:, sl], nw[:, sl]))
    o_ref[0, rows, :] = jnp.concatenate(outs, axis=1).astype(BF16)


def _ssd(xbc, z, dtp, conv_w, conv_b, dt_bias, a_log, d, norm_w, b, s):
    L = SSM_CHUNK * SSD_CHUNKS_PER_STEP
    cx = SSM_D_INNER + 2 * SSM_GROUPS * SSM_STATE
    pad = LANES - SSM_HEADS
    expand = jnp.repeat(jnp.eye(SSM_HEADS, dtype=F32), SSM_HEAD_DIM, axis=1)
    expand = jnp.pad(expand, ((0, pad), (0, 0)))
    blk = lambda c: pl.BlockSpec((1, L, c), lambda bi, i: (bi, i, 0))
    return pl.pallas_call(
        _ssd_kernel,
        grid=(b, s // L),
        in_specs=[blk(cx), blk(SSM_D_INNER), blk(LANES)] + [_vmem_full()] * 7,
        out_specs=blk(SSM_D_INNER),
        out_shape=jax.ShapeDtypeStruct((b, s, SSM_D_INNER), BF16),
        scratch_shapes=[pltpu.VMEM((SUBLANES, cx), F32),
                        pltpu.VMEM((SSM_GROUPS, SSM_STATE, SSM_D_INNER // SSM_GROUPS), F32)],
        compiler_params=_params("parallel", "arbitrary"),
        name="ssd_mixer",
    )(xbc.reshape(b, s, cx), z.reshape(b, s, SSM_D_INNER), dtp.reshape(b, s, LANES),
      conv_w, conv_b.reshape(1, -1), jnp.pad(dt_bias, (0, pad)).reshape(1, -1),
      jnp.pad(a_log, (0, pad)).reshape(1, -1), jnp.repeat(d, SSM_HEAD_DIM).reshape(1, -1),
      norm_w.reshape(1, -1), expand.astype(BF16))


def _swiglu(x, g, wg, wu, wd):
    hn = _rms(x, g).astype(BF16)
    a = _dot(hn, wg[...])
    u = _dot(hn, wu[...])
    act = (a * jax.nn.sigmoid(a) * u).astype(BF16)
    return x + _dot(act, wd[...])


def _mix_ffn_kernel(x_ref, ya_ref, yb_ref, wa, wb, g_ref, wg, wu, wd, o_ref):
    x = x_ref[...] + _dot(ya_ref[...], wa[...]) + _dot(yb_ref[...], wb[...])
    o_ref[...] = _swiglu(x, g_ref[...], wg, wu, wd)


def _mix_ffn(x2, ya, yb, out_w, g, w_gate, w_up, w_down):
    n = x2.shape[0]
    tm = TOKEN_TILE
    ca = N_HEADS_A * HEAD_DIM_A
    w = out_w.astype(BF16)
    row = lambda c: pl.BlockSpec((tm, c), lambda i: (i, 0))
    return pl.pallas_call(
        _mix_ffn_kernel,
        grid=(n // tm,),
        in_specs=[row(D_MODEL), row(ca), row(SSM_D_INNER)] + [_vmem_full()] * 6,
        out_specs=row(D_MODEL),
        out_shape=jax.ShapeDtypeStruct((n, D_MODEL), F32),
        compiler_params=_params("parallel"),
        name="mix_ffn",
    )(x2, ya, yb, w[:ca], w[ca:], g.reshape(1, -1), w_gate.astype(BF16), w_up.astype(BF16), w_down.astype(BF16))


def _conv_ffn_kernel(x_ref, xh_ref, gc_ref, wb, wc, wv, cw_ref, wo, g_ref, wg, wu, wd, fn_ref, o_ref,
                     *, tiles_per_seq):
    i = pl.program_id(0)
    x = x_ref[...]
    xn = _rms(x, gc_ref[...]).astype(BF16)
    u = _dot(xn, wc[...]) * _dot(xn, wv[...])
    xhn = _rms(xh_ref[...], gc_ref[...]).astype(BF16)
    halo = jnp.where(i % tiles_per_seq == 0, 0.0, _dot(xhn, wc[...]) * _dot(xhn, wv[...]))
    y = (_dot(xn, wb[...]) * _causal_conv(u, cw_ref[...], halo)).astype(BF16)
    x = x + _dot(y, wo[...])
    o_ref[...] = _rms(_swiglu(x, g_ref[...], wg, wu, wd), fn_ref[...])


def _conv_ffn(x2, norm_w, in_w, conv_w, out_w, g, w_gate, w_up, w_down, final_norm, s):
    n = x2.shape[0]
    tm = CONV_TOKEN_TILE
    w = in_w.astype(BF16)
    row = pl.BlockSpec((tm, D_MODEL), lambda i: (i, 0))
    per = tm // SUBLANES
    halo = pl.BlockSpec((SUBLANES, D_MODEL), lambda i: (jnp.maximum(i * per - 1, 0), 0))
    return pl.pallas_call(
        functools.partial(_conv_ffn_kernel, tiles_per_seq=s // tm),
        grid=(n // tm,),
        in_specs=[row, halo] + [_vmem_full()] * 11,
        out_specs=row,
        out_shape=jax.ShapeDtypeStruct((n, D_MODEL), F32),
        compiler_params=_params("parallel"),
        name="conv_ffn",
    )(x2, x2, norm_w.reshape(1, -1), w[:, :SC_WIDTH], w[:, SC_WIDTH:2 * SC_WIDTH], w[:, 2 * SC_WIDTH:],
      conv_w, out_w.astype(BF16), g.reshape(1, -1), w_gate.astype(BF16), w_up.astype(BF16), w_down.astype(BF16),
      final_norm.reshape(1, -1))


def kernel(x, l0_attn_norm, l0_in_w, l0_kv_norm, l0_w_uk, l0_w_uv, l0_conv_w, l0_conv_b, l0_dt_bias,
           l0_A_log, l0_D, l0_ssm_norm, l0_out_w, l0_ffn_norm, l0_w_gate, l0_w_up, l0_w_down,
           l1_conv_norm, l1_in_w, l1_conv_w, l1_out_w, l1_ffn_norm, l1_w_gate, l1_w_up, l1_w_down,
           final_norm):
    b, s, d = x.shape
    assert d == D_MODEL and s % TOKEN_TILE == 0 and s % CONV_TOKEN_TILE == 0 and s % KEY_CHUNK == 0 and s % (SSM_CHUNK * SSD_CHUNKS_PER_STEP) == 0
    x2 = x.reshape(b * s, d)
    qt, iqt, ikwt, ckv, ckvt, ik, z, xbc, dtp = _in_proj0(x2, l0_attn_norm, l0_in_w, l0_kv_norm)
    ya = _dsa(qt, iqt, ikwt, ik, ckv, ckvt, l0_w_uk, l0_w_uv, b, s)
    yb = _ssd(xbc, z, dtp, l0_conv_w, l0_conv_b, l0_dt_bias, l0_A_log, l0_D, l0_ssm_norm, b, s)
    x2 = _mix_ffn(x2, ya, yb.reshape(b * s, -1), l0_out_w, l0_ffn_norm, l0_w_gate, l0_w_up, l0_w_down)
    x2 = _conv_ffn(x2, l1_conv_norm, l1_in_w, l1_conv_w, l1_out_w, l1_ffn_norm, l1_w_gate, l1_w_up, l1_w_down,
                   final_norm, s)
    return x2.reshape(b, s, d)
```
